```python
import math
import jax, jax.numpy as jnp
from jax import lax
import numpy as np


D_MODEL = 1024
BATCH = 2
SEQ = 8192
DEPTH = 2
DEC_BATCH = 128
DEC_SEQ = 1
PAST_LEN = 2048
PAGE_SIZE = 128

N_A_LAYERS = DEPTH // 2
N_B_LAYERS = DEPTH - N_A_LAYERS
HG_HEADS = 8
HG_DK = D_MODEL // HG_HEADS
HG_DV = D_MODEL // HG_HEADS
CHUNK = 64
DA_HEADS = 4
DA_HEAD_DIM = D_MODEL // (2 * DA_HEADS)
Q_BLOCK = 128
N_BUCKETS = 32
MAX_DISTANCE = 128
D_FF = ((-(-8 * D_MODEL // 3)) + 255) // 256 * 256
EPS = 1e-6

kernel_name = 'yoco_hgrn2_diffattn_decode_step'


def rmsnorm(x, g):
    xf = x.astype(jnp.float32)
    y = xf * lax.rsqrt(jnp.mean(xf * xf, axis=-1, keepdims=True) + EPS)
    return (y * g.astype(jnp.float32)).astype(x.dtype)


def swiglu_ffn(x, w_gate_up, w_down):
    g, u = jnp.split(x @ w_gate_up, 2, axis=-1)
    return (jax.nn.silu(g) * u) @ w_down


def hgrn_lower_bounds(lower_bound):
    return jnp.cumsum(jax.nn.softmax(lower_bound.astype(jnp.float32), axis=0), axis=0)


def hgrn_chunked(q, k, v, logf):
    B, S, H, K = q.shape
    V = v.shape[-1]
    n = S // CHUNK
    def to_chunks(a):
        return jnp.moveaxis(a.reshape((B, n, CHUNK) + a.shape[2:]), 1, 0)
    mask = jnp.tril(jnp.ones((CHUNK, CHUNK), dtype=bool))[None, :, :, None, None]
    def step(state, inp):
        qc, kc, vc, lc = inp
        b = jnp.cumsum(lc, axis=1)
        diff = b[:, :, None] - b[:, None, :]
        decay = jnp.exp(jnp.where(mask, diff, -jnp.inf))
        scores = jnp.einsum('bthk,bshk,btshk->bhts', qc, kc, decay)
        o = (jnp.einsum('bhts,bshv->bthv', scores, vc)
             + jnp.einsum('bthk,bhkv->bthv', qc * jnp.exp(b), state))
        b_last = b[:, -1]
        new_state = (jnp.exp(b_last)[..., None] * state
                     + jnp.einsum('bshk,bshv->bhkv', kc * jnp.exp(b_last[:, None] - b), vc))
        return new_state, o
    s0 = jnp.zeros((B, H, K, V), jnp.float32)
    s_final, o = lax.scan(step, s0, (to_chunks(q), to_chunks(k), to_chunks(v), to_chunks(logf)))
    return jnp.moveaxis(o, 0, 1).reshape(B, S, H, V), s_final


def hgrn_recurrent(q, k, v, logf, s0):
    def step(state, inp):
        qt, kt, vt, lt = inp
        state = jnp.exp(lt)[..., None] * state + kt[..., None] * vt[..., None, :]
        return state, jnp.einsum('bhk,bhkv->bhv', qt, state)
    xs = (jnp.moveaxis(q, 1, 0), jnp.moveaxis(k, 1, 0), jnp.moveaxis(v, 1, 0), jnp.moveaxis(logf, 1, 0))
    s_final, o = lax.scan(step, s0.astype(jnp.float32), xs)
    return jnp.moveaxis(o, 0, 1), s_final


def hgrn_mixer(xn, w_in, lb, gnorm, w_out, s0):
    q, f, i, g = jnp.split(xn @ w_in, 4, axis=-1)
    fg = lb + (1.0 - lb) * jax.nn.sigmoid(f.astype(jnp.float32))
    kshape = xn.shape[:-1] + (HG_HEADS, HG_DK)
    qh = jax.nn.silu(q.astype(jnp.float32)).reshape(kshape)
    kh = (1.0 - fg).reshape(kshape)
    logf = jnp.log(fg).reshape(kshape)
    vh = i.astype(jnp.float32).reshape(xn.shape[:-1] + (HG_HEADS, HG_DV))
    if s0 is None:
        o, s_new = hgrn_chunked(qh, kh, vh, logf)
    else:
        o, s_new = hgrn_recurrent(qh, kh, vh, logf, s0)
    o = rmsnorm(o, gnorm.reshape(HG_HEADS, HG_DV)).reshape(xn.shape[:-1] + (D_MODEL,))
    o = o * jax.nn.silu(g.astype(jnp.float32))
    return o.astype(xn.dtype) @ w_out, s_new


def t5_bucket(q_pos, k_pos):
    n = jnp.maximum(q_pos[:, None] - k_pos[None, :], 0)
    max_exact = N_BUCKETS // 2
    nf = jnp.maximum(n, 1).astype(jnp.float32)
    large = max_exact + (jnp.log(nf / max_exact) / math.log(MAX_DISTANCE / max_exact)
                         * (N_BUCKETS - max_exact)).astype(jnp.int32)
    large = jnp.minimum(large, N_BUCKETS - 1)
    return jnp.where(n < max_exact, n, large)


def rel_bias_for(rel_bias, q_pos, k_pos):
    return jnp.transpose(rel_bias[t5_bucket(q_pos, k_pos)], (2, 0, 1)).astype(jnp.float32)


def shared_kv(h, kv_norm, w_kv):
    k, v = jnp.split(rmsnorm(h, kv_norm) @ w_kv, 2, axis=-1)
    shp = h.shape[:-1] + (DA_HEADS, 2 * DA_HEAD_DIM)
    return k.reshape(shp), v.reshape(shp)


def split_qk(a):
    return a.reshape(a.shape[:-1] + (2, DA_HEAD_DIM))


def diff_combine(s, lam, valid):
    p = jax.nn.softmax(jnp.where(valid, s, -jnp.inf), axis=-1)
    return p[:, :, 0] - lam * p[:, :, 1]


def diff_attn_prompt(q, k, v, lam, rel_bias):
    B, S = q.shape[:2]
    nb = S // Q_BLOCK
    scale = DA_HEAD_DIM ** -0.5
    k_pos = jnp.arange(S, dtype=jnp.int32)
    qb = jnp.moveaxis(q.reshape((B, nb, Q_BLOCK) + q.shape[2:]), 1, 0)
    def block(args):
        q_blk, start = args
        q_pos = start + jnp.arange(Q_BLOCK, dtype=jnp.int32)
        s = jnp.einsum('bqhcd,bkhcd->bhcqk', q_blk, k).astype(jnp.float32) * scale
        s = s + rel_bias_for(rel_bias, q_pos, k_pos)[None, :, None]
        a = diff_combine(s, lam, k_pos[None, :] <= q_pos[:, None])
        return jnp.einsum('bhqk,bkhv->bqhv', a.astype(v.dtype), v)
    o = lax.map(block, (qb, jnp.arange(nb, dtype=jnp.int32) * Q_BLOCK))
    return jnp.moveaxis(o, 0, 1).reshape(B, S, DA_HEADS, 2 * DA_HEAD_DIM)


def diff_attn_sample(q, k_new, v_new, k_past, v_past, lam, rel_bias):
    T = q.shape[1]
    P = k_past.shape[1]
    scale = DA_HEAD_DIM ** -0.5
    q_pos = P + jnp.arange(T, dtype=jnp.int32)
    k_pos = jnp.arange(P + T, dtype=jnp.int32)
    s = jnp.concatenate([jnp.einsum('bqhcd,bkhcd->bhcqk', q, k_past),
                         jnp.einsum('bqhcd,bkhcd->bhcqk', q, k_new)], axis=-1).astype(jnp.float32) * scale
    s = s + rel_bias_for(rel_bias, q_pos, k_pos)[None, :, None]
    a = diff_combine(s, lam, k_pos[None, :] <= q_pos[:, None]).astype(v_new.dtype)
    return (jnp.einsum('bhqk,bkhv->bqhv', a[..., :P], v_past)
            + jnp.einsum('bhqk,bkhv->bqhv', a[..., P:], v_new))


def run_trunk(x, hg_state, k_past, v_past, w_in_a, lower_bound, gnorm_a, w_out_a, w_q_b,
              lambda_q1, lambda_k1, lambda_q2, lambda_k2, subln_b, w_out_b, kv_norm, w_kv,
              rel_bias, norm_mix, norm_ffn, w_gate_up, w_down, norm_final):
    lbs = hgrn_lower_bounds(lower_bound)
    new_states = []
    k_sh = None
    v_sh = None
    for l in range(DEPTH):
        xn = rmsnorm(x, norm_mix[l])
        if l < N_A_LAYERS:
            h, s_new = hgrn_mixer(xn, w_in_a[l], lbs[l], gnorm_a[l], w_out_a[l],
                                  None if hg_state is None else hg_state[l])
            new_states.append(s_new)
        else:
            j = l - N_A_LAYERS
            q = (xn @ w_q_b[j]).reshape(x.shape[:-1] + (DA_HEADS, 2, DA_HEAD_DIM))
            lam_init = 0.8 - 0.6 * math.exp(-0.3 * l)
            lam = (jnp.exp(jnp.sum(lambda_q1[j] * lambda_k1[j]).astype(jnp.float32))
                   - jnp.exp(jnp.sum(lambda_q2[j] * lambda_k2[j]).astype(jnp.float32)) + lam_init)
            if k_past is None:
                o = diff_attn_prompt(q, split_qk(k_sh), v_sh, lam, rel_bias)
            else:
                o = diff_attn_sample(q, split_qk(k_sh), v_sh, split_qk(k_past), v_past, lam, rel_bias)
            o = rmsnorm(o, subln_b[j]) * (1.0 - lam_init)
            h = o.reshape(x.shape[:-1] + (D_MODEL,)).astype(x.dtype) @ w_out_b[j]
        x = x + h
        x = x + swiglu_ffn(rmsnorm(x, norm_ffn[l]), w_gate_up[l], w_down[l])
        if l == N_A_LAYERS - 1:
            k_sh, v_sh = shared_kv(x, kv_norm, w_kv)
    return rmsnorm(x, norm_final), k_sh, v_sh, jnp.stack(new_states)


def setup_inputs(seed: int = 0) -> dict:
    key = jax.random.key(seed)
    ks = jax.random.split(key, 26)
    n_pages = PAST_LEN // PAGE_SIZE
    n_used = DEC_BATCH * n_pages
    n_pool = n_used + max(1, n_used // 4)
    f32 = jnp.float32
    def nrm(k, shape, scale):
        return jax.random.normal(k, shape, f32) * scale
    def gain(k, shape):
        return 1.0 + 0.02 * jax.random.normal(k, shape, f32)
    kv_shape = (n_pool, PAGE_SIZE, DA_HEADS, 2 * DA_HEAD_DIM)
    page_table = jax.random.permutation(ks[5], n_pool)[:n_used].reshape(DEC_BATCH, n_pages).astype(jnp.int32)
    return {
        'x_prompt': nrm(ks[0], (BATCH, SEQ, D_MODEL), 1.0),
        'x_sample': nrm(ks[1], (DEC_BATCH, DEC_SEQ, D_MODEL), 1.0),
        'cache_k': nrm(ks[2], kv_shape, 1.0),
        'cache_v': nrm(ks[3], kv_shape, 1.0),
        'state_hgrn': nrm(ks[4], (N_A_LAYERS, DEC_BATCH, HG_HEADS, HG_DK, HG_DV), 0.3),
        'page_table': page_table,
        'w_in_a': nrm(ks[6], (N_A_LAYERS, D_MODEL, 4 * D_MODEL), D_MODEL ** -0.5),
        'lower_bound': nrm(ks[7], (N_A_LAYERS + 1, D_MODEL), 0.1),
        'gnorm_a': gain(ks[8], (N_A_LAYERS, D_MODEL)),
        'w_out_a': nrm(ks[9], (N_A_LAYERS, D_MODEL, D_MODEL), D_MODEL ** -0.5),
        'w_q_b': nrm(ks[10], (N_B_LAYERS, D_MODEL, D_MODEL), D_MODEL ** -0.5),
        'lambda_q1': nrm(ks[11], (N_B_LAYERS, DA_HEAD_DIM), 0.1),
        'lambda_k1': nrm(ks[12], (N_B_LAYERS, DA_HEAD_DIM), 0.1),
        'lambda_q2': nrm(ks[13], (N_B_LAYERS, DA_HEAD_DIM), 0.1),
        'lambda_k2': nrm(ks[14], (N_B_LAYERS, DA_HEAD_DIM), 0.1),
        'subln_b': gain(ks[15], (N_B_LAYERS, 2 * DA_HEAD_DIM)),
        'w_out_b': nrm(ks[16], (N_B_LAYERS, D_MODEL, D_MODEL), D_MODEL ** -0.5),
        'kv_norm': gain(ks[17], (D_MODEL,)),
        'w_kv': nrm(ks[18], (D_MODEL, 2 * D_MODEL), D_MODEL ** -0.5),
        'rel_bias': nrm(ks[19], (N_BUCKETS, DA_HEADS), 0.5),
        'norm_mix': gain(ks[20], (DEPTH, D_MODEL)),
        'norm_ffn': gain(ks[21], (DEPTH, D_MODEL)),
        'w_gate_up': nrm(ks[22], (DEPTH, D_MODEL, 2 * D_FF), D_MODEL ** -0.5),
        'w_down': nrm(ks[23], (DEPTH, D_FF, D_MODEL), D_FF ** -0.5),
        'norm_final': gain(ks[24], (D_MODEL,)),
    }


def reference(x_prompt, x_sample, cache_k, cache_v, state_hgrn, page_table, w_in_a, lower_bound,
              gnorm_a, w_out_a, w_q_b, lambda_q1, lambda_k1, lambda_q2, lambda_k2, subln_b, w_out_b,
              kv_norm, w_kv, rel_bias, norm_mix, norm_ffn, w_gate_up, w_down, norm_final):
    y_prompt, new_k_prompt, new_v_prompt, new_state_prompt = run_trunk(
        x_prompt, None, None, None, w_in_a, lower_bound, gnorm_a, w_out_a, w_q_b,
        lambda_q1, lambda_k1, lambda_q2, lambda_k2, subln_b, w_out_b, kv_norm, w_kv,
        rel_bias, norm_mix, norm_ffn, w_gate_up, w_down, norm_final)
    nb = page_table.shape[0]
    k_past = cache_k[page_table].reshape(nb, -1, DA_HEADS, 2 * DA_HEAD_DIM)
    v_past = cache_v[page_table].reshape(nb, -1, DA_HEADS, 2 * DA_HEAD_DIM)
    y_sample, new_k_sample, new_v_sample, new_state_sample = run_trunk(
        x_sample, state_hgrn, k_past, v_past, w_in_a, lower_bound, gnorm_a, w_out_a, w_q_b,
        lambda_q1, lambda_k1, lambda_q2, lambda_k2, subln_b, w_out_b, kv_norm, w_kv,
        rel_bias, norm_mix, norm_ffn, w_gate_up, w_down, norm_final)
    return (y_prompt, y_sample, new_k_prompt, new_v_prompt, new_state_prompt, new_k_sample, new_v_sample, new_state_sample)
```

```python
import functools
import math

import numpy as np
import jax
import jax.numpy as jnp
from jax import lax
from jax.experimental import pallas as pl
from jax.experimental.pallas import tpu as pltpu

F32 = jnp.float32
BF16 = jnp.bfloat16
EPS = 1e-6
NEG_BIG = -1e30

HG_HEADS = 8
DA_HEADS = 4
N_BUCKETS = 32
MAX_DISTANCE = 128
VMEM_LIMIT = 56 * 1024 * 1024

_NT = (((1,), (1,)), ((), ()))
_TN = (((0,), (0,)), ((), ()))


def _cparams(sem):
    return pltpu.CompilerParams(dimension_semantics=sem, vmem_limit_bytes=VMEM_LIMIT)


def _sigmoid(x):
    return 1.0 / (1.0 + jnp.exp(-x))


def _silu(x):
    return x * _sigmoid(x)


def _rms(x, g):
    ms = jnp.mean(x * x, axis=-1, keepdims=True)
    return x * lax.rsqrt(ms + EPS) * g


def _norm_mm_kernel(x_ref, g_ref, *refs, group_outs, scale):
    n_groups = len(group_outs)
    w_refs = refs[:n_groups]
    out_refs = refs[n_groups:-1]
    xn_ref = refs[-1]

    @pl.when(pl.program_id(1) == 0)
    def _():
        xn_ref[...] = _rms(x_ref[...], g_ref[...]).astype(BF16)

    xn = xn_ref[...]
    o = 0
    for gi in range(n_groups):
        acc = jnp.dot(xn, w_refs[gi][...].astype(BF16), preferred_element_type=F32)
        if scale != 1.0:
            acc = acc * scale
        for _ in group_outs[gi]:
            out_refs[o][...] = acc.astype(out_refs[o].dtype)
            o += 1


def norm_matmul(x, g, w, group_outs, *, tm, tn, scale=1.0, name):
    m, d = x.shape
    n_groups = len(group_outs)
    ng = w.shape[1] // n_groups
    tm = min(tm, m)
    tn = min(tn, ng)
    nj = ng // tn
    in_specs = [pl.BlockSpec((tm, d), lambda i, j: (i, 0)),
                pl.BlockSpec((1, d), lambda i, j: (0, 0))]
    for gi in range(n_groups):
        in_specs.append(pl.BlockSpec((d, tn), functools.partial(lambda i, j, off: (0, off + j), off=gi * nj)))
    out_shape, out_specs = [], []
    for dts in group_outs:
        for dt in dts:
            out_shape.append(jax.ShapeDtypeStruct((m, ng), dt))
            out_specs.append(pl.BlockSpec((tm, tn), lambda i, j: (i, j)))
    return pl.pallas_call(
        functools.partial(_norm_mm_kernel, group_outs=group_outs, scale=scale),
        grid=(m // tm, nj),
        in_specs=in_specs,
        out_specs=out_specs,
        out_shape=out_shape,
        scratch_shapes=[pltpu.VMEM((tm, d), BF16)],
        compiler_params=_cparams(("parallel", "arbitrary")),
        name=name,
    )(x, g.reshape(1, d), *([w] * n_groups))


def _mm_res_kernel(a_ref, w_ref, r_ref, o_ref):
    o_ref[...] = r_ref[...] + jnp.dot(a_ref[...], w_ref[...].astype(BF16), preferred_element_type=F32)


def matmul_res(a, w, res, *, tm, name):
    m, k = a.shape
    n = w.shape[1]
    tm = min(tm, m)
    return pl.pallas_call(
        _mm_res_kernel,
        grid=(m // tm,),
        in_specs=[pl.BlockSpec((tm, k), lambda i: (i, 0)),
                  pl.BlockSpec((k, n), lambda i: (0, 0)),
                  pl.BlockSpec((tm, n), lambda i: (i, 0))],
        out_specs=pl.BlockSpec((tm, n), lambda i: (i, 0)),
        out_shape=jax.ShapeDtypeStruct((m, n), F32),
        compiler_params=_cparams(("parallel",)),
        name=name,
    )(a, w, res)


def _ffn_kernel(x_ref, g_ref, wg_ref, wu_ref, wd_ref, gf_ref, o_ref, xn_ref, acc_ref, *, final_norm):
    j = pl.program_id(1)

    @pl.when(j == 0)
    def _():
        xn_ref[...] = _rms(x_ref[...], g_ref[...]).astype(BF16)
        acc_ref[...] = jnp.zeros_like(acc_ref)

    xn = xn_ref[...]
    gt = jnp.dot(xn, wg_ref[...].astype(BF16), preferred_element_type=F32)
    ut = jnp.dot(xn, wu_ref[...].astype(BF16), preferred_element_type=F32)
    hid = (_silu(gt) * ut).astype(BF16)
    acc_ref[...] += jnp.dot(hid, wd_ref[...].astype(BF16), preferred_element_type=F32)

    @pl.when(j == pl.num_programs(1) - 1)
    def _():
        y = x_ref[...] + acc_ref[...]
        if final_norm:
            y = _rms(y, gf_ref[...])
        o_ref[...] = y


def ffn(x, g, w_gate_up, w_down, g_final, *, tm, tf, final_norm, name):
    m, d = x.shape
    dff = w_down.shape[0]
    tm = min(tm, m)
    nf = dff // tf
    return pl.pallas_call(
        functools.partial(_ffn_kernel, final_norm=final_norm),
        grid=(m // tm, nf),
        in_specs=[pl.BlockSpec((tm, d), lambda i, j: (i, 0)),
                  pl.BlockSpec((1, d), lambda i, j: (0, 0)),
                  pl.BlockSpec((d, tf), lambda i, j: (0, j)),
                  pl.BlockSpec((d, tf), lambda i, j: (0, nf + j)),
                  pl.BlockSpec((tf, d), lambda i, j: (j, 0)),
                  pl.BlockSpec((1, d), lambda i, j: (0, 0))],
        out_specs=pl.BlockSpec((tm, d), lambda i, j: (i, 0)),
        out_shape=jax.ShapeDtypeStruct((m, d), F32),
        scratch_shapes=[pltpu.VMEM((tm, d), BF16), pltpu.VMEM((tm, d), F32)],
        compiler_params=_cparams(("parallel", "arbitrary")),
        name=name,
    )(x, g.reshape(1, d), w_gate_up, w_gate_up, w_down, g_final.reshape(1, d))


def _bucket_bias(n, rb_ref, n_heads):
    max_exact = N_BUCKETS // 2
    nf = jnp.maximum(n, 1).astype(F32)
    large = max_exact + (jnp.log(nf / max_exact) / math.log(MAX_DISTANCE / max_exact)
                         * (N_BUCKETS - max_exact)).astype(jnp.int32)
    large = jnp.minimum(large, N_BUCKETS - 1)
    bucket = jnp.where(n < max_exact, n, large)
    outs = []
    for h in range(n_heads):
        far = rb_ref[N_BUCKETS - 1, h]
        acc = jnp.zeros(n.shape, F32)
        for b in range(N_BUCKETS - 1):
            acc = jnp.where(bucket == b, rb_ref[b, h] - far, acc)
        outs.append(acc)
    return outs


def _tables_kernel(rb_ref, lb_in_ref, tiles_ref, dec_ref, lb_ref, *, T):
    r = lax.broadcasted_iota(jnp.int32, (T, T), 0)
    c = lax.broadcasted_iota(jnp.int32, (T, T), 1)
    n0 = r - c
    diag = _bucket_bias(jnp.maximum(n0, 0), rb_ref, DA_HEADS)
    prev = _bucket_bias(T + n0, rb_ref, DA_HEADS)
    for h in range(DA_HEADS):
        tiles_ref[0, h] = jnp.where(n0 >= 0, diag[h], NEG_BIG)
        tiles_ref[1, h] = prev[h]
        tiles_ref[2, h] = jnp.zeros((T, T), F32)
    lane = lax.broadcasted_iota(jnp.int32, (1, 256), 1)
    nd = jnp.where(lane < 128, 128 - lane, 0)
    dec = _bucket_bias(nd, rb_ref, DA_HEADS)
    dec_ref[...] = jnp.concatenate(dec + [jnp.zeros((8 - DA_HEADS, 256), F32)], axis=0)
    lbi = lb_in_ref[...]
    mx = jnp.max(lbi, axis=0, keepdims=True)
    e = jnp.exp(lbi - mx)
    lb_ref[...] = e[0:1, :] / jnp.sum(e, axis=0, keepdims=True)


def param_tables(rel_bias, lower_bound, T):
    d = lower_bound.shape[1]
    return pl.pallas_call(
        functools.partial(_tables_kernel, T=T),
        in_specs=[pl.BlockSpec(memory_space=pltpu.SMEM),
                  pl.BlockSpec(memory_space=pltpu.VMEM)],
        out_specs=[pl.BlockSpec(memory_space=pltpu.VMEM)] * 3,
        out_shape=[jax.ShapeDtypeStruct((3, DA_HEADS, T, T), F32),
                   jax.ShapeDtypeStruct((8, 256), F32),
                   jax.ShapeDtypeStruct((1, d), F32)],
        compiler_params=pltpu.CompilerParams(vmem_limit_bytes=VMEM_LIMIT),
        name="param_tables",
    )(rel_bias, lower_bound)


def _hgrn_sum_matrices(C):
    levels = int(math.log2(C))
    t = np.arange(C)
    u = t[None, :]
    mats = [u <= t[:, None], u > t[:, None]]
    for lv in range(levels):
        c = 1 << lv
        e = (t // (2 * c)) * (2 * c) + c - 1
        upper = (t > e)[:, None]
        seg = np.where(upper, (u > e[:, None]) & (u <= t[:, None]), (u > t[:, None]) & (u <= e[:, None]))
        mats.append(seg)
    return np.concatenate(mats, axis=0).astype(np.float32), levels


def _split3(x):
    hi = x.astype(BF16)
    r1 = x - hi.astype(F32)
    mid = r1.astype(BF16)
    lo = (r1 - mid.astype(F32)).astype(BF16)
    return hi, mid, lo


def _hgrn_prompt_kernel(q_ref, f_ref, i_ref, g_ref, lb_ref, gn_ref, mall_ref, o_ref, st_ref, s_scr, *, C, levels):
    cidx = pl.program_id(1)
    dk = s_scr.shape[-1]

    @pl.when(cidx == 0)
    def _():
        s_scr[...] = jnp.zeros_like(s_scr)

    lb = lb_ref[...]
    q = _silu(q_ref[0])
    fg = lb + (1.0 - lb) * _sigmoid(f_ref[0])
    kk = 1.0 - fg
    logf = jnp.log(fg)
    v = i_ref[0]
    v_bf = v.astype(BF16)

    mall = mall_ref[...]
    hi, mid, lo = _split3(logf)
    sums = (jnp.dot(mall, hi, preferred_element_type=F32)
            + jnp.dot(mall, mid, preferred_element_type=F32)
            + jnp.dot(mall, lo, preferred_element_type=F32))
    b = sums[0:C]
    suf = sums[C:2 * C]
    qe = (q * jnp.exp(b)).astype(BF16)
    kt = (kk * jnp.exp(suf)).astype(BF16)
    dec_last = jnp.exp(b[C - 1:C, :])

    ti = lax.broadcasted_iota(jnp.int32, (C, C), 0)
    si = lax.broadcasted_iota(jnp.int32, (C, C), 1)
    txs = jnp.bitwise_xor(ti, si)
    lower = ti > si
    scores = [jnp.zeros((C, C), F32) for _ in range(HG_HEADS)]
    for lv in range(levels):
        e = jnp.exp(sums[(2 + lv) * C:(3 + lv) * C])
        a_bf = (q * e).astype(BF16)
        b_bf = (kk * e).astype(BF16)
        valid = jnp.logical_and(jnp.right_shift(txs, lv) == 1, lower)
        for h in range(HG_HEADS):
            hs = slice(h * dk, (h + 1) * dk)
            sl = lax.dot_general(a_bf[:, hs], b_bf[:, hs], _NT, preferred_element_type=F32)
            scores[h] = scores[h] + jnp.where(valid, sl, 0.0)

    qk = q * kk
    g = g_ref[0]
    gn = gn_ref[...]
    for h in range(HG_HEADS):
        hs = slice(h * dk, (h + 1) * dk)
        st = s_scr[h]
        diag = jnp.sum(qk[:, hs], axis=-1, keepdims=True)
        o = (jnp.dot(scores[h].astype(BF16), v_bf[:, hs], preferred_element_type=F32)
             + diag * v[:, hs]
             + lax.dot_general(qe[:, hs], st.astype(BF16), _NT, preferred_element_type=F32))
        s_scr[h] = st * dec_last[:, hs] + lax.dot_general(v_bf[:, hs], kt[:, hs], _TN, preferred_element_type=F32)
        o_ref[0, :, hs] = (_rms(o, gn[:, hs]) * _silu(g[:, hs])).astype(o_ref.dtype)

    @pl.when(cidx == pl.num_programs(1) - 1)
    def _():
        for h in range(HG_HEADS):
            st_ref[0, h] = s_scr[h].T


def hgrn_prompt(q, f, i, g, lb, gnorm, *, C):
    bsz, s, d = q.shape
    dk = d // HG_HEADS
    mall_np, levels = _hgrn_sum_matrices(C)
    mall = jnp.asarray(mall_np, dtype=BF16)
    blk = pl.BlockSpec((1, C, d), lambda b, c: (b, c, 0))
    vec = pl.BlockSpec((1, d), lambda b, c: (0, 0))
    return pl.pallas_call(
        functools.partial(_hgrn_prompt_kernel, C=C, levels=levels),
        grid=(bsz, s // C),
        in_specs=[blk, blk, blk, blk, vec, vec,
                  pl.BlockSpec(mall.shape, lambda b, c: (0, 0))],
        out_specs=[pl.BlockSpec((1, C, d), lambda b, c: (b, c, 0)),
                   pl.BlockSpec((1, HG_HEADS, dk, dk), lambda b, c: (b, 0, 0, 0))],
        out_shape=[jax.ShapeDtypeStruct((bsz, s, d), BF16),
                   jax.ShapeDtypeStruct((bsz, HG_HEADS, dk, dk), F32)],
        scratch_shapes=[pltpu.VMEM((HG_HEADS, dk, dk), F32)],
        compiler_params=_cparams(("parallel", "arbitrary")),
        name="hgrn_prompt",
    )(q, f, i, g, lb, gnorm.reshape(1, d), mall)


def _hgrn_step_kernel(q_ref, f_ref, i_ref, g_ref, lb_ref, gn_ref, s_ref, o_ref, so_ref):
    rows = q_ref.shape[0]
    lb = lb_ref[...]
    fg = lb + (1.0 - lb) * _sigmoid(f_ref[...])
    q_t = _silu(q_ref[...]).T
    fg_t = fg.T
    kk_t = (1.0 - fg).T
    v = i_ref[...]
    outs = []
    for r in range(rows):
        bl, h = divmod(r, HG_HEADS)
        s_new = fg_t[:, r:r + 1] * s_ref[bl, h] + kk_t[:, r:r + 1] * v[r:r + 1, :]
        so_ref[bl, h] = s_new
        outs.append(jnp.sum(q_t[:, r:r + 1] * s_new, axis=0, keepdims=True))
    o = jnp.concatenate(outs, axis=0)
    o_ref[...] = (_rms(o, gn_ref[...]) * _silu(g_ref[...])).astype(o_ref.dtype)


def hgrn_step(q, f, i, g, lb, gnorm, state, *, bb):
    bsz, d = q.shape
    dk = d // HG_HEADS
    rows = bb * HG_HEADS
    to_rows = lambda a: a.reshape(bsz * HG_HEADS, dk)
    tile = lambda p: jnp.tile(p.reshape(HG_HEADS, dk), (bb, 1))
    rblk = pl.BlockSpec((rows, dk), lambda b: (b, 0))
    pblk = pl.BlockSpec((rows, dk), lambda b: (0, 0))
    sblk = pl.BlockSpec((bb, HG_HEADS, dk, dk), lambda b: (b, 0, 0, 0))
    o, s_new = pl.pallas_call(
        _hgrn_step_kernel,
        grid=(bsz // bb,),
        in_specs=[rblk, rblk, rblk, rblk, pblk, pblk, sblk],
        out_specs=[rblk, sblk],
        out_shape=[jax.ShapeDtypeStruct((bsz * HG_HEADS, dk), BF16),
                   jax.ShapeDtypeStruct(state.shape, F32)],
        compiler_params=_cparams(("parallel",)),
        name="hgrn_step",
    )(to_rows(q), to_rows(f), to_rows(i), to_rows(g), tile(lb), tile(gnorm), state)
    return o.reshape(bsz, d), s_new


def _lambda(lq1_ref, lk1_ref, lq2_ref, lk2_ref, lam_init):
    s1 = jnp.sum(lq1_ref[...] * lk1_ref[...], axis=-1, keepdims=True)
    s2 = jnp.sum(lq2_ref[...] * lk2_ref[...], axis=-1, keepdims=True)
    return jnp.exp(s1) - jnp.exp(s2) + lam_init


def _attn_prompt_kernel(qi_ref, kj_ref, ts_ref, q_ref, k_ref, v_ref, bias_ref,
                        lq1_ref, lk1_ref, lq2_ref, lk2_ref, sub_ref, o_ref,
                        m_scr, l_scr, acc_scr, *, lam_init):
    step = pl.program_id(1)
    i = qi_ref[step]
    j = kj_ref[step]
    dv = acc_scr.shape[-1]
    dh = dv // 2

    @pl.when(j == 0)
    def _():
        m_scr[...] = jnp.full_like(m_scr, NEG_BIG)
        l_scr[...] = jnp.zeros_like(l_scr)
        acc_scr[...] = jnp.zeros_like(acc_scr)

    q = q_ref[0]
    k = k_ref[0]
    v = v_ref[0]
    for h in range(DA_HEADS):
        vh = v[:, h * dv:(h + 1) * dv]
        bias = bias_ref[0, h]
        for c in range(2):
            idx = 2 * h + c
            lo = h * dv + c * dh
            s = lax.dot_general(q[:, lo:lo + dh], k[:, lo:lo + dh], _NT, preferred_element_type=F32) + bias
            m_prev = m_scr[idx]
            m_new = jnp.maximum(m_prev, jnp.max(s, axis=-1, keepdims=True))
            alpha = jnp.exp(m_prev - m_new)
            p = jnp.exp(s - m_new)
            l_scr[idx] = alpha * l_scr[idx] + jnp.sum(p, axis=-1, keepdims=True)
            acc_scr[idx] = alpha * acc_scr[idx] + jnp.dot(p.astype(BF16), vh, preferred_element_type=F32)
            m_scr[idx] = m_new

    @pl.when(j == i)
    def _():
        lam = _lambda(lq1_ref, lk1_ref, lq2_ref, lk2_ref, lam_init)
        sub = sub_ref[...]
        for h in range(DA_HEADS):
            o = (acc_scr[2 * h] * (1.0 / l_scr[2 * h])
                 - lam * (acc_scr[2 * h + 1] * (1.0 / l_scr[2 * h + 1])))
            o_ref[0, :, h * dv:(h + 1) * dv] = (_rms(o, sub) * (1.0 - lam_init)).astype(o_ref.dtype)


def attn_prompt(q, k, v, tiles, lams, subln, *, T, lam_init):
    bsz, s, d = q.shape
    nq = s // T
    dv = d // DA_HEADS
    qi, kj, ts = [], [], []
    for i in range(nq):
        for j in range(i + 1):
            qi.append(i)
            kj.append(j)
            ts.append(0 if j == i else (1 if j == i - 1 else 2))
    qi, kj, ts = (jnp.asarray(a, jnp.int32) for a in (qi, kj, ts))
    dl = lams[0].shape[-1]
    lam_spec = pl.BlockSpec((1, dl), lambda b, st, qi, kj, ts: (0, 0))
    grid_spec = pltpu.PrefetchScalarGridSpec(
        num_scalar_prefetch=3,
        grid=(bsz, int(qi.shape[0])),
        in_specs=[pl.BlockSpec((1, T, d), lambda b, st, qi, kj, ts: (b, qi[st], 0)),
                  pl.BlockSpec((1, T, d), lambda b, st, qi, kj, ts: (b, kj[st], 0)),
                  pl.BlockSpec((1, T, d), lambda b, st, qi, kj, ts: (b, kj[st], 0)),
                  pl.BlockSpec((1, DA_HEADS, T, T), lambda b, st, qi, kj, ts: (ts[st], 0, 0, 0)),
                  lam_spec, lam_spec, lam_spec, lam_spec,
                  pl.BlockSpec((1, dv), lambda b, st, qi, kj, ts: (0, 0))],
        out_specs=pl.BlockSpec((1, T, d), lambda b, st, qi, kj, ts: (b, qi[st], 0)),
        scratch_shapes=[pltpu.VMEM((2 * DA_HEADS, T, 1), F32),
                        pltpu.VMEM((2 * DA_HEADS, T, 1), F32),
                        pltpu.VMEM((2 * DA_HEADS, T, dv), F32)])
    return pl.pallas_call(
        functools.partial(_attn_prompt_kernel, lam_init=lam_init),
        grid_spec=grid_spec,
        out_shape=jax.ShapeDtypeStruct((bsz, s, d), BF16),
        compiler_params=_cparams(("parallel", "arbitrary")),
        name="attn_prompt",
    )(qi, kj, ts, q, k, v, tiles, *lams, subln.reshape(1, dv))


def _attn_decode_kernel(pt_ref, q_ref, kn_ref, vn_ref, *refs, pps, lam_init):
    k_refs = refs[:pps]
    v_refs = refs[pps:2 * pps]
    (dbias_ref, lq1_ref, lk1_ref, lq2_ref, lk2_ref, sub_ref, o_ref, m_scr, l_scr, acc_scr) = refs[2 * pps:]
    pg = pl.program_id(1)
    last = pg == pl.num_programs(1) - 1
    dv = acc_scr.shape[-1]
    dh = dv // 2

    @pl.when(pg == 0)
    def _():
        m_scr[...] = jnp.full_like(m_scr, NEG_BIG)
        l_scr[...] = jnp.zeros_like(l_scr)
        acc_scr[...] = jnp.zeros_like(acc_scr)

    q = q_ref[0].astype(F32)
    row = lax.broadcasted_iota(jnp.int32, (8, dv), 0)
    lane = lax.broadcasted_iota(jnp.int32, (8, dv), 1)
    sel = jnp.logical_or(jnp.logical_and(row == 0, lane < dh), jnp.logical_and(row == 1, lane >= dh))
    lhs = [jnp.where(sel, jnp.broadcast_to(q[:, h * dv:(h + 1) * dv], (8, dv)), 0.0).astype(BF16)
           for h in range(DA_HEADS)]
    near = jnp.where(last, 1.0, 0.0)

    def update(h, s, vals):
        m_prev = m_scr[h]
        m_new = jnp.maximum(m_prev, jnp.max(s, axis=-1, keepdims=True))
        alpha = jnp.exp(m_prev - m_new)
        p = jnp.exp(s - m_new)
        l_scr[h] = alpha * l_scr[h] + jnp.sum(p, axis=-1, keepdims=True)
        acc_scr[h] = alpha * acc_scr[h] + jnp.dot(p.astype(BF16), vals, preferred_element_type=F32)
        m_scr[h] = m_new

    for r in range(pps):
        kp = k_refs[r][0].astype(BF16)
        vp = v_refs[r][0].astype(BF16)
        for h in range(DA_HEADS):
            s = lax.dot_general(lhs[h], kp[:, h * dv:(h + 1) * dv], _NT, preferred_element_type=F32)
            if r == pps - 1:
                s = s + near * dbias_ref[h:h + 1, 0:128]
            update(h, s, vp[:, h * dv:(h + 1) * dv])

    @pl.when(last)
    def _():
        lam = _lambda(lq1_ref, lk1_ref, lq2_ref, lk2_ref, lam_init)
        sub = sub_ref[...]
        kn = kn_ref[0]
        vn = vn_ref[0]
        for h in range(DA_HEADS):
            hs = slice(h * dv, (h + 1) * dv)
            kn_b = jnp.broadcast_to(kn[:, hs], (8, dv)).astype(BF16)
            s_new = lax.dot_general(lhs[h], kn_b, _NT, preferred_element_type=F32)[:, 0:1]
            s_new = s_new + dbias_ref[h:h + 1, 128:129]
            m_prev = m_scr[h]
            m_new = jnp.maximum(m_prev, s_new)
            alpha = jnp.exp(m_prev - m_new)
            p_new = jnp.exp(s_new - m_new)
            l_fin = alpha * l_scr[h] + p_new
            acc = alpha * acc_scr[h] + p_new * vn[:, hs]
            an = acc * (1.0 / l_fin)
            o = an[0:1, :] - lam * an[1:2, :]
            o_ref[0, :, hs] = (_rms(o, sub) * (1.0 - lam_init)).astype(o_ref.dtype)


def attn_decode(q, k_new, v_new, cache_k, cache_v, page_table, dbias, lams, subln, *, pps, lam_init):
    bsz, d = q.shape
    n_pages = page_table.shape[1]
    page = cache_k.shape[1]
    dv = d // DA_HEADS
    dl = lams[0].shape[-1]
    row_spec = pl.BlockSpec((1, 1, d), lambda b, p, pt: (b, 0, 0))
    page_specs = [pl.BlockSpec((1, page, d), functools.partial(lambda b, p, pt, r: (pt[b, p * pps + r], 0, 0), r=r))
                  for r in range(pps)]
    lam_spec = pl.BlockSpec((1, dl), lambda b, p, pt: (0, 0))
    grid_spec = pltpu.PrefetchScalarGridSpec(
        num_scalar_prefetch=1,
        grid=(bsz, n_pages // pps),
        in_specs=[row_spec, row_spec, row_spec] + page_specs + page_specs
                 + [pl.BlockSpec((8, 256), lambda b, p, pt: (0, 0)),
                    lam_spec, lam_spec, lam_spec, lam_spec,
                    pl.BlockSpec((1, dv), lambda b, p, pt: (0, 0))],
        out_specs=pl.BlockSpec((1, 1, d), lambda b, p, pt: (b, 0, 0)),
        scratch_shapes=[pltpu.VMEM((DA_HEADS, 8, 1), F32),
                        pltpu.VMEM((DA_HEADS, 8, 1), F32),
                        pltpu.VMEM((DA_HEADS, 8, dv), F32)])
    r3 = lambda a: a.reshape(bsz, 1, d)
    out = pl.pallas_call(
        functools.partial(_attn_decode_kernel, pps=pps, lam_init=lam_init),
        grid_spec=grid_spec,
        out_shape=jax.ShapeDtypeStruct((bsz, 1, d), BF16),
        compiler_params=_cparams(("parallel", "arbitrary")),
        name="attn_decode",
    )(page_table, r3(q), r3(k_new), r3(v_new), *([cache_k] * pps), *([cache_v] * pps),
      dbias, *lams, subln.reshape(1, dv))
    return out.reshape(bsz, d)


ATTN_T = 256
HGRN_C = 64
TM = 1024


def _trunk(x, hg_state, kv_past, page_table, tiles, dbias, lb, p, *, batch_shape):
    m, d = x.shape
    prompt = hg_state is None
    dv = d // DA_HEADS
    scale = (dv // 2) ** -0.5
    lam_init = 0.8 - 0.6 * math.exp(-0.3 * 1)
    lams = [p[n][0].reshape(1, -1) for n in ("lambda_q1", "lambda_k1", "lambda_q2", "lambda_k2")]

    q, f, i, g = norm_matmul(x, p["norm_mix"][0], p["w_in_a"][0], [(F32,)] * 4, tm=TM, tn=256, name="in_proj")
    if prompt:
        bsz, s = batch_shape
        r3 = lambda a: a.reshape(bsz, s, d)
        o, state = hgrn_prompt(r3(q), r3(f), r3(i), r3(g), lb, p["gnorm_a"][0], C=HGRN_C)
        o = o.reshape(m, d)
    else:
        o, state = hgrn_step(q, f, i, g, lb, p["gnorm_a"][0], hg_state, bb=16)
    x = matmul_res(o, p["w_out_a"][0], x, tm=TM, name="out_proj_a")
    x = ffn(x, p["norm_ffn"][0], p["w_gate_up"][0], p["w_down"][0], p["norm_final"],
            tm=TM, tf=256, final_norm=False, name="ffn0")

    k32, k16, v32, v16 = norm_matmul(x, p["kv_norm"], p["w_kv"], [(F32, BF16)] * 2, tm=TM, tn=512, name="kv_proj")

    (qa,) = norm_matmul(x, p["norm_mix"][1], p["w_q_b"][0], [(BF16,)], tm=TM, tn=1024, scale=scale, name="q_proj")
    if prompt:
        o = attn_prompt(r3(qa), r3(k16), r3(v16), tiles, lams, p["subln_b"][0], T=ATTN_T, lam_init=lam_init)
        o = o.reshape(m, d)
    else:
        o = attn_decode(qa, k32, v32, kv_past[0], kv_past[1], page_table, dbias, lams, p["subln_b"][0],
                        pps=4, lam_init=lam_init)
    x = matmul_res(o, p["w_out_b"][0], x, tm=TM, name="out_proj_b")
    y = ffn(x, p["norm_ffn"][1], p["w_gate_up"][1], p["w_down"][1], p["norm_final"],
            tm=TM, tf=256, final_norm=True, name="ffn1")
    return y, k32, v32, state


def kernel(x_prompt, x_sample, cache_k, cache_v, state_hgrn, page_table, w_in_a, lower_bound, gnorm_a, w_out_a,
           w_q_b, lambda_q1, lambda_k1, lambda_q2, lambda_k2, subln_b, w_out_b, kv_norm, w_kv, rel_bias,
           norm_mix, norm_ffn, w_gate_up, w_down, norm_final):
    p = dict(w_in_a=w_in_a, gnorm_a=gnorm_a, w_out_a=w_out_a, w_q_b=w_q_b, lambda_q1=lambda_q1,
             lambda_k1=lambda_k1, lambda_q2=lambda_q2, lambda_k2=lambda_k2, subln_b=subln_b, w_out_b=w_out_b,
             kv_norm=kv_norm, w_kv=w_kv, norm_mix=norm_mix, norm_ffn=norm_ffn, w_gate_up=w_gate_up,
             w_down=w_down, norm_final=norm_final)
    bsz, s, d = x_prompt.shape
    nb = x_sample.shape[0]
    n_pool, page = cache_k.shape[:2]
    hk = DA_HEADS, d // DA_HEADS
    tiles, dbias, lb = param_tables(rel_bias, lower_bound, ATTN_T)

    y_p, k_p, v_p, st_p = _trunk(x_prompt.reshape(bsz * s, d), None, None, None, tiles, dbias, lb, p,
                                 batch_shape=(bsz, s))
    kv_past = (cache_k.reshape(n_pool, page, d), cache_v.reshape(n_pool, page, d))
    y_s, k_s, v_s, st_s = _trunk(x_sample.reshape(nb, d), state_hgrn[0], kv_past, page_table, tiles, dbias, lb, p,
                                 batch_shape=(nb, 1))
    return (y_p.reshape(bsz, s, d), y_s.reshape(nb, 1, d),
            k_p.reshape(bsz, s, *hk), v_p.reshape(bsz, s, *hk), st_p[None],
            k_s.reshape(nb, 1, *hk), v_s.reshape(nb, 1, *hk), st_s[None])
```

```python
import functools
import math

import numpy as np
import jax
import jax.numpy as jnp
from jax import lax
from jax.experimental import pallas as pl
from jax.experimental.pallas import tpu as pltpu

F32 = jnp.float32
BF16 = jnp.bfloat16
EPS = 1e-6
NEG_BIG = -1e30
LOG2E = math.log2(math.e)

HG_HEADS = 8
DA_HEADS = 4
N_BUCKETS = 32
MAX_DISTANCE = 128
LANES = 128
VMEM_LIMIT = 56 * 1024 * 1024

_NT = (((1,), (1,)), ((), ()))
_TN = (((0,), (0,)), ((), ()))


def _cparams(sem):
    return pltpu.CompilerParams(dimension_semantics=sem, vmem_limit_bytes=VMEM_LIMIT)


def _sigmoid(x):
    return 1.0 / (1.0 + jnp.exp(-x))


def _silu(x):
    return x * _sigmoid(x)


def _rms(x, g):
    ms = jnp.mean(x * x, axis=-1, keepdims=True)
    return x * lax.rsqrt(ms + EPS) * g


def _lane_tile(x, reps):
    return x if reps == 1 else jnp.concatenate([x] * reps, axis=1)


def _norm_mm_kernel(x_ref, g_ref, *refs, group_outs, scale):
    n_groups = len(group_outs)
    w_refs = refs[:n_groups]
    out_refs = refs[n_groups:-1]
    xn_ref = refs[-1]

    @pl.when(pl.program_id(1) == 0)
    def _():
        xn_ref[...] = _rms(x_ref[...], g_ref[...]).astype(BF16)

    xn = xn_ref[...]
    o = 0
    for gi in range(n_groups):
        acc = jnp.dot(xn, w_refs[gi][...].astype(BF16), preferred_element_type=F32)
        if scale != 1.0:
            acc = acc * scale
        for _ in group_outs[gi]:
            ref = out_refs[o]
            if len(ref.shape) == 3:
                hd = ref.shape[2]
                for hh in range(ref.shape[1]):
                    ref[:, hh, :] = acc[:, hh * hd:(hh + 1) * hd].astype(ref.dtype)
            else:
                ref[...] = acc.astype(ref.dtype)
            o += 1


def norm_matmul(x, g, w, group_outs, *, tm, tn, scale=1.0, name):
    m, d = x.shape
    n_groups = len(group_outs)
    ng = w.shape[1] // n_groups
    tm = min(tm, m)
    tn = min(tn, ng)
    nj = ng // tn
    in_specs = [pl.BlockSpec((tm, d), lambda i, j: (i, 0)),
                pl.BlockSpec((1, d), lambda i, j: (0, 0))]
    for gi in range(n_groups):
        in_specs.append(pl.BlockSpec((d, tn), functools.partial(lambda i, j, off: (0, off + j), off=gi * nj)))
    out_shape, out_specs = [], []
    for outs in group_outs:
        for dt, heads in outs:
            if heads:
                assert tn == ng
                out_shape.append(jax.ShapeDtypeStruct((m, heads, ng // heads), dt))
                out_specs.append(pl.BlockSpec((tm, heads, ng // heads), lambda i, j: (i, 0, 0)))
            else:
                out_shape.append(jax.ShapeDtypeStruct((m, ng), dt))
                out_specs.append(pl.BlockSpec((tm, tn), lambda i, j: (i, j)))
    return pl.pallas_call(
        functools.partial(_norm_mm_kernel, group_outs=group_outs, scale=scale),
        grid=(m // tm, nj),
        in_specs=in_specs,
        out_specs=out_specs,
        out_shape=out_shape,
        scratch_shapes=[pltpu.VMEM((tm, d), BF16)],
        compiler_params=_cparams(("parallel", "arbitrary")),
        name=name,
    )(x, g.reshape(1, d), *([w] * n_groups))


def _mm_res_kernel(a_ref, w_ref, r_ref, o_ref):
    o_ref[...] = r_ref[...] + jnp.dot(a_ref[...], w_ref[...].astype(BF16), preferred_element_type=F32)


def matmul_res(a, w, res, *, tm, name):
    m, k = a.shape
    n = w.shape[1]
    tm = min(tm, m)
    return pl.pallas_call(
        _mm_res_kernel,
        grid=(m // tm,),
        in_specs=[pl.BlockSpec((tm, k), lambda i: (i, 0)),
                  pl.BlockSpec((k, n), lambda i: (0, 0)),
                  pl.BlockSpec((tm, n), lambda i: (i, 0))],
        out_specs=pl.BlockSpec((tm, n), lambda i: (i, 0)),
        out_shape=jax.ShapeDtypeStruct((m, n), F32),
        compiler_params=_cparams(("parallel",)),
        name=name,
    )(a, w, res)


def _ffn_kernel(x_ref, g_ref, wg_ref, wu_ref, wd_ref, gf_ref, o_ref, xn_ref, acc_ref, *, final_norm):
    j = pl.program_id(1)

    @pl.when(j == 0)
    def _():
        xn_ref[...] = _rms(x_ref[...], g_ref[...]).astype(BF16)
        acc_ref[...] = jnp.zeros_like(acc_ref)

    xn = xn_ref[...]
    gt = jnp.dot(xn, wg_ref[...].astype(BF16), preferred_element_type=F32)
    ut = jnp.dot(xn, wu_ref[...].astype(BF16), preferred_element_type=F32)
    hid = (_silu(gt) * ut).astype(BF16)
    acc_ref[...] += jnp.dot(hid, wd_ref[...].astype(BF16), preferred_element_type=F32)

    @pl.when(j == pl.num_programs(1) - 1)
    def _():
        y = x_ref[...] + acc_ref[...]
        if final_norm:
            y = _rms(y, gf_ref[...])
        o_ref[...] = y


def ffn(x, g, w_gate_up, w_down, g_final, *, tm, tf, final_norm, name):
    m, d = x.shape
    dff = w_down.shape[0]
    tm = min(tm, m)
    nf = dff // tf
    return pl.pallas_call(
        functools.partial(_ffn_kernel, final_norm=final_norm),
        grid=(m // tm, nf),
        in_specs=[pl.BlockSpec((tm, d), lambda i, j: (i, 0)),
                  pl.BlockSpec((1, d), lambda i, j: (0, 0)),
                  pl.BlockSpec((d, tf), lambda i, j: (0, j)),
                  pl.BlockSpec((d, tf), lambda i, j: (0, nf + j)),
                  pl.BlockSpec((tf, d), lambda i, j: (j, 0)),
                  pl.BlockSpec((1, d), lambda i, j: (0, 0))],
        out_specs=pl.BlockSpec((tm, d), lambda i, j: (i, 0)),
        out_shape=jax.ShapeDtypeStruct((m, d), F32),
        scratch_shapes=[pltpu.VMEM((tm, d), BF16), pltpu.VMEM((tm, d), F32)],
        compiler_params=_cparams(("parallel", "arbitrary")),
        name=name,
    )(x, g.reshape(1, d), w_gate_up, w_gate_up, w_down, g_final.reshape(1, d))


BIAS_T = 2 * MAX_DISTANCE


def _bucket_bias(n, rb_ref, n_heads):
    max_exact = N_BUCKETS // 2
    nf = jnp.maximum(n, 1).astype(F32)
    large = max_exact + (jnp.log(nf / max_exact) / math.log(MAX_DISTANCE / max_exact)
                         * (N_BUCKETS - max_exact)).astype(jnp.int32)
    large = jnp.minimum(large, N_BUCKETS - 1)
    bucket = jnp.where(n < max_exact, n, large)
    outs = []
    for h in range(n_heads):
        far = rb_ref[N_BUCKETS - 1, h]
        acc = jnp.zeros(n.shape, F32)
        for b in range(N_BUCKETS - 1):
            acc = jnp.where(bucket == b, (rb_ref[b, h] - far) * LOG2E, acc)
        outs.append(acc)
    return outs


def _tables_kernel(rb_ref, lb_in_ref, tiles_ref, dec_ref, lb_ref):
    bt = tiles_ref.shape[-1]
    r = lax.broadcasted_iota(jnp.int32, (bt, bt), 0)
    c = lax.broadcasted_iota(jnp.int32, (bt, bt), 1)
    n0 = r - c
    diag = _bucket_bias(jnp.maximum(n0, 0), rb_ref, DA_HEADS)
    prev = _bucket_bias(bt + n0, rb_ref, DA_HEADS)
    for h in range(DA_HEADS):
        tiles_ref[0, h] = jnp.where(n0 >= 0, diag[h], NEG_BIG)
        tiles_ref[1, h] = prev[h]
    lane = lax.broadcasted_iota(jnp.int32, (1, 2 * LANES), 1)
    nd = jnp.where(lane < LANES, LANES - lane, 0)
    dec = _bucket_bias(nd, rb_ref, DA_HEADS)
    dec_ref[...] = jnp.concatenate([dec[h] for h in range(DA_HEADS) for _ in range(2)], axis=0)
    lbi = lb_in_ref[...]
    mx = jnp.max(lbi, axis=0, keepdims=True)
    e = jnp.exp(lbi - mx)
    lb_ref[...] = e[0:1, :] / jnp.sum(e, axis=0, keepdims=True)


def param_tables(rel_bias, lower_bound):
    d = lower_bound.shape[1]
    return pl.pallas_call(
        _tables_kernel,
        in_specs=[pl.BlockSpec(memory_space=pltpu.SMEM),
                  pl.BlockSpec(memory_space=pltpu.VMEM)],
        out_specs=[pl.BlockSpec(memory_space=pltpu.VMEM)] * 3,
        out_shape=[jax.ShapeDtypeStruct((2, DA_HEADS, BIAS_T, BIAS_T), F32),
                   jax.ShapeDtypeStruct((2 * DA_HEADS, 2 * LANES), F32),
                   jax.ShapeDtypeStruct((1, d), F32)],
        compiler_params=pltpu.CompilerParams(vmem_limit_bytes=VMEM_LIMIT),
        name="param_tables",
    )(rel_bias, lower_bound)


def _hgrn_sum_matrices(C):
    levels = int(math.log2(C))
    t = np.arange(C)
    u = t[None, :]
    mats = [u <= t[:, None], u > t[:, None]]
    for lv in range(levels):
        c = 1 << lv
        e = (t // (2 * c)) * (2 * c) + c - 1
        upper = (t > e)[:, None]
        seg = np.where(upper, (u > e[:, None]) & (u <= t[:, None]), (u > t[:, None]) & (u <= e[:, None]))
        mats.append(seg)
    return np.concatenate(mats, axis=0).astype(np.float32), levels


def _split3(x):
    hi = x.astype(BF16)
    r1 = x - hi.astype(F32)
    mid = r1.astype(BF16)
    lo = (r1 - mid.astype(F32)).astype(BF16)
    return hi, mid, lo


def _hgrn_prompt_kernel(q_ref, f_ref, i_ref, g_ref, lb_ref, gn_ref, mall_ref, o_ref, st_ref, s_scr, *, C, levels):
    cidx = pl.program_id(1)
    dk = s_scr.shape[-1]

    @pl.when(cidx == 0)
    def _():
        s_scr[...] = jnp.zeros_like(s_scr)

    lb = lb_ref[...]
    q = _silu(q_ref[0])
    fg = lb + (1.0 - lb) * _sigmoid(f_ref[0])
    kk = 1.0 - fg
    logf = jnp.log(fg)
    v = i_ref[0]
    v_bf = v.astype(BF16)

    mall = mall_ref[...]
    hi, mid, lo = _split3(logf)
    sums = (jnp.dot(mall, hi, preferred_element_type=F32)
            + jnp.dot(mall, mid, preferred_element_type=F32)
            + jnp.dot(mall, lo, preferred_element_type=F32))
    b = sums[0:C]
    suf = sums[C:2 * C]
    qe = (q * jnp.exp(b)).astype(BF16)
    kt = (kk * jnp.exp(suf)).astype(BF16)
    dec_last = jnp.exp(b[C - 1:C, :])

    ti = lax.broadcasted_iota(jnp.int32, (C, C), 0)
    si = lax.broadcasted_iota(jnp.int32, (C, C), 1)
    txs = jnp.bitwise_xor(ti, si)
    lower = ti > si
    scores = [jnp.zeros((C, C), F32) for _ in range(HG_HEADS)]
    for lv in range(levels):
        e = jnp.exp(sums[(2 + lv) * C:(3 + lv) * C])
        a_bf = (q * e).astype(BF16)
        b_bf = (kk * e).astype(BF16)
        valid = jnp.logical_and(jnp.right_shift(txs, lv) == 1, lower)
        for h in range(HG_HEADS):
            hs = slice(h * dk, (h + 1) * dk)
            sl = lax.dot_general(a_bf[:, hs], b_bf[:, hs], _NT, preferred_element_type=F32)
            scores[h] = scores[h] + jnp.where(valid, sl, 0.0)

    qk = q * kk
    g = g_ref[0]
    gn = gn_ref[...]
    for h in range(HG_HEADS):
        hs = slice(h * dk, (h + 1) * dk)
        st = s_scr[h]
        diag = jnp.sum(qk[:, hs], axis=-1, keepdims=True)
        o = (jnp.dot(scores[h].astype(BF16), v_bf[:, hs], preferred_element_type=F32)
             + diag * v[:, hs]
             + lax.dot_general(qe[:, hs], st.astype(BF16), _NT, preferred_element_type=F32))
        s_scr[h] = st * dec_last[:, hs] + lax.dot_general(v_bf[:, hs], kt[:, hs], _TN, preferred_element_type=F32)
        o_ref[0, :, hs] = (_rms(o, gn[:, hs]) * _silu(g[:, hs])).astype(o_ref.dtype)

    @pl.when(cidx == pl.num_programs(1) - 1)
    def _():
        for h in range(HG_HEADS):
            st_ref[0, h] = s_scr[h].T


def hgrn_prompt(q, f, i, g, lb, gnorm, *, C):
    bsz, s, d = q.shape
    dk = d // HG_HEADS
    mall_np, levels = _hgrn_sum_matrices(C)
    mall = jnp.asarray(mall_np, dtype=BF16)
    blk = pl.BlockSpec((1, C, d), lambda b, c: (b, c, 0))
    vec = pl.BlockSpec((1, d), lambda b, c: (0, 0))
    return pl.pallas_call(
        functools.partial(_hgrn_prompt_kernel, C=C, levels=levels),
        grid=(bsz, s // C),
        in_specs=[blk, blk, blk, blk, vec, vec,
                  pl.BlockSpec(mall.shape, lambda b, c: (0, 0))],
        out_specs=[pl.BlockSpec((1, C, d), lambda b, c: (b, c, 0)),
                   pl.BlockSpec((1, HG_HEADS, dk, dk), lambda b, c: (b, 0, 0, 0))],
        out_shape=[jax.ShapeDtypeStruct((bsz, s, d), BF16),
                   jax.ShapeDtypeStruct((bsz, HG_HEADS, dk, dk), F32)],
        scratch_shapes=[pltpu.VMEM((HG_HEADS, dk, dk), F32)],
        compiler_params=_cparams(("parallel", "arbitrary")),
        name="hgrn_prompt",
    )(q, f, i, g, lb, gnorm.reshape(1, d), mall)


def _hgrn_step_kernel(q_ref, f_ref, i_ref, g_ref, lb_ref, gn_ref, s_ref, o_ref, so_ref):
    rows = q_ref.shape[0]
    lb = lb_ref[...]
    fg = lb + (1.0 - lb) * _sigmoid(f_ref[...])
    q_t = _silu(q_ref[...]).T
    fg_t = fg.T
    kk_t = (1.0 - fg).T
    v = i_ref[...]
    outs = []
    for r in range(rows):
        bl, h = divmod(r, HG_HEADS)
        s_new = fg_t[:, r:r + 1] * s_ref[bl, h] + kk_t[:, r:r + 1] * v[r:r + 1, :]
        so_ref[bl, h] = s_new
        outs.append(jnp.sum(q_t[:, r:r + 1] * s_new, axis=0, keepdims=True))
    o = jnp.concatenate(outs, axis=0)
    o_ref[...] = (_rms(o, gn_ref[...]) * _silu(g_ref[...])).astype(o_ref.dtype)


def hgrn_step(q, f, i, g, lb, gnorm, state, *, bb):
    bsz, d = q.shape
    dk = d // HG_HEADS
    rows = bb * HG_HEADS
    to_rows = lambda a: a.reshape(bsz * HG_HEADS, dk)
    tile = lambda p: jnp.tile(p.reshape(HG_HEADS, dk), (bb, 1))
    rblk = pl.BlockSpec((rows, dk), lambda b: (b, 0))
    pblk = pl.BlockSpec((rows, dk), lambda b: (0, 0))
    sblk = pl.BlockSpec((bb, HG_HEADS, dk, dk), lambda b: (b, 0, 0, 0))
    o, s_new = pl.pallas_call(
        _hgrn_step_kernel,
        grid=(bsz // bb,),
        in_specs=[rblk, rblk, rblk, rblk, pblk, pblk, sblk],
        out_specs=[rblk, sblk],
        out_shape=[jax.ShapeDtypeStruct((bsz * HG_HEADS, dk), BF16),
                   jax.ShapeDtypeStruct(state.shape, F32)],
        compiler_params=_cparams(("parallel",)),
        name="hgrn_step",
    )(to_rows(q), to_rows(f), to_rows(i), to_rows(g), tile(lb), tile(gnorm), state)
    return o.reshape(bsz, d), s_new


def _lambda(lq1_ref, lk1_ref, lq2_ref, lk2_ref, lam_init):
    s1 = jnp.sum(lq1_ref[...] * lk1_ref[...], axis=-1, keepdims=True)
    s2 = jnp.sum(lq2_ref[...] * lk2_ref[...], axis=-1, keepdims=True)
    return jnp.exp(s1) - jnp.exp(s2) + lam_init


def _attn_prompt_kernel(qi_ref, kj_ref, q_ref, k_ref, v_ref, tiles_ref,
                        lq1_ref, lk1_ref, lq2_ref, lk2_ref, sub_ref, o_ref,
                        m_scr, l_scr, acc_scr, *, lam_init):
    step = pl.program_id(1)
    i = qi_ref[step]
    j = kj_ref[step]
    T = q_ref.shape[1]
    bt = tiles_ref.shape[-1]
    dv = acc_scr.shape[-1]
    dh = dv // 2
    assert T == 2 * bt

    @pl.when(j == 0)
    def _():
        m_scr[...] = jnp.full_like(m_scr, NEG_BIG)
        l_scr[...] = jnp.zeros_like(l_scr)
        acc_scr[...] = jnp.zeros_like(acc_scr)

    def diag_bias(h, s):
        t0 = tiles_ref[0, h]
        top = jnp.concatenate([s[:bt, :bt] + t0, jnp.full((bt, bt), NEG_BIG, F32)], axis=1)
        bot = jnp.concatenate([s[bt:, :bt] + tiles_ref[1, h], s[bt:, bt:] + t0], axis=1)
        return jnp.concatenate([top, bot], axis=0)

    def prev_bias(h, s):
        top = jnp.concatenate([s[:bt, :bt], s[:bt, bt:] + tiles_ref[1, h]], axis=1)
        return jnp.concatenate([top, s[bt:]], axis=0)

    def update(bias_fn):
        q = q_ref[0]
        k = k_ref[0]
        v = v_ref[0]
        for h in range(DA_HEADS):
            vh = v[:, h * dv:(h + 1) * dv]
            for c in range(2):
                idx = 2 * h + c
                lo = h * dv + c * dh
                s = lax.dot_general(q[:, lo:lo + dh], k[:, lo:lo + dh], _NT, preferred_element_type=F32)
                if bias_fn is not None:
                    s = bias_fn(h, s)
                m_prev = m_scr[idx]
                m_new = jnp.maximum(m_prev, jnp.max(s, axis=-1, keepdims=True))
                alpha = jnp.exp2(m_prev - m_new)
                chunks = [jnp.exp2(s[:, t * LANES:(t + 1) * LANES] - m_new) for t in range(T // LANES)]
                psum = chunks[0]
                for ch in chunks[1:]:
                    psum = psum + ch
                p = jnp.concatenate([ch.astype(BF16) for ch in chunks], axis=1)
                l_scr[idx] = alpha * l_scr[idx] + psum
                acc_scr[idx] = (_lane_tile(alpha, dv // LANES) * acc_scr[idx]
                                + jnp.dot(p, vh, preferred_element_type=F32))
                m_scr[idx] = m_new

    @pl.when(j == i)
    def _():
        update(diag_bias)

    @pl.when(j == i - 1)
    def _():
        update(prev_bias)

    @pl.when(j < i - 1)
    def _():
        update(None)

    @pl.when(j == i)
    def _():
        lam = _lambda(lq1_ref, lk1_ref, lq2_ref, lk2_ref, lam_init)
        sub = sub_ref[...]
        for h in range(DA_HEADS):
            inv1 = 1.0 / jnp.sum(l_scr[2 * h], axis=-1, keepdims=True)
            inv2 = 1.0 / jnp.sum(l_scr[2 * h + 1], axis=-1, keepdims=True)
            o = acc_scr[2 * h] * inv1 - lam * (acc_scr[2 * h + 1] * inv2)
            o_ref[0, :, h * dv:(h + 1) * dv] = (_rms(o, sub) * (1.0 - lam_init)).astype(o_ref.dtype)


def attn_prompt(q, k, v, tiles, lams, subln, *, lam_init):
    bsz, s, d = q.shape
    T = 2 * tiles.shape[-1]
    nq = s // T
    dv = d // DA_HEADS
    qi = jnp.asarray([i for i in range(nq) for j in range(i + 1)], jnp.int32)
    kj = jnp.asarray([j for i in range(nq) for j in range(i + 1)], jnp.int32)
    dl = lams[0].shape[-1]
    lam_spec = pl.BlockSpec((1, dl), lambda b, st, qi, kj: (0, 0))
    grid_spec = pltpu.PrefetchScalarGridSpec(
        num_scalar_prefetch=2,
        grid=(bsz, int(qi.shape[0])),
        in_specs=[pl.BlockSpec((1, T, d), lambda b, st, qi, kj: (b, qi[st], 0)),
                  pl.BlockSpec((1, T, d), lambda b, st, qi, kj: (b, kj[st], 0)),
                  pl.BlockSpec((1, T, d), lambda b, st, qi, kj: (b, kj[st], 0)),
                  pl.BlockSpec(tiles.shape, lambda b, st, qi, kj: (0, 0, 0, 0)),
                  lam_spec, lam_spec, lam_spec, lam_spec,
                  pl.BlockSpec((1, dv), lambda b, st, qi, kj: (0, 0))],
        out_specs=pl.BlockSpec((1, T, d), lambda b, st, qi, kj: (b, qi[st], 0)),
        scratch_shapes=[pltpu.VMEM((2 * DA_HEADS, T, LANES), F32),
                        pltpu.VMEM((2 * DA_HEADS, T, LANES), F32),
                        pltpu.VMEM((2 * DA_HEADS, T, dv), F32)])
    return pl.pallas_call(
        functools.partial(_attn_prompt_kernel, lam_init=lam_init),
        grid_spec=grid_spec,
        out_shape=jax.ShapeDtypeStruct((bsz, s, d), BF16),
        compiler_params=_cparams(("parallel", "arbitrary")),
        name="attn_prompt",
    )(qi, kj, q, k, v, tiles, *lams, subln.reshape(1, dv))


def _attn_decode_kernel(pt_ref, q_ref, kn_ref, vn_ref, *refs, n_pages, lam_init):
    k_refs = refs[:n_pages]
    v_refs = refs[n_pages:2 * n_pages]
    dbias_ref, lq1_ref, lk1_ref, lq2_ref, lk2_ref, sub_ref, o_ref = refs[2 * n_pages:]
    d = q_ref.shape[-1]
    dv = d // DA_HEADS
    dh = dv // 2
    nmap = 2 * DA_HEADS
    page = k_refs[0].shape[1]

    def page2d(ref):
        return jnp.concatenate([ref[0, :, h, :] for h in range(DA_HEADS)], axis=-1).astype(BF16)

    q = q_ref[0].astype(F32)
    row = lax.broadcasted_iota(jnp.int32, (nmap, d), 0)
    lane = lax.broadcasted_iota(jnp.int32, (nmap, d), 1)
    lhs = jnp.where(lane // dh == row, jnp.broadcast_to(q, (nmap, d)), 0.0).astype(BF16)

    dbias = dbias_ref[...]
    s_pages = [lax.dot_general(lhs, page2d(k_refs[r]), _NT, preferred_element_type=F32) for r in range(n_pages)]
    s_pages[-1] = s_pages[-1] + dbias[:, 0:page]
    s = jnp.concatenate(s_pages, axis=1)
    kn_b = jnp.broadcast_to(kn_ref[0], (nmap, d)).astype(BF16)
    s_new = lax.dot_general(lhs, kn_b, _NT, preferred_element_type=F32)[:, 0:1] + dbias[:, page:page + 1]
    m = jnp.maximum(jnp.max(s, axis=-1, keepdims=True), s_new)
    p = jnp.exp2(s - m)
    p_new = jnp.exp2(s_new - m)
    inv_l = 1.0 / (jnp.sum(p, axis=-1, keepdims=True) + p_new)
    p_bf = p.astype(BF16)
    acc = p_new * vn_ref[0]
    for r in range(n_pages):
        acc = acc + jnp.dot(p_bf[:, r * page:(r + 1) * page], page2d(v_refs[r]), preferred_element_type=F32)
    an = acc * inv_l
    lam = _lambda(lq1_ref, lk1_ref, lq2_ref, lk2_ref, lam_init)
    sub = sub_ref[...]
    for h in range(DA_HEADS):
        hs = slice(h * dv, (h + 1) * dv)
        o = an[2 * h:2 * h + 1, hs] - lam * an[2 * h + 1:2 * h + 2, hs]
        o_ref[0, :, hs] = (_rms(o, sub) * (1.0 - lam_init)).astype(o_ref.dtype)


def attn_decode(q, k_new, v_new, cache_k, cache_v, page_table, dbias, lams, subln, *, lam_init):
    bsz, d = q.shape
    n_pages = page_table.shape[1]
    page = cache_k.shape[1]
    assert page == LANES
    dv = d // DA_HEADS
    dl = lams[0].shape[-1]
    row_spec = pl.BlockSpec((1, 1, d), lambda b, pt: (b, 0, 0))
    page_specs = [pl.BlockSpec((1, page, DA_HEADS, dv), functools.partial(lambda b, pt, r: (pt[b, r], 0, 0, 0), r=r))
                  for r in range(n_pages)]
    lam_spec = pl.BlockSpec((1, dl), lambda b, pt: (0, 0))
    grid_spec = pltpu.PrefetchScalarGridSpec(
        num_scalar_prefetch=1,
        grid=(bsz,),
        in_specs=[row_spec, row_spec, row_spec] + page_specs + page_specs
                 + [pl.BlockSpec(dbias.shape, lambda b, pt: (0, 0)),
                    lam_spec, lam_spec, lam_spec, lam_spec,
                    pl.BlockSpec((1, dv), lambda b, pt: (0, 0))],
        out_specs=pl.BlockSpec((1, 1, d), lambda b, pt: (b, 0, 0)))
    r3 = lambda a: a.reshape(bsz, 1, d)
    out = pl.pallas_call(
        functools.partial(_attn_decode_kernel, n_pages=n_pages, lam_init=lam_init),
        grid_spec=grid_spec,
        out_shape=jax.ShapeDtypeStruct((bsz, 1, d), BF16),
        compiler_params=_cparams(("arbitrary",)),
        name="attn_decode",
    )(page_table, r3(q), r3(k_new), r3(v_new), *([cache_k] * n_pages), *([cache_v] * n_pages),
      dbias, *lams, subln.reshape(1, dv))
    return out.reshape(bsz, d)


HGRN_C = 64
TM = 1024


def _trunk(x, hg_state, kv_past, page_table, tiles, dbias, lb, p, *, batch_shape):
    m, d = x.shape
    prompt = hg_state is None
    dv = d // DA_HEADS
    scale = (dv // 2) ** -0.5
    lam_init = 0.8 - 0.6 * math.exp(-0.3 * 1)
    lams = [p[n][0].reshape(1, -1) for n in ("lambda_q1", "lambda_k1", "lambda_q2", "lambda_k2")]

    q, f, i, g = norm_matmul(x, p["norm_mix"][0], p["w_in_a"][0], [((F32, 0),)] * 4, tm=TM, tn=256, name="in_proj")
    if prompt:
        bsz, s = batch_shape
        r3 = lambda a: a.reshape(bsz, s, d)
        o, state = hgrn_prompt(r3(q), r3(f), r3(i), r3(g), lb, p["gnorm_a"][0], C=HGRN_C)
        o = o.reshape(m, d)
    else:
        o, state = hgrn_step(q, f, i, g, lb, p["gnorm_a"][0], hg_state, bb=16)
    x = matmul_res(o, p["w_out_a"][0], x, tm=TM, name="out_proj_a")
    x = ffn(x, p["norm_ffn"][0], p["w_gate_up"][0], p["w_down"][0], p["norm_final"],
            tm=TM, tf=256, final_norm=False, name="ffn0")

    k32, k16, v32, v16 = norm_matmul(x, p["kv_norm"], p["w_kv"], [((F32, DA_HEADS), (BF16, 0))] * 2,
                                     tm=TM // 2, tn=d, name="kv_proj")

    (qa,) = norm_matmul(x, p["norm_mix"][1], p["w_q_b"][0], [((BF16, 0),)], tm=TM, tn=d,
                        scale=scale * LOG2E, name="q_proj")
    if prompt:
        o = attn_prompt(r3(qa), r3(k16), r3(v16), tiles, lams, p["subln_b"][0], lam_init=lam_init)
        o = o.reshape(m, d)
    else:
        o = attn_decode(qa, k32.reshape(m, d), v32.reshape(m, d), kv_past[0], kv_past[1], page_table, dbias,
                        lams, p["subln_b"][0], lam_init=lam_init)
    x = matmul_res(o, p["w_out_b"][0], x, tm=TM, name="out_proj_b")
    y = ffn(x, p["norm_ffn"][1], p["w_gate_up"][1], p["w_down"][1], p["norm_final"],
            tm=TM, tf=256, final_norm=True, name="ffn1")
    return y, k32, v32, state


def kernel(x_prompt, x_sample, cache_k, cache_v, state_hgrn, page_table, w_in_a, lower_bound, gnorm_a, w_out_a,
           w_q_b, lambda_q1, lambda_k1, lambda_q2, lambda_k2, subln_b, w_out_b, kv_norm, w_kv, rel_bias,
           norm_mix, norm_ffn, w_gate_up, w_down, norm_final):
    p = dict(w_in_a=w_in_a, gnorm_a=gnorm_a, w_out_a=w_out_a, w_q_b=w_q_b, lambda_q1=lambda_q1,
             lambda_k1=lambda_k1, lambda_q2=lambda_q2, lambda_k2=lambda_k2, subln_b=subln_b, w_out_b=w_out_b,
             kv_norm=kv_norm, w_kv=w_kv, norm_mix=norm_mix, norm_ffn=norm_ffn, w_gate_up=w_gate_up,
             w_down=w_down, norm_final=norm_final)
    bsz, s, d = x_prompt.shape
    nb = x_sample.shape[0]
    hk = DA_HEADS, d // DA_HEADS
    tiles, dbias, lb = param_tables(rel_bias, lower_bound)

    y_p, k_p, v_p, st_p = _trunk(x_prompt.reshape(bsz * s, d), None, None, None, tiles, dbias, lb, p,
                                 batch_shape=(bsz, s))
    y_s, k_s, v_s, st_s = _trunk(x_sample.reshape(nb, d), state_hgrn[0], (cache_k, cache_v), page_table,
                                 tiles, dbias, lb, p, batch_shape=(nb, 1))
    return (y_p.reshape(bsz, s, d), y_s.reshape(nb, 1, d),
            k_p.reshape(bsz, s, *hk), v_p.reshape(bsz, s, *hk), st_p[None],
            k_s.reshape(nb, 1, *hk), v_s.reshape(nb, 1, *hk), st_s[None])
```

```python
import functools
import math

import numpy as np
import jax
import jax.numpy as jnp
from jax import lax
from jax.experimental import pallas as pl
from jax.experimental.pallas import tpu as pltpu

F32 = jnp.float32
BF16 = jnp.bfloat16
EPS = 1e-6
NEG_BIG = -1e30
LOG2E = math.log2(math.e)

HG_HEADS = 8
DA_HEADS = 4
N_BUCKETS = 32
MAX_DISTANCE = 128
LANES = 128
MXU_N = 256
VMEM_LIMIT = 56 * 1024 * 1024

_NT = (((1,), (1,)), ((), ()))
_TN = (((0,), (0,)), ((), ()))


def _cparams(sem):
    return pltpu.CompilerParams(dimension_semantics=sem, vmem_limit_bytes=VMEM_LIMIT)


def _resident(shape):
    return pl.BlockSpec(shape, lambda *_: (0,) * len(shape), pipeline_mode=pl.Buffered(1))


def _sigmoid(x):
    return 1.0 / (1.0 + jnp.exp(-x))


def _silu(x):
    return x * _sigmoid(x)


def _rms(x, g):
    ms = jnp.mean(x * x, axis=-1, keepdims=True)
    return x * lax.rsqrt(ms + EPS) * g


def _lane_tile(x, reps):
    return x if reps == 1 else jnp.concatenate([x] * reps, axis=1)


def _proj_kernel(x_ref, *refs, branches):
    nb = len(branches)
    g_refs = refs[:nb]
    w_refs = refs[nb:2 * nb]
    out_refs = refs[2 * nb:]
    x = x_ref[...]
    inv = lax.rsqrt(jnp.mean(x * x, axis=-1, keepdims=True) + EPS)
    o = 0
    for bi, (scale, groups) in enumerate(branches):
        xn = (x * inv * g_refs[bi][...]).astype(BF16)
        ng = w_refs[bi].shape[1] // len(groups)
        for gi, outs in enumerate(groups):
            acc = jnp.dot(xn, w_refs[bi][:, gi * ng:(gi + 1) * ng], preferred_element_type=F32)
            if scale != 1.0:
                acc = acc * scale
            for _ in outs:
                ref = out_refs[o]
                if len(ref.shape) == 3:
                    hd = ref.shape[2]
                    for hh in range(ref.shape[1]):
                        ref[:, hh, :] = acc[:, hh * hd:(hh + 1) * hd].astype(ref.dtype)
                else:
                    ref[...] = acc.astype(ref.dtype)
                o += 1


def proj(x, branches, *, tm, name):
    m, d = x.shape
    tm = min(tm, m)
    gains = [b[0].reshape(1, d) for b in branches]
    weights = [b[1] for b in branches]
    out_shape, out_specs = [], []
    for _, w, _, groups in branches:
        ng = w.shape[1] // len(groups)
        for outs in groups:
            for dt, heads in outs:
                if heads:
                    out_shape.append(jax.ShapeDtypeStruct((m, heads, ng // heads), dt))
                    out_specs.append(pl.BlockSpec((tm, heads, ng // heads), lambda i: (i, 0, 0)))
                else:
                    out_shape.append(jax.ShapeDtypeStruct((m, ng), dt))
                    out_specs.append(pl.BlockSpec((tm, ng), lambda i: (i, 0)))
    return pl.pallas_call(
        functools.partial(_proj_kernel, branches=[(b[2], b[3]) for b in branches]),
        grid=(m // tm,),
        in_specs=([pl.BlockSpec((tm, d), lambda i: (i, 0))]
                  + [_resident(g.shape) for g in gains] + [_resident(w.shape) for w in weights]),
        out_specs=out_specs,
        out_shape=out_shape,
        compiler_params=_cparams(("parallel",)),
        name=name,
    )(x, *gains, *weights)


def _mix_ffn_kernel(a_ref, x_ref, wo_ref, g_ref, wgu_ref, wd_ref, gf_ref, o_ref, x1_scr, hid_scr, *, final_norm):
    dff = wd_ref.shape[0]
    x1 = x_ref[...] + jnp.dot(a_ref[...], wo_ref[...], preferred_element_type=F32)
    x1_scr[...] = x1
    xn = _rms(x1, g_ref[...]).astype(BF16)
    for j in range(dff // MXU_N):
        cs = slice(j * MXU_N, (j + 1) * MXU_N)
        gt = jnp.dot(xn, wgu_ref[:, cs], preferred_element_type=F32)
        ut = jnp.dot(xn, wgu_ref[:, dff + j * MXU_N:dff + (j + 1) * MXU_N], preferred_element_type=F32)
        hid_scr[:, cs] = (_silu(gt) * ut).astype(BF16)
    y = x1_scr[...] + jnp.dot(hid_scr[...], wd_ref[...], preferred_element_type=F32)
    if final_norm:
        y = _rms(y, gf_ref[...])
    o_ref[...] = y


def mix_ffn(a, x, w_out, g, w_gate_up, w_down, g_final, *, tm, final_norm, name):
    m, d = x.shape
    dff = w_down.shape[0]
    assert dff % MXU_N == 0
    tm = min(tm, m)
    row = lambda width: pl.BlockSpec((tm, width), lambda i: (i, 0))
    return pl.pallas_call(
        functools.partial(_mix_ffn_kernel, final_norm=final_norm),
        grid=(m // tm,),
        in_specs=[row(a.shape[1]), row(d), _resident(w_out.shape), _resident((1, d)),
                  _resident(w_gate_up.shape), _resident(w_down.shape), _resident((1, d))],
        out_specs=row(d),
        out_shape=jax.ShapeDtypeStruct((m, d), F32),
        scratch_shapes=[pltpu.VMEM((tm, d), F32), pltpu.VMEM((tm, dff), BF16)],
        compiler_params=_cparams(("parallel",)),
        name=name,
    )(a, x, w_out, g.reshape(1, d), w_gate_up, w_down, g_final.reshape(1, d))


BIAS_T = 2 * MAX_DISTANCE


def _bucket_bias(n, rb_ref, n_heads):
    max_exact = N_BUCKETS // 2
    nf = jnp.maximum(n, 1).astype(F32)
    large = max_exact + (jnp.log(nf / max_exact) / math.log(MAX_DISTANCE / max_exact)
                         * (N_BUCKETS - max_exact)).astype(jnp.int32)
    large = jnp.minimum(large, N_BUCKETS - 1)
    bucket = jnp.where(n < max_exact, n, large)
    outs = []
    for h in range(n_heads):
        far = rb_ref[N_BUCKETS - 1, h]
        acc = jnp.zeros(n.shape, F32)
        for b in range(N_BUCKETS - 1):
            acc = jnp.where(bucket == b, (rb_ref[b, h] - far) * LOG2E, acc)
        outs.append(acc)
    return outs


def _tables_kernel(rb_ref, lb_in_ref, tiles_ref, dec_ref, lb_ref):
    bt = tiles_ref.shape[-1]
    r = lax.broadcasted_iota(jnp.int32, (bt, bt), 0)
    c = lax.broadcasted_iota(jnp.int32, (bt, bt), 1)
    n0 = r - c
    diag = _bucket_bias(jnp.maximum(n0, 0), rb_ref, DA_HEADS)
    prev = _bucket_bias(bt + n0, rb_ref, DA_HEADS)
    for h in range(DA_HEADS):
        tiles_ref[0, h] = jnp.where(n0 >= 0, diag[h], NEG_BIG)
        tiles_ref[1, h] = prev[h]
    lane = lax.broadcasted_iota(jnp.int32, (1, 2 * LANES), 1)
    nd = jnp.where(lane < LANES, LANES - lane, 0)
    dec = _bucket_bias(nd, rb_ref, DA_HEADS)
    dec_ref[...] = jnp.concatenate([dec[h] for h in range(DA_HEADS) for _ in range(2)], axis=0)
    lbi = lb_in_ref[...]
    mx = jnp.max(lbi, axis=0, keepdims=True)
    e = jnp.exp(lbi - mx)
    lb_ref[...] = e[0:1, :] / jnp.sum(e, axis=0, keepdims=True)


def param_tables(rel_bias, lower_bound):
    d = lower_bound.shape[1]
    return pl.pallas_call(
        _tables_kernel,
        in_specs=[pl.BlockSpec(memory_space=pltpu.SMEM),
                  pl.BlockSpec(memory_space=pltpu.VMEM)],
        out_specs=[pl.BlockSpec(memory_space=pltpu.VMEM)] * 3,
        out_shape=[jax.ShapeDtypeStruct((2, DA_HEADS, BIAS_T, BIAS_T), F32),
                   jax.ShapeDtypeStruct((2 * DA_HEADS, 2 * LANES), F32),
                   jax.ShapeDtypeStruct((1, d), F32)],
        compiler_params=pltpu.CompilerParams(vmem_limit_bytes=VMEM_LIMIT),
        name="param_tables",
    )(rel_bias, lower_bound)


def _hgrn_sum_matrices(C):
    levels = int(math.log2(C))
    t = np.arange(C)
    u = t[None, :]
    mats = [u <= t[:, None], u > t[:, None]]
    for lv in range(levels):
        c = 1 << lv
        e = (t // (2 * c)) * (2 * c) + c - 1
        upper = (t > e)[:, None]
        seg = np.where(upper, (u > e[:, None]) & (u <= t[:, None]), (u > t[:, None]) & (u <= e[:, None]))
        mats.append(seg)
    return np.concatenate(mats, axis=0).astype(np.float32), levels


def _split2(x):
    hi = x.astype(BF16)
    lo = (x - hi.astype(F32)).astype(BF16)
    return hi, lo


def _hgrn_prompt_kernel(q_ref, f_ref, i_ref, g_ref, lb_ref, gn_ref, mall_ref, o_ref, st_ref, s_scr, *, C, levels):
    cidx = pl.program_id(1)
    dk = s_scr.shape[-1]

    @pl.when(cidx == 0)
    def _():
        s_scr[...] = jnp.zeros_like(s_scr)

    lb = lb_ref[...]
    q = _silu(q_ref[0])
    fg = lb + (1.0 - lb) * _sigmoid(f_ref[0])
    kk = 1.0 - fg
    logf = jnp.log(fg)
    v = i_ref[0]
    v_bf = v.astype(BF16)

    mall = mall_ref[...]
    hi, lo = _split2(logf)
    sums = jnp.dot(mall, hi, preferred_element_type=F32) + jnp.dot(mall, lo, preferred_element_type=F32)
    b = sums[0:C]
    suf = sums[C:2 * C]
    qe = (q * jnp.exp(b)).astype(BF16)
    kt = (kk * jnp.exp(suf)).astype(BF16)
    dec_last = jnp.exp(b[C - 1:C, :])

    ti = lax.broadcasted_iota(jnp.int32, (C, C), 0)
    si = lax.broadcasted_iota(jnp.int32, (C, C), 1)
    txs = jnp.bitwise_xor(ti, si)
    lower = ti > si
    scores = [jnp.zeros((C, C), F32) for _ in range(HG_HEADS)]
    for lv in range(levels):
        e = jnp.exp(sums[(2 + lv) * C:(3 + lv) * C])
        a_bf = (q * e).astype(BF16)
        b_bf = (kk * e).astype(BF16)
        valid = jnp.logical_and(jnp.right_shift(txs, lv) == 1, lower)
        for h in range(HG_HEADS):
            hs = slice(h * dk, (h + 1) * dk)
            sl = lax.dot_general(a_bf[:, hs], b_bf[:, hs], _NT, preferred_element_type=F32)
            scores[h] = scores[h] + jnp.where(valid, sl, 0.0)

    qk = q * kk
    g = g_ref[0]
    gn = gn_ref[...]
    for h in range(HG_HEADS):
        hs = slice(h * dk, (h + 1) * dk)
        st = s_scr[h]
        diag = jnp.sum(qk[:, hs], axis=-1, keepdims=True)
        o = (jnp.dot(scores[h].astype(BF16), v_bf[:, hs], preferred_element_type=F32)
             + diag * v[:, hs]
             + lax.dot_general(qe[:, hs], st.astype(BF16), _NT, preferred_element_type=F32))
        s_scr[h] = st * dec_last[:, hs] + lax.dot_general(v_bf[:, hs], kt[:, hs], _TN, preferred_element_type=F32)
        o_ref[0, :, hs] = (_rms(o, gn[:, hs]) * _silu(g[:, hs])).astype(o_ref.dtype)

    @pl.when(cidx == pl.num_programs(1) - 1)
    def _():
        for h in range(HG_HEADS):
            st_ref[0, 0, h] = s_scr[h].T


def hgrn_prompt(q, f, i, g, lb, gnorm, *, C):
    bsz, s, d = q.shape
    dk = d // HG_HEADS
    mall_np, levels = _hgrn_sum_matrices(C)
    mall = jnp.asarray(mall_np, dtype=BF16)
    blk = pl.BlockSpec((1, C, d), lambda b, c: (b, c, 0))
    vec = pl.BlockSpec((1, d), lambda b, c: (0, 0))
    return pl.pallas_call(
        functools.partial(_hgrn_prompt_kernel, C=C, levels=levels),
        grid=(bsz, s // C),
        in_specs=[blk, blk, blk, blk, vec, vec,
                  pl.BlockSpec(mall.shape, lambda b, c: (0, 0))],
        out_specs=[pl.BlockSpec((1, C, d), lambda b, c: (b, c, 0)),
                   pl.BlockSpec((1, 1, HG_HEADS, dk, dk), lambda b, c: (0, b, 0, 0, 0))],
        out_shape=[jax.ShapeDtypeStruct((bsz, s, d), BF16),
                   jax.ShapeDtypeStruct((1, bsz, HG_HEADS, dk, dk), F32)],
        scratch_shapes=[pltpu.VMEM((HG_HEADS, dk, dk), F32)],
        compiler_params=_cparams(("parallel", "arbitrary")),
        name="hgrn_prompt",
    )(q, f, i, g, lb, gnorm.reshape(1, d), mall)


def _hgrn_step_kernel(q_ref, f_ref, i_ref, g_ref, lb_ref, gn_ref, s_ref, o_ref, so_ref):
    rows = q_ref.shape[0]
    lb = lb_ref[...]
    fg = lb + (1.0 - lb) * _sigmoid(f_ref[...])
    q_t = _silu(q_ref[...]).T
    fg_t = fg.T
    kk_t = (1.0 - fg).T
    v = i_ref[...]
    outs = []
    for r in range(rows):
        bl, h = divmod(r, HG_HEADS)
        s_new = fg_t[:, r:r + 1] * s_ref[0, bl, h] + kk_t[:, r:r + 1] * v[r:r + 1, :]
        so_ref[0, bl, h] = s_new
        outs.append(jnp.sum(q_t[:, r:r + 1] * s_new, axis=0, keepdims=True))
    o = jnp.concatenate(outs, axis=0)
    o_ref[...] = (_rms(o, gn_ref[...]) * _silu(g_ref[...])).astype(o_ref.dtype)


def hgrn_step(q, f, i, g, lb, gnorm, state, *, bb):
    bsz, d = q.shape
    dk = d // HG_HEADS
    rows = bb * HG_HEADS
    to_rows = lambda a: a.reshape(bsz * HG_HEADS, dk)
    tile = lambda p: jnp.tile(p.reshape(HG_HEADS, dk), (bb, 1))
    rblk = pl.BlockSpec((rows, dk), lambda b: (b, 0))
    pblk = pl.BlockSpec((rows, dk), lambda b: (0, 0))
    sblk = pl.BlockSpec((1, bb, HG_HEADS, dk, dk), lambda b: (0, b, 0, 0, 0))
    o, s_new = pl.pallas_call(
        _hgrn_step_kernel,
        grid=(bsz // bb,),
        in_specs=[rblk, rblk, rblk, rblk, pblk, pblk, sblk],
        out_specs=[rblk, sblk],
        out_shape=[jax.ShapeDtypeStruct((bsz * HG_HEADS, dk), BF16),
                   jax.ShapeDtypeStruct(state.shape, F32)],
        compiler_params=_cparams(("parallel",)),
        name="hgrn_step",
    )(to_rows(q), to_rows(f), to_rows(i), to_rows(g), tile(lb), tile(gnorm), state)
    return o.reshape(bsz, d), s_new


def _lambda(lq1_ref, lk1_ref, lq2_ref, lk2_ref, lam_init):
    s1 = jnp.sum(lq1_ref[...] * lk1_ref[...], axis=-1, keepdims=True)
    s2 = jnp.sum(lq2_ref[...] * lk2_ref[...], axis=-1, keepdims=True)
    return jnp.exp(s1) - jnp.exp(s2) + lam_init


def _attn_prompt_kernel(qi_ref, kj_ref, q_ref, k_ref, v_ref, tiles_ref,
                        lq1_ref, lk1_ref, lq2_ref, lk2_ref, sub_ref, o_ref,
                        m_scr, l_scr, acc_scr, *, lam_init):
    step = pl.program_id(1)
    i = qi_ref[step]
    j = kj_ref[step]
    T = q_ref.shape[1]
    bt = tiles_ref.shape[-1]
    dv = acc_scr.shape[-1]
    dh = dv // 2
    assert T == 2 * bt

    @pl.when(j == 0)
    def _():
        m_scr[...] = jnp.full_like(m_scr, NEG_BIG)
        l_scr[...] = jnp.zeros_like(l_scr)
        acc_scr[...] = jnp.zeros_like(acc_scr)

    def diag_bias(h, s):
        t0 = tiles_ref[0, h]
        top = jnp.concatenate([s[:bt, :bt] + t0, jnp.full((bt, bt), NEG_BIG, F32)], axis=1)
        bot = jnp.concatenate([s[bt:, :bt] + tiles_ref[1, h], s[bt:, bt:] + t0], axis=1)
        return jnp.concatenate([top, bot], axis=0)

    def prev_bias(h, s):
        top = jnp.concatenate([s[:bt, :bt], s[:bt, bt:] + tiles_ref[1, h]], axis=1)
        return jnp.concatenate([top, s[bt:]], axis=0)

    def update(bias_fn):
        q = q_ref[0]
        k = k_ref[0]
        v = v_ref[0]
        for h in range(DA_HEADS):
            vh = v[:, h * dv:(h + 1) * dv]
            for c in range(2):
                idx = 2 * h + c
                lo = h * dv + c * dh
                s = lax.dot_general(q[:, lo:lo + dh], k[:, lo:lo + dh], _NT, preferred_element_type=F32)
                if bias_fn is not None:
                    s = bias_fn(h, s)
                m_prev = m_scr[idx]
                m_new = jnp.maximum(m_prev, jnp.max(s, axis=-1, keepdims=True))
                alpha = jnp.exp2(m_prev - m_new)
                chunks = [jnp.exp2(s[:, t * LANES:(t + 1) * LANES] - m_new) for t in range(T // LANES)]
                psum = chunks[0]
                for ch in chunks[1:]:
                    psum = psum + ch
                p = jnp.concatenate([ch.astype(BF16) for ch in chunks], axis=1)
                l_scr[idx] = alpha * l_scr[idx] + psum
                acc_scr[idx] = (_lane_tile(alpha, dv // LANES) * acc_scr[idx]
                                + jnp.dot(p, vh, preferred_element_type=F32))
                m_scr[idx] = m_new

    @pl.when(j == i)
    def _():
        update(diag_bias)

    @pl.when(j == i - 1)
    def _():
        update(prev_bias)

    @pl.when(j < i - 1)
    def _():
        update(None)

    @pl.when(j == i)
    def _():
        lam = _lambda(lq1_ref, lk1_ref, lq2_ref, lk2_ref, lam_init)
        sub = sub_ref[...]
        for h in range(DA_HEADS):
            inv1 = 1.0 / jnp.sum(l_scr[2 * h], axis=-1, keepdims=True)
            inv2 = 1.0 / jnp.sum(l_scr[2 * h + 1], axis=-1, keepdims=True)
            o = acc_scr[2 * h] * inv1 - lam * (acc_scr[2 * h + 1] * inv2)
            o_ref[0, :, h * dv:(h + 1) * dv] = (_rms(o, sub) * (1.0 - lam_init)).astype(o_ref.dtype)


def attn_prompt(q, k, v, tiles, lams, subln, *, lam_init):
    bsz, s, d = q.shape
    T = 2 * tiles.shape[-1]
    nq = s // T
    dv = d // DA_HEADS
    qi = jnp.asarray([i for i in range(nq) for j in range(i + 1)], jnp.int32)
    kj = jnp.asarray([j for i in range(nq) for j in range(i + 1)], jnp.int32)
    dl = lams[0].shape[-1]
    lam_spec = pl.BlockSpec((1, dl), lambda b, st, qi, kj: (0, 0))
    grid_spec = pltpu.PrefetchScalarGridSpec(
        num_scalar_prefetch=2,
        grid=(bsz, int(qi.shape[0])),
        in_specs=[pl.BlockSpec((1, T, d), lambda b, st, qi, kj: (b, qi[st], 0)),
                  pl.BlockSpec((1, T, d), lambda b, st, qi, kj: (b, kj[st], 0)),
                  pl.BlockSpec((1, T, d), lambda b, st, qi, kj: (b, kj[st], 0)),
                  pl.BlockSpec(tiles.shape, lambda b, st, qi, kj: (0, 0, 0, 0)),
                  lam_spec, lam_spec, lam_spec, lam_spec,
                  pl.BlockSpec((1, dv), lambda b, st, qi, kj: (0, 0))],
        out_specs=pl.BlockSpec((1, T, d), lambda b, st, qi, kj: (b, qi[st], 0)),
        scratch_shapes=[pltpu.VMEM((2 * DA_HEADS, T, LANES), F32),
                        pltpu.VMEM((2 * DA_HEADS, T, LANES), F32),
                        pltpu.VMEM((2 * DA_HEADS, T, dv), F32)])
    return pl.pallas_call(
        functools.partial(_attn_prompt_kernel, lam_init=lam_init),
        grid_spec=grid_spec,
        out_shape=jax.ShapeDtypeStruct((bsz, s, d), BF16),
        compiler_params=_cparams(("parallel", "arbitrary")),
        name="attn_prompt",
    )(qi, kj, q, k, v, tiles, *lams, subln.reshape(1, dv))


def _attn_decode_kernel(pt_ref, q_ref, kn_ref, vn_ref, *refs, n_pages, lam_init):
    k_refs = refs[:n_pages]
    v_refs = refs[n_pages:2 * n_pages]
    dbias_ref, lq1_ref, lk1_ref, lq2_ref, lk2_ref, sub_ref, o_ref = refs[2 * n_pages:]
    d = q_ref.shape[-1]
    dv = d // DA_HEADS
    dh = dv // 2
    nmap = 2 * DA_HEADS
    page = k_refs[0].shape[1]

    def page2d(ref):
        return jnp.concatenate([ref[0, :, h, :] for h in range(DA_HEADS)], axis=-1).astype(BF16)

    q = q_ref[0].astype(F32)
    row = lax.broadcasted_iota(jnp.int32, (nmap, d), 0)
    lane = lax.broadcasted_iota(jnp.int32, (nmap, d), 1)
    lhs = jnp.where(lane // dh == row, jnp.broadcast_to(q, (nmap, d)), 0.0).astype(BF16)

    dbias = dbias_ref[...]
    s_pages = [lax.dot_general(lhs, page2d(k_refs[r]), _NT, preferred_element_type=F32) for r in range(n_pages)]
    s_pages[-1] = s_pages[-1] + dbias[:, 0:page]
    s = jnp.concatenate(s_pages, axis=1)
    kn_b = jnp.broadcast_to(kn_ref[0], (nmap, d)).astype(BF16)
    s_new = lax.dot_general(lhs, kn_b, _NT, preferred_element_type=F32)[:, 0:1] + dbias[:, page:page + 1]
    m = jnp.maximum(jnp.max(s, axis=-1, keepdims=True), s_new)
    p = jnp.exp2(s - m)
    p_new = jnp.exp2(s_new - m)
    inv_l = 1.0 / (jnp.sum(p, axis=-1, keepdims=True) + p_new)
    p_bf = p.astype(BF16)
    acc = p_new * vn_ref[0]
    for r in range(n_pages):
        acc = acc + jnp.dot(p_bf[:, r * page:(r + 1) * page], page2d(v_refs[r]), preferred_element_type=F32)
    an = acc * inv_l
    lam = _lambda(lq1_ref, lk1_ref, lq2_ref, lk2_ref, lam_init)
    sub = sub_ref[...]
    for h in range(DA_HEADS):
        hs = slice(h * dv, (h + 1) * dv)
        o = an[2 * h:2 * h + 1, hs] - lam * an[2 * h + 1:2 * h + 2, hs]
        o_ref[0, :, hs] = (_rms(o, sub) * (1.0 - lam_init)).astype(o_ref.dtype)


def attn_decode(q, k_new, v_new, cache_k, cache_v, page_table, dbias, lams, subln, *, lam_init):
    bsz, d = q.shape
    n_pages = page_table.shape[1]
    page = cache_k.shape[1]
    assert page == LANES
    dv = d // DA_HEADS
    dl = lams[0].shape[-1]
    row_spec = pl.BlockSpec((1, 1, d), lambda b, pt: (b, 0, 0))
    page_specs = [pl.BlockSpec((1, page, DA_HEADS, dv), functools.partial(lambda b, pt, r: (pt[b, r], 0, 0, 0), r=r))
                  for r in range(n_pages)]
    lam_spec = pl.BlockSpec((1, dl), lambda b, pt: (0, 0))
    grid_spec = pltpu.PrefetchScalarGridSpec(
        num_scalar_prefetch=1,
        grid=(bsz,),
        in_specs=[row_spec, row_spec, row_spec] + page_specs + page_specs
                 + [pl.BlockSpec(dbias.shape, lambda b, pt: (0, 0)),
                    lam_spec, lam_spec, lam_spec, lam_spec,
                    pl.BlockSpec((1, dv), lambda b, pt: (0, 0))],
        out_specs=pl.BlockSpec((1, 1, d), lambda b, pt: (b, 0, 0)))
    r3 = lambda a: a.reshape(bsz, 1, d)
    out = pl.pallas_call(
        functools.partial(_attn_decode_kernel, n_pages=n_pages, lam_init=lam_init),
        grid_spec=grid_spec,
        out_shape=jax.ShapeDtypeStruct((bsz, 1, d), BF16),
        compiler_params=_cparams(("arbitrary",)),
        name="attn_decode",
    )(page_table, r3(q), r3(k_new), r3(v_new), *([cache_k] * n_pages), *([cache_v] * n_pages),
      dbias, *lams, subln.reshape(1, dv))
    return out.reshape(bsz, d)


HGRN_C = 64
TM = 512


def _trunk(x, hg_state, kv_past, page_table, tiles, dbias, lb, p, *, batch_shape):
    m, d = x.shape
    prompt = hg_state is None
    dv = d // DA_HEADS
    scale = (dv // 2) ** -0.5
    lam_init = 0.8 - 0.6 * math.exp(-0.3 * 1)
    lams = [p[n][0].reshape(1, -1) for n in ("lambda_q1", "lambda_k1", "lambda_q2", "lambda_k2")]

    q, f, i, g = proj(x, [(p["norm_mix"][0], p["w_in_a"][0], 1.0, [[(F32, 0)]] * 4)], tm=TM, name="in_proj")
    if prompt:
        bsz, s = batch_shape
        r3 = lambda a: a.reshape(bsz, s, d)
        o, state = hgrn_prompt(r3(q), r3(f), r3(i), r3(g), lb, p["gnorm_a"][0], C=HGRN_C)
        o = o.reshape(m, d)
    else:
        o, state = hgrn_step(q, f, i, g, lb, p["gnorm_a"][0], hg_state, bb=16)
    x = mix_ffn(o, x, p["w_out_a"][0], p["norm_ffn"][0], p["w_gate_up"][0], p["w_down"][0], p["norm_final"],
                tm=TM, final_norm=False, name="mix_ffn0")

    k32, k16, v32, v16, qa = proj(
        x, [(p["kv_norm"], p["w_kv"], 1.0, [[(F32, DA_HEADS), (BF16, 0)]] * 2),
            (p["norm_mix"][1], p["w_q_b"][0], scale * LOG2E, [[(BF16, 0)]])],
        tm=TM, name="kvq_proj")

    if prompt:
        o = attn_prompt(r3(qa), r3(k16), r3(v16), tiles, lams, p["subln_b"][0], lam_init=lam_init)
        o = o.reshape(m, d)
    else:
        o = attn_decode(qa, k32.reshape(m, d), v32.reshape(m, d), kv_past[0], kv_past[1], page_table, dbias,
                        lams, p["subln_b"][0], lam_init=lam_init)
    y = mix_ffn(o, x, p["w_out_b"][0], p["norm_ffn"][1], p["w_gate_up"][1], p["w_down"][1], p["norm_final"],
                tm=TM, final_norm=True, name="mix_ffn1")
    return y, k32, v32, state


def kernel(x_prompt, x_sample, cache_k, cache_v, state_hgrn, page_table, w_in_a, lower_bound, gnorm_a, w_out_a,
           w_q_b, lambda_q1, lambda_k1, lambda_q2, lambda_k2, subln_b, w_out_b, kv_norm, w_kv, rel_bias,
           norm_mix, norm_ffn, w_gate_up, w_down, norm_final):
    bf = lambda w: w.astype(BF16)
    p = dict(w_in_a=bf(w_in_a), gnorm_a=gnorm_a, w_out_a=bf(w_out_a), w_q_b=bf(w_q_b), lambda_q1=lambda_q1,
             lambda_k1=lambda_k1, lambda_q2=lambda_q2, lambda_k2=lambda_k2, subln_b=subln_b, w_out_b=bf(w_out_b),
             kv_norm=kv_norm, w_kv=bf(w_kv), norm_mix=norm_mix, norm_ffn=norm_ffn, w_gate_up=bf(w_gate_up),
             w_down=bf(w_down), norm_final=norm_final)
    bsz, s, d = x_prompt.shape
    nb = x_sample.shape[0]
    hk = DA_HEADS, d // DA_HEADS
    tiles, dbias, lb = param_tables(rel_bias, lower_bound)

    y_p, k_p, v_p, st_p = _trunk(x_prompt.reshape(bsz * s, d), None, None, None, tiles, dbias, lb, p,
                                 batch_shape=(bsz, s))
    y_s, k_s, v_s, st_s = _trunk(x_sample.reshape(nb, d), state_hgrn, (cache_k, cache_v), page_table,
                                 tiles, dbias, lb, p, batch_shape=(nb, 1))
    return (y_p.reshape(bsz, s, d), y_s.reshape(nb, 1, d),
            k_p.reshape(bsz, s, *hk), v_p.reshape(bsz, s, *hk), st_p,
            k_s.reshape(nb, 1, *hk), v_s.reshape(nb, 1, *hk), st_s)
```

```python
import functools
import math

import numpy as np
import jax
import jax.numpy as jnp
from jax import lax
from jax.experimental import pallas as pl
from jax.experimental.pallas import tpu as pltpu

F32 = jnp.float32
BF16 = jnp.bfloat16
EPS = 1e-6
NEG_BIG = -1e30
LOG2E = math.log2(math.e)

HG_HEADS = 8
DA_HEADS = 4
N_BUCKETS = 32
MAX_DISTANCE = 128
LANES = 128
MXU_N = 256
VMEM_LIMIT = 56 * 1024 * 1024

_NT = (((1,), (1,)), ((), ()))
_TN = (((0,), (0,)), ((), ()))


def _cparams(sem):
    return pltpu.CompilerParams(dimension_semantics=sem, vmem_limit_bytes=VMEM_LIMIT)


def _resident(shape):
    return pl.BlockSpec(shape, lambda *_: (0,) * len(shape), pipeline_mode=pl.Buffered(1))


def _sigmoid(x):
    return 1.0 / (1.0 + jnp.exp(-x))


def _silu(x):
    return x * _sigmoid(x)


def _rms(x, g):
    ms = jnp.mean(x * x, axis=-1, keepdims=True)
    return x * lax.rsqrt(ms + EPS) * g


def _lane_tile(x, reps):
    return x if reps == 1 else jnp.concatenate([x] * reps, axis=1)


def _proj_kernel(x_ref, *refs, branches):
    nb = len(branches)
    g_refs = refs[:nb]
    w_refs = refs[nb:2 * nb]
    out_refs = refs[2 * nb:]
    x = x_ref[...]
    inv = lax.rsqrt(jnp.mean(x * x, axis=-1, keepdims=True) + EPS)
    o = 0
    for bi, (scale, groups) in enumerate(branches):
        xn = (x * inv * g_refs[bi][...]).astype(BF16)
        ng = w_refs[bi].shape[1] // len(groups)
        for gi, outs in enumerate(groups):
            acc = jnp.dot(xn, w_refs[bi][:, gi * ng:(gi + 1) * ng], preferred_element_type=F32)
            if scale != 1.0:
                acc = acc * scale
            for _ in outs:
                ref = out_refs[o]
                if len(ref.shape) == 3:
                    hd = ref.shape[2]
                    for hh in range(ref.shape[1]):
                        ref[:, hh, :] = acc[:, hh * hd:(hh + 1) * hd].astype(ref.dtype)
                else:
                    ref[...] = acc.astype(ref.dtype)
                o += 1


def proj(x, branches, *, tm, name):
    m, d = x.shape
    tm = min(tm, m)
    gains = [b[0].reshape(1, d) for b in branches]
    weights = [b[1] for b in branches]
    out_shape, out_specs = [], []
    for _, w, _, groups in branches:
        ng = w.shape[1] // len(groups)
        for outs in groups:
            for dt, heads in outs:
                if heads:
                    out_shape.append(jax.ShapeDtypeStruct((m, heads, ng // heads), dt))
                    out_specs.append(pl.BlockSpec((tm, heads, ng // heads), lambda i: (i, 0, 0)))
                else:
                    out_shape.append(jax.ShapeDtypeStruct((m, ng), dt))
                    out_specs.append(pl.BlockSpec((tm, ng), lambda i: (i, 0)))
    return pl.pallas_call(
        functools.partial(_proj_kernel, branches=[(b[2], b[3]) for b in branches]),
        grid=(m // tm,),
        in_specs=([pl.BlockSpec((tm, d), lambda i: (i, 0))]
                  + [_resident(g.shape) for g in gains] + [_resident(w.shape) for w in weights]),
        out_specs=out_specs,
        out_shape=out_shape,
        compiler_params=_cparams(("parallel",)),
        name=name,
    )(x, *gains, *weights)


def _mix_ffn_kernel(a_ref, x_ref, wo_ref, g_ref, wgu_ref, wd_ref, gf_ref, o_ref, x1_scr, hid_scr, *, final_norm):
    dff = wd_ref.shape[0]
    x1 = x_ref[...] + jnp.dot(a_ref[...], wo_ref[...], preferred_element_type=F32)
    x1_scr[...] = x1
    xn = _rms(x1, g_ref[...]).astype(BF16)
    for j in range(dff // MXU_N):
        cs = slice(j * MXU_N, (j + 1) * MXU_N)
        gt = jnp.dot(xn, wgu_ref[:, cs], preferred_element_type=F32)
        ut = jnp.dot(xn, wgu_ref[:, dff + j * MXU_N:dff + (j + 1) * MXU_N], preferred_element_type=F32)
        hid_scr[:, cs] = (_silu(gt) * ut).astype(BF16)
    y = x1_scr[...] + jnp.dot(hid_scr[...], wd_ref[...], preferred_element_type=F32)
    if final_norm:
        y = _rms(y, gf_ref[...])
    o_ref[...] = y


def mix_ffn(a, x, w_out, g, w_gate_up, w_down, g_final, *, tm, final_norm, name):
    m, d = x.shape
    dff = w_down.shape[0]
    assert dff % MXU_N == 0
    tm = min(tm, m)
    row = lambda width: pl.BlockSpec((tm, width), lambda i: (i, 0))
    return pl.pallas_call(
        functools.partial(_mix_ffn_kernel, final_norm=final_norm),
        grid=(m // tm,),
        in_specs=[row(a.shape[1]), row(d), _resident(w_out.shape), _resident((1, d)),
                  _resident(w_gate_up.shape), _resident(w_down.shape), _resident((1, d))],
        out_specs=row(d),
        out_shape=jax.ShapeDtypeStruct((m, d), F32),
        scratch_shapes=[pltpu.VMEM((tm, d), F32), pltpu.VMEM((tm, dff), BF16)],
        compiler_params=_cparams(("parallel",)),
        name=name,
    )(a, x, w_out, g.reshape(1, d), w_gate_up, w_down, g_final.reshape(1, d))


BIAS_T = 2 * MAX_DISTANCE


def _bucket_bias(n, rb_ref, n_heads):
    max_exact = N_BUCKETS // 2
    nf = jnp.maximum(n, 1).astype(F32)
    large = max_exact + (jnp.log(nf / max_exact) / math.log(MAX_DISTANCE / max_exact)
                         * (N_BUCKETS - max_exact)).astype(jnp.int32)
    large = jnp.minimum(large, N_BUCKETS - 1)
    bucket = jnp.where(n < max_exact, n, large)
    outs = []
    for h in range(n_heads):
        far = rb_ref[N_BUCKETS - 1, h]
        acc = jnp.zeros(n.shape, F32)
        for b in range(N_BUCKETS - 1):
            acc = jnp.where(bucket == b, (rb_ref[b, h] - far) * LOG2E, acc)
        outs.append(acc)
    return outs


def _tables_kernel(rb_ref, lb_in_ref, tiles_ref, dec_ref, lb_ref):
    bt = tiles_ref.shape[-1]
    r = lax.broadcasted_iota(jnp.int32, (bt, bt), 0)
    c = lax.broadcasted_iota(jnp.int32, (bt, bt), 1)
    n0 = r - c
    diag = _bucket_bias(jnp.maximum(n0, 0), rb_ref, DA_HEADS)
    prev = _bucket_bias(bt + n0, rb_ref, DA_HEADS)
    for h in range(DA_HEADS):
        tiles_ref[0, h] = jnp.where(n0 >= 0, diag[h], NEG_BIG)
        tiles_ref[1, h] = prev[h]
    page_rows = dec_ref.shape[0] - 8
    rr = lax.broadcasted_iota(jnp.int32, dec_ref.shape, 0)
    ll = lax.broadcasted_iota(jnp.int32, dec_ref.shape, 1)
    nd = jnp.where(rr < page_rows, page_rows // DA_HEADS - rr // DA_HEADS, 0)
    dec = _bucket_bias(nd, rb_ref, DA_HEADS)
    acc = jnp.zeros(dec_ref.shape, F32)
    for h in range(DA_HEADS):
        acc = jnp.where(ll // 2 == h, dec[h], acc)
    dec_ref[...] = acc
    lbi = lb_in_ref[...]
    mx = jnp.max(lbi, axis=0, keepdims=True)
    e = jnp.exp(lbi - mx)
    lb_ref[...] = e[0:1, :] / jnp.sum(e, axis=0, keepdims=True)


def param_tables(rel_bias, lower_bound, page):
    d = lower_bound.shape[1]
    return pl.pallas_call(
        _tables_kernel,
        in_specs=[pl.BlockSpec(memory_space=pltpu.SMEM),
                  pl.BlockSpec(memory_space=pltpu.VMEM)],
        out_specs=[pl.BlockSpec(memory_space=pltpu.VMEM)] * 3,
        out_shape=[jax.ShapeDtypeStruct((2, DA_HEADS, BIAS_T, BIAS_T), F32),
                   jax.ShapeDtypeStruct((page * DA_HEADS + 8, LANES), F32),
                   jax.ShapeDtypeStruct((1, d), F32)],
        compiler_params=pltpu.CompilerParams(vmem_limit_bytes=VMEM_LIMIT),
        name="param_tables",
    )(rel_bias, lower_bound)


def _hgrn_sum_matrices(C):
    levels = int(math.log2(C))
    t = np.arange(C)
    u = t[None, :]
    mats = [u <= t[:, None], u > t[:, None]]
    for lv in range(levels):
        c = 1 << lv
        e = (t // (2 * c)) * (2 * c) + c - 1
        upper = (t > e)[:, None]
        seg = np.where(upper, (u > e[:, None]) & (u <= t[:, None]), (u > t[:, None]) & (u <= e[:, None]))
        mats.append(seg)
    return np.concatenate(mats, axis=0).astype(np.float32), levels


def _split2(x):
    hi = x.astype(BF16)
    lo = (x - hi.astype(F32)).astype(BF16)
    return hi, lo


def _hgrn_prompt_kernel(q_ref, f_ref, i_ref, g_ref, lb_ref, gn_ref, mall_ref, o_ref, st_ref, s_scr, *, C, levels):
    cidx = pl.program_id(1)
    dk = s_scr.shape[-1]

    @pl.when(cidx == 0)
    def _():
        s_scr[...] = jnp.zeros_like(s_scr)

    lb = lb_ref[...]
    q = _silu(q_ref[0])
    fg = lb + (1.0 - lb) * _sigmoid(f_ref[0])
    kk = 1.0 - fg
    logf = jnp.log(fg)
    v = i_ref[0]
    v_bf = v.astype(BF16)

    mall = mall_ref[...]
    hi, lo = _split2(logf)
    sums = jnp.dot(mall, hi, preferred_element_type=F32) + jnp.dot(mall, lo, preferred_element_type=F32)
    b = sums[0:C]
    suf = sums[C:2 * C]
    qe = (q * jnp.exp(b)).astype(BF16)
    kt = (kk * jnp.exp(suf)).astype(BF16)
    dec_last = jnp.exp(b[C - 1:C, :])

    ti = lax.broadcasted_iota(jnp.int32, (C, C), 0)
    si = lax.broadcasted_iota(jnp.int32, (C, C), 1)
    txs = jnp.bitwise_xor(ti, si)
    lower = ti > si
    scores = [jnp.zeros((C, C), F32) for _ in range(HG_HEADS)]
    for lv in range(levels):
        e = jnp.exp(sums[(2 + lv) * C:(3 + lv) * C])
        a_bf = (q * e).astype(BF16)
        b_bf = (kk * e).astype(BF16)
        valid = jnp.logical_and(jnp.right_shift(txs, lv) == 1, lower)
        for h in range(HG_HEADS):
            hs = slice(h * dk, (h + 1) * dk)
            sl = lax.dot_general(a_bf[:, hs], b_bf[:, hs], _NT, preferred_element_type=F32)
            scores[h] = scores[h] + jnp.where(valid, sl, 0.0)

    qk = q * kk
    g = g_ref[0]
    gn = gn_ref[...]
    for h in range(HG_HEADS):
        hs = slice(h * dk, (h + 1) * dk)
        st = s_scr[h]
        diag = jnp.sum(qk[:, hs], axis=-1, keepdims=True)
        o = (jnp.dot(scores[h].astype(BF16), v_bf[:, hs], preferred_element_type=F32)
             + diag * v[:, hs]
             + lax.dot_general(qe[:, hs], st.astype(BF16), _NT, preferred_element_type=F32))
        s_scr[h] = st * dec_last[:, hs] + lax.dot_general(v_bf[:, hs], kt[:, hs], _TN, preferred_element_type=F32)
        o_ref[0, :, hs] = (_rms(o, gn[:, hs]) * _silu(g[:, hs])).astype(o_ref.dtype)

    @pl.when(cidx == pl.num_programs(1) - 1)
    def _():
        for h in range(HG_HEADS):
            st_ref[0, 0, h] = s_scr[h].T


def hgrn_prompt(q, f, i, g, lb, gnorm, *, C):
    bsz, s, d = q.shape
    dk = d // HG_HEADS
    mall_np, levels = _hgrn_sum_matrices(C)
    mall = jnp.asarray(mall_np, dtype=BF16)
    blk = pl.BlockSpec((1, C, d), lambda b, c: (b, c, 0))
    vec = pl.BlockSpec((1, d), lambda b, c: (0, 0))
    return pl.pallas_call(
        functools.partial(_hgrn_prompt_kernel, C=C, levels=levels),
        grid=(bsz, s // C),
        in_specs=[blk, blk, blk, blk, vec, vec,
                  pl.BlockSpec(mall.shape, lambda b, c: (0, 0))],
        out_specs=[pl.BlockSpec((1, C, d), lambda b, c: (b, c, 0)),
                   pl.BlockSpec((1, 1, HG_HEADS, dk, dk), lambda b, c: (0, b, 0, 0, 0))],
        out_shape=[jax.ShapeDtypeStruct((bsz, s, d), BF16),
                   jax.ShapeDtypeStruct((1, bsz, HG_HEADS, dk, dk), F32)],
        scratch_shapes=[pltpu.VMEM((HG_HEADS, dk, dk), F32)],
        compiler_params=_cparams(("parallel", "arbitrary")),
        name="hgrn_prompt",
    )(q, f, i, g, lb, gnorm.reshape(1, d), mall)


def _hgrn_step_kernel(q_ref, f_ref, i_ref, g_ref, lb_ref, gn_ref, s_ref, o_ref, so_ref):
    rows = q_ref.shape[0]
    lb = lb_ref[...]
    fg = lb + (1.0 - lb) * _sigmoid(f_ref[...])
    q_t = _silu(q_ref[...]).T
    fg_t = fg.T
    kk_t = (1.0 - fg).T
    v = i_ref[...]
    outs = []
    for r in range(rows):
        bl, h = divmod(r, HG_HEADS)
        s_new = fg_t[:, r:r + 1] * s_ref[0, bl, h] + kk_t[:, r:r + 1] * v[r:r + 1, :]
        so_ref[0, bl, h] = s_new
        outs.append(jnp.sum(q_t[:, r:r + 1] * s_new, axis=0, keepdims=True))
    o = jnp.concatenate(outs, axis=0)
    o_ref[...] = (_rms(o, gn_ref[...]) * _silu(g_ref[...])).astype(o_ref.dtype)


def hgrn_step(q, f, i, g, lb, gnorm, state, *, bb):
    bsz, d = q.shape
    dk = d // HG_HEADS
    rows = bb * HG_HEADS
    to_rows = lambda a: a.reshape(bsz * HG_HEADS, dk)
    tile = lambda p: jnp.tile(p.reshape(HG_HEADS, dk), (bb, 1))
    rblk = pl.BlockSpec((rows, dk), lambda b: (b, 0))
    pblk = pl.BlockSpec((rows, dk), lambda b: (0, 0))
    sblk = pl.BlockSpec((1, bb, HG_HEADS, dk, dk), lambda b: (0, b, 0, 0, 0))
    o, s_new = pl.pallas_call(
        _hgrn_step_kernel,
        grid=(bsz // bb,),
        in_specs=[rblk, rblk, rblk, rblk, pblk, pblk, sblk],
        out_specs=[rblk, sblk],
        out_shape=[jax.ShapeDtypeStruct((bsz * HG_HEADS, dk), BF16),
                   jax.ShapeDtypeStruct(state.shape, F32)],
        compiler_params=_cparams(("parallel",)),
        name="hgrn_step",
    )(to_rows(q), to_rows(f), to_rows(i), to_rows(g), tile(lb), tile(gnorm), state)
    return o.reshape(bsz, d), s_new


def _lambda(lq1_ref, lk1_ref, lq2_ref, lk2_ref, lam_init):
    s1 = jnp.sum(lq1_ref[...] * lk1_ref[...], axis=-1, keepdims=True)
    s2 = jnp.sum(lq2_ref[...] * lk2_ref[...], axis=-1, keepdims=True)
    return jnp.exp(s1) - jnp.exp(s2) + lam_init


def _attn_prompt_kernel(qi_ref, kj_ref, q_ref, k_ref, v_ref, tiles_ref,
                        lq1_ref, lk1_ref, lq2_ref, lk2_ref, sub_ref, o_ref,
                        m_scr, l_scr, acc_scr, *, lam_init):
    step = pl.program_id(1)
    i = qi_ref[step]
    j = kj_ref[step]
    T = q_ref.shape[1]
    bt = tiles_ref.shape[-1]
    dv = acc_scr.shape[-1]
    dh = dv // 2
    assert T == 2 * bt

    @pl.when(j == 0)
    def _():
        m_scr[...] = jnp.full_like(m_scr, NEG_BIG)
        l_scr[...] = jnp.zeros_like(l_scr)
        acc_scr[...] = jnp.zeros_like(acc_scr)

    def diag_bias(h, s):
        t0 = tiles_ref[0, h]
        top = jnp.concatenate([s[:bt, :bt] + t0, jnp.full((bt, bt), NEG_BIG, F32)], axis=1)
        bot = jnp.concatenate([s[bt:, :bt] + tiles_ref[1, h], s[bt:, bt:] + t0], axis=1)
        return jnp.concatenate([top, bot], axis=0)

    def prev_bias(h, s):
        top = jnp.concatenate([s[:bt, :bt], s[:bt, bt:] + tiles_ref[1, h]], axis=1)
        return jnp.concatenate([top, s[bt:]], axis=0)

    def update(bias_fn):
        q = q_ref[0]
        k = k_ref[0]
        v = v_ref[0]
        for h in range(DA_HEADS):
            vh = v[:, h * dv:(h + 1) * dv]
            for c in range(2):
                idx = 2 * h + c
                lo = h * dv + c * dh
                s = lax.dot_general(q[:, lo:lo + dh], k[:, lo:lo + dh], _NT, preferred_element_type=F32)
                if bias_fn is not None:
                    s = bias_fn(h, s)
                m_prev = m_scr[idx]
                m_new = jnp.maximum(m_prev, jnp.max(s, axis=-1, keepdims=True))
                alpha = jnp.exp2(m_prev - m_new)
                chunks = [jnp.exp2(s[:, t * LANES:(t + 1) * LANES] - m_new) for t in range(T // LANES)]
                psum = chunks[0]
                for ch in chunks[1:]:
                    psum = psum + ch
                p = jnp.concatenate([ch.astype(BF16) for ch in chunks], axis=1)
                l_scr[idx] = alpha * l_scr[idx] + psum
                acc_scr[idx] = (_lane_tile(alpha, dv // LANES) * acc_scr[idx]
                                + jnp.dot(p, vh, preferred_element_type=F32))
                m_scr[idx] = m_new

    @pl.when(j == i)
    def _():
        update(diag_bias)

    @pl.when(j == i - 1)
    def _():
        update(prev_bias)

    @pl.when(j < i - 1)
    def _():
        update(None)

    @pl.when(j == i)
    def _():
        lam = _lambda(lq1_ref, lk1_ref, lq2_ref, lk2_ref, lam_init)
        sub = sub_ref[...]
        for h in range(DA_HEADS):
            inv1 = 1.0 / jnp.sum(l_scr[2 * h], axis=-1, keepdims=True)
            inv2 = 1.0 / jnp.sum(l_scr[2 * h + 1], axis=-1, keepdims=True)
            o = acc_scr[2 * h] * inv1 - lam * (acc_scr[2 * h + 1] * inv2)
            o_ref[0, :, h * dv:(h + 1) * dv] = (_rms(o, sub) * (1.0 - lam_init)).astype(o_ref.dtype)


def attn_prompt(q, k, v, tiles, lams, subln, *, lam_init):
    bsz, s, d = q.shape
    T = 2 * tiles.shape[-1]
    nq = s // T
    dv = d // DA_HEADS
    qi = jnp.asarray([i for i in range(nq) for j in range(i + 1)], jnp.int32)
    kj = jnp.asarray([j for i in range(nq) for j in range(i + 1)], jnp.int32)
    dl = lams[0].shape[-1]
    lam_spec = pl.BlockSpec((1, dl), lambda b, st, qi, kj: (0, 0))
    grid_spec = pltpu.PrefetchScalarGridSpec(
        num_scalar_prefetch=2,
        grid=(bsz, int(qi.shape[0])),
        in_specs=[pl.BlockSpec((1, T, d), lambda b, st, qi, kj: (b, qi[st], 0)),
                  pl.BlockSpec((1, T, d), lambda b, st, qi, kj: (b, kj[st], 0)),
                  pl.BlockSpec((1, T, d), lambda b, st, qi, kj: (b, kj[st], 0)),
                  pl.BlockSpec(tiles.shape, lambda b, st, qi, kj: (0, 0, 0, 0)),
                  lam_spec, lam_spec, lam_spec, lam_spec,
                  pl.BlockSpec((1, dv), lambda b, st, qi, kj: (0, 0))],
        out_specs=pl.BlockSpec((1, T, d), lambda b, st, qi, kj: (b, qi[st], 0)),
        scratch_shapes=[pltpu.VMEM((2 * DA_HEADS, T, LANES), F32),
                        pltpu.VMEM((2 * DA_HEADS, T, LANES), F32),
                        pltpu.VMEM((2 * DA_HEADS, T, dv), F32)])
    return pl.pallas_call(
        functools.partial(_attn_prompt_kernel, lam_init=lam_init),
        grid_spec=grid_spec,
        out_shape=jax.ShapeDtypeStruct((bsz, s, d), BF16),
        compiler_params=_cparams(("parallel", "arbitrary")),
        name="attn_prompt",
    )(qi, kj, q, k, v, tiles, *lams, subln.reshape(1, dv))


def _attn_decode_kernel(pt_ref, q_ref, kn_ref, vn_ref, *refs, n_pages, lam_init):
    k_refs = refs[:n_pages]
    v_refs = refs[n_pages:2 * n_pages]
    dbias_ref, lq1_ref, lk1_ref, lq2_ref, lk2_ref, sub_ref, o_ref = refs[2 * n_pages:]
    d = q_ref.shape[-1]
    dv = d // DA_HEADS
    dh = dv // 2
    page = k_refs[0].shape[1]
    rows = page * DA_HEADS

    def rows2d(ref):
        return ref[0].reshape(rows, dv).astype(BF16)

    def head_rows(x):
        return jnp.concatenate([x[:, h * dv:(h + 1) * dv] for h in range(DA_HEADS)]
                               + [jnp.zeros((8 - DA_HEADS, dv), x.dtype)], axis=0)

    def own_head(n, n_valid):
        r = lax.broadcasted_iota(jnp.int32, (n, LANES), 0)
        j = lax.broadcasted_iota(jnp.int32, (n, LANES), 1)
        return jnp.logical_and(r % DA_HEADS == j // 2, jnp.logical_and(j < 2 * DA_HEADS, r < n_valid))

    q = q_ref[0].astype(F32)
    jrow = lax.broadcasted_iota(jnp.int32, (LANES, dv), 0)
    lane = lax.broadcasted_iota(jnp.int32, (LANES, dv), 1)
    qmat = jnp.zeros((LANES, dv), F32)
    for h in range(DA_HEADS):
        qh = jnp.broadcast_to(q[:, h * dv:(h + 1) * dv], (LANES, dv))
        qmat = jnp.where(jnp.logical_and(jrow // 2 == h, lane // dh == jrow % 2), qh, qmat)
    qmat = qmat.astype(BF16)

    valid = own_head(rows, rows)
    s_pages = []
    for r in range(n_pages):
        s = lax.dot_general(rows2d(k_refs[r]), qmat, _NT, preferred_element_type=F32)
        if r == n_pages - 1:
            s = s + dbias_ref[0:rows, :]
        s_pages.append(jnp.where(valid, s, NEG_BIG))
    kn8 = head_rows(kn_ref[0]).astype(BF16)
    s_new = lax.dot_general(kn8, qmat, _NT, preferred_element_type=F32) + dbias_ref[rows:rows + 8, :]
    s_new = jnp.where(own_head(8, DA_HEADS), s_new, NEG_BIG)

    smax = s_pages[0]
    for s in s_pages[1:]:
        smax = jnp.maximum(smax, s)
    m = jnp.maximum(jnp.max(smax, axis=0, keepdims=True), jnp.max(s_new, axis=0, keepdims=True))
    p_new = jnp.exp2(s_new - m)
    acc = lax.dot_general(p_new.astype(BF16), head_rows(vn_ref[0]).astype(BF16), _TN, preferred_element_type=F32)
    psum = jnp.zeros((rows, LANES), F32)
    for r in range(n_pages):
        p = jnp.exp2(s_pages[r] - m)
        psum = psum + p
        acc = acc + lax.dot_general(p.astype(BF16), rows2d(v_refs[r]), _TN, preferred_element_type=F32)
    lsum = jnp.sum(p_new, axis=0, keepdims=True) + jnp.sum(psum, axis=0, keepdims=True)
    eye = (lax.broadcasted_iota(jnp.int32, (LANES, LANES), 0) == lax.broadcasted_iota(jnp.int32, (LANES, LANES), 1))
    inv_col = jnp.sum(jnp.where(eye, jnp.broadcast_to(1.0 / lsum, (LANES, LANES)), 0.0), axis=1, keepdims=True)
    an = acc * inv_col
    lam = _lambda(lq1_ref, lk1_ref, lq2_ref, lk2_ref, lam_init)
    sub = sub_ref[...]
    for h in range(DA_HEADS):
        o = an[2 * h:2 * h + 1, :] - lam * an[2 * h + 1:2 * h + 2, :]
        o_ref[0, :, h * dv:(h + 1) * dv] = (_rms(o, sub) * (1.0 - lam_init)).astype(o_ref.dtype)


def attn_decode(q, k_new, v_new, cache_k, cache_v, page_table, dbias, lams, subln, *, lam_init):
    bsz, d = q.shape
    n_pages = page_table.shape[1]
    page = cache_k.shape[1]
    assert page == LANES
    dv = d // DA_HEADS
    dl = lams[0].shape[-1]
    row_spec = pl.BlockSpec((1, 1, d), lambda b, pt: (b, 0, 0))
    page_specs = [pl.BlockSpec((1, page, DA_HEADS, dv), functools.partial(lambda b, pt, r: (pt[b, r], 0, 0, 0), r=r))
                  for r in range(n_pages)]
    lam_spec = pl.BlockSpec((1, dl), lambda b, pt: (0, 0))
    grid_spec = pltpu.PrefetchScalarGridSpec(
        num_scalar_prefetch=1,
        grid=(bsz,),
        in_specs=[row_spec, row_spec, row_spec] + page_specs + page_specs
                 + [pl.BlockSpec(dbias.shape, lambda b, pt: (0, 0)),
                    lam_spec, lam_spec, lam_spec, lam_spec,
                    pl.BlockSpec((1, dv), lambda b, pt: (0, 0))],
        out_specs=pl.BlockSpec((1, 1, d), lambda b, pt: (b, 0, 0)))
    r3 = lambda a: a.reshape(bsz, 1, d)
    out = pl.pallas_call(
        functools.partial(_attn_decode_kernel, n_pages=n_pages, lam_init=lam_init),
        grid_spec=grid_spec,
        out_shape=jax.ShapeDtypeStruct((bsz, 1, d), BF16),
        compiler_params=_cparams(("arbitrary",)),
        name="attn_decode",
    )(page_table, r3(q), r3(k_new), r3(v_new), *([cache_k] * n_pages), *([cache_v] * n_pages),
      dbias, *lams, subln.reshape(1, dv))
    return out.reshape(bsz, d)


HGRN_C = 64
TM = 512


def _trunk(x, hg_state, kv_past, page_table, tiles, dbias, lb, p, *, batch_shape):
    m, d = x.shape
    prompt = hg_state is None
    dv = d // DA_HEADS
    scale = (dv // 2) ** -0.5
    lam_init = 0.8 - 0.6 * math.exp(-0.3 * 1)
    lams = [p[n][0].reshape(1, -1) for n in ("lambda_q1", "lambda_k1", "lambda_q2", "lambda_k2")]

    q, f, i, g = proj(x, [(p["norm_mix"][0], p["w_in_a"][0], 1.0, [[(F32, 0)]] * 4)], tm=TM, name="in_proj")
    if prompt:
        bsz, s = batch_shape
        r3 = lambda a: a.reshape(bsz, s, d)
        o, state = hgrn_prompt(r3(q), r3(f), r3(i), r3(g), lb, p["gnorm_a"][0], C=HGRN_C)
        o = o.reshape(m, d)
    else:
        o, state = hgrn_step(q, f, i, g, lb, p["gnorm_a"][0], hg_state, bb=16)
    x = mix_ffn(o, x, p["w_out_a"][0], p["norm_ffn"][0], p["w_gate_up"][0], p["w_down"][0], p["norm_final"],
                tm=TM, final_norm=False, name="mix_ffn0")

    k32, k16, v32, v16, qa = proj(
        x, [(p["kv_norm"], p["w_kv"], 1.0, [[(F32, DA_HEADS), (BF16, 0)]] * 2),
            (p["norm_mix"][1], p["w_q_b"][0], scale * LOG2E, [[(BF16, 0)]])],
        tm=TM, name="kvq_proj")

    if prompt:
        o = attn_prompt(r3(qa), r3(k16), r3(v16), tiles, lams, p["subln_b"][0], lam_init=lam_init)
        o = o.reshape(m, d)
    else:
        o = attn_decode(qa, k32.reshape(m, d), v32.reshape(m, d), kv_past[0], kv_past[1], page_table, dbias,
                        lams, p["subln_b"][0], lam_init=lam_init)
    y = mix_ffn(o, x, p["w_out_b"][0], p["norm_ffn"][1], p["w_gate_up"][1], p["w_down"][1], p["norm_final"],
                tm=TM, final_norm=True, name="mix_ffn1")
    return y, k32, v32, state


def kernel(x_prompt, x_sample, cache_k, cache_v, state_hgrn, page_table, w_in_a, lower_bound, gnorm_a, w_out_a,
           w_q_b, lambda_q1, lambda_k1, lambda_q2, lambda_k2, subln_b, w_out_b, kv_norm, w_kv, rel_bias,
           norm_mix, norm_ffn, w_gate_up, w_down, norm_final):
    bf = lambda w: [w[l].astype(BF16) for l in range(w.shape[0])] if w.ndim == 3 else w.astype(BF16)
    p = dict(w_in_a=bf(w_in_a), gnorm_a=gnorm_a, w_out_a=bf(w_out_a), w_q_b=bf(w_q_b), lambda_q1=lambda_q1,
             lambda_k1=lambda_k1, lambda_q2=lambda_q2, lambda_k2=lambda_k2, subln_b=subln_b, w_out_b=bf(w_out_b),
             kv_norm=kv_norm, w_kv=bf(w_kv), norm_mix=norm_mix, norm_ffn=norm_ffn, w_gate_up=bf(w_gate_up),
             w_down=bf(w_down), norm_final=norm_final)
    bsz, s, d = x_prompt.shape
    nb = x_sample.shape[0]
    hk = DA_HEADS, d // DA_HEADS
    tiles, dbias, lb = param_tables(rel_bias, lower_bound, cache_k.shape[1])

    y_p, k_p, v_p, st_p = _trunk(x_prompt.reshape(bsz * s, d), None, None, None, tiles, dbias, lb, p,
                                 batch_shape=(bsz, s))
    y_s, k_s, v_s, st_s = _trunk(x_sample.reshape(nb, d), state_hgrn, (cache_k, cache_v), page_table,
                                 tiles, dbias, lb, p, batch_shape=(nb, 1))
    return (y_p.reshape(bsz, s, d), y_s.reshape(nb, 1, d),
            k_p.reshape(bsz, s, *hk), v_p.reshape(bsz, s, *hk), st_p,
            k_s.reshape(nb, 1, *hk), v_s.reshape(nb, 1, *hk), st_s)
```

```python
import functools
import math

import numpy as np
import jax
import jax.numpy as jnp
from jax import lax
from jax.experimental import pallas as pl
from jax.experimental.pallas import tpu as pltpu

F32 = jnp.float32
BF16 = jnp.bfloat16
EPS = 1e-6
NEG_BIG = -1e30
LOG2E = math.log2(math.e)

HG_HEADS = 8
DA_HEADS = 4
N_BUCKETS = 32
MAX_DISTANCE = 128
LANES = 128
MXU_N = 256
VMEM_LIMIT = 56 * 1024 * 1024

_NT = (((1,), (1,)), ((), ()))
_TN = (((0,), (0,)), ((), ()))


def _cparams(sem):
    return pltpu.CompilerParams(dimension_semantics=sem, vmem_limit_bytes=VMEM_LIMIT)


def _resident(shape):
    return pl.BlockSpec(shape, lambda *_: (0,) * len(shape), pipeline_mode=pl.Buffered(1))


def _resident_layer(w, layer):
    return pl.BlockSpec((1,) + w.shape[1:], lambda *_: (layer, 0, 0), pipeline_mode=pl.Buffered(1))


def _sigmoid(x):
    return 1.0 / (1.0 + jnp.exp(-x))


def _silu(x):
    return x * _sigmoid(x)


def _rms(x, g):
    ms = jnp.mean(x * x, axis=-1, keepdims=True)
    return x * lax.rsqrt(ms + EPS) * g


def _lane_tile(x, reps):
    return x if reps == 1 else jnp.concatenate([x] * reps, axis=1)


def _proj_kernel(x_ref, *refs, branches):
    nb = len(branches)
    g_refs = refs[:nb]
    w_refs = refs[nb:2 * nb]
    out_refs = refs[2 * nb:]
    x = x_ref[...]
    inv = lax.rsqrt(jnp.mean(x * x, axis=-1, keepdims=True) + EPS)
    o = 0
    for bi, (scale, groups) in enumerate(branches):
        xn = (x * inv * g_refs[bi][...]).astype(BF16)
        ng = w_refs[bi].shape[2] // len(groups)
        for gi, outs in enumerate(groups):
            acc = jnp.dot(xn, w_refs[bi][0, :, gi * ng:(gi + 1) * ng], preferred_element_type=F32)
            if scale != 1.0:
                acc = acc * scale
            for _ in outs:
                ref = out_refs[o]
                if len(ref.shape) == 3:
                    hd = ref.shape[2]
                    for hh in range(ref.shape[1]):
                        ref[:, hh, :] = acc[:, hh * hd:(hh + 1) * hd].astype(ref.dtype)
                else:
                    ref[...] = acc.astype(ref.dtype)
                o += 1


def proj(x, branches, *, tm, name):
    m, d = x.shape
    tm = min(tm, m)
    gains = [b[0].reshape(1, d) for b in branches]
    weights = [b[1] for b in branches]
    out_shape, out_specs = [], []
    for _, (w, _), _, groups in branches:
        ng = w.shape[2] // len(groups)
        for outs in groups:
            for dt, heads in outs:
                if heads:
                    out_shape.append(jax.ShapeDtypeStruct((m, heads, ng // heads), dt))
                    out_specs.append(pl.BlockSpec((tm, heads, ng // heads), lambda i: (i, 0, 0)))
                else:
                    out_shape.append(jax.ShapeDtypeStruct((m, ng), dt))
                    out_specs.append(pl.BlockSpec((tm, ng), lambda i: (i, 0)))
    return pl.pallas_call(
        functools.partial(_proj_kernel, branches=[(b[2], b[3]) for b in branches]),
        grid=(m // tm,),
        in_specs=([pl.BlockSpec((tm, d), lambda i: (i, 0))]
                  + [_resident(g.shape) for g in gains] + [_resident_layer(w, l) for w, l in weights]),
        out_specs=out_specs,
        out_shape=out_shape,
        compiler_params=_cparams(("parallel",)),
        name=name,
    )(x, *gains, *[w for w, _ in weights])


def _mix_ffn_kernel(a_ref, x_ref, wo_ref, g_ref, wgu_ref, wd_ref, gf_ref, o_ref, x1_scr, hid_scr, *, final_norm):
    dff = wd_ref.shape[1]
    x1 = x_ref[...] + jnp.dot(a_ref[...], wo_ref[0], preferred_element_type=F32)
    x1_scr[...] = x1
    xn = _rms(x1, g_ref[...]).astype(BF16)
    for j in range(dff // MXU_N):
        cs = slice(j * MXU_N, (j + 1) * MXU_N)
        gt = jnp.dot(xn, wgu_ref[0, :, cs], preferred_element_type=F32)
        ut = jnp.dot(xn, wgu_ref[0, :, dff + j * MXU_N:dff + (j + 1) * MXU_N], preferred_element_type=F32)
        hid_scr[:, cs] = (_silu(gt) * ut).astype(BF16)
    y = x1_scr[...] + jnp.dot(hid_scr[...], wd_ref[0], preferred_element_type=F32)
    if final_norm:
        y = _rms(y, gf_ref[...])
    o_ref[...] = y


def mix_ffn(a, x, w_out, g, w_gate_up, w_down, g_final, *, layer, tm, final_norm, name):
    m, d = x.shape
    dff = w_down.shape[1]
    assert dff % MXU_N == 0
    tm = min(tm, m)
    row = lambda width: pl.BlockSpec((tm, width), lambda i: (i, 0))
    return pl.pallas_call(
        functools.partial(_mix_ffn_kernel, final_norm=final_norm),
        grid=(m // tm,),
        in_specs=[row(a.shape[1]), row(d), _resident_layer(w_out, 0), _resident((1, d)),
                  _resident_layer(w_gate_up, layer), _resident_layer(w_down, layer), _resident((1, d))],
        out_specs=row(d),
        out_shape=jax.ShapeDtypeStruct((m, d), F32),
        scratch_shapes=[pltpu.VMEM((tm, d), F32), pltpu.VMEM((tm, dff), BF16)],
        compiler_params=_cparams(("parallel",)),
        name=name,
    )(a, x, w_out, g.reshape(1, d), w_gate_up, w_down, g_final.reshape(1, d))


BIAS_T = 2 * MAX_DISTANCE


def _bucket_bias(n, rb_ref, n_heads):
    max_exact = N_BUCKETS // 2
    nf = jnp.maximum(n, 1).astype(F32)
    large = max_exact + (jnp.log(nf / max_exact) / math.log(MAX_DISTANCE / max_exact)
                         * (N_BUCKETS - max_exact)).astype(jnp.int32)
    large = jnp.minimum(large, N_BUCKETS - 1)
    bucket = jnp.where(n < max_exact, n, large)
    outs = []
    for h in range(n_heads):
        far = rb_ref[N_BUCKETS - 1, h]
        acc = jnp.zeros(n.shape, F32)
        for b in range(N_BUCKETS - 1):
            acc = jnp.where(bucket == b, (rb_ref[b, h] - far) * LOG2E, acc)
        outs.append(acc)
    return outs


def _tables_kernel(rb_ref, lb_in_ref, tiles_ref, dec_ref, lb_ref):
    bt = tiles_ref.shape[-1]
    r = lax.broadcasted_iota(jnp.int32, (bt, bt), 0)
    c = lax.broadcasted_iota(jnp.int32, (bt, bt), 1)
    n0 = r - c
    diag = _bucket_bias(jnp.maximum(n0, 0), rb_ref, DA_HEADS)
    prev = _bucket_bias(bt + n0, rb_ref, DA_HEADS)
    for h in range(DA_HEADS):
        tiles_ref[0, h] = jnp.where(n0 >= 0, diag[h], NEG_BIG)
        tiles_ref[1, h] = prev[h]
    page_rows = dec_ref.shape[0] - 8
    rr = lax.broadcasted_iota(jnp.int32, dec_ref.shape, 0)
    ll = lax.broadcasted_iota(jnp.int32, dec_ref.shape, 1)
    nd = jnp.where(rr < page_rows, page_rows // DA_HEADS - rr // DA_HEADS, 0)
    dec = _bucket_bias(nd, rb_ref, DA_HEADS)
    acc = jnp.zeros(dec_ref.shape, F32)
    for h in range(DA_HEADS):
        acc = jnp.where(ll // 2 == h, dec[h], acc)
    dec_ref[...] = acc
    lbi = lb_in_ref[...]
    mx = jnp.max(lbi, axis=0, keepdims=True)
    e = jnp.exp(lbi - mx)
    lb_ref[...] = e[0:1, :] / jnp.sum(e, axis=0, keepdims=True)


def param_tables(rel_bias, lower_bound, page):
    d = lower_bound.shape[1]
    return pl.pallas_call(
        _tables_kernel,
        in_specs=[pl.BlockSpec(memory_space=pltpu.SMEM),
                  pl.BlockSpec(memory_space=pltpu.VMEM)],
        out_specs=[pl.BlockSpec(memory_space=pltpu.VMEM)] * 3,
        out_shape=[jax.ShapeDtypeStruct((2, DA_HEADS, BIAS_T, BIAS_T), F32),
                   jax.ShapeDtypeStruct((page * DA_HEADS + 8, LANES), F32),
                   jax.ShapeDtypeStruct((1, d), F32)],
        compiler_params=pltpu.CompilerParams(vmem_limit_bytes=VMEM_LIMIT),
        name="param_tables",
    )(rel_bias, lower_bound)


def _hgrn_sum_matrices(C):
    levels = int(math.log2(C))
    t = np.arange(C)
    u = t[None, :]
    mats = [u <= t[:, None], u > t[:, None]]
    for lv in range(levels):
        c = 1 << lv
        e = (t // (2 * c)) * (2 * c) + c - 1
        upper = (t > e)[:, None]
        seg = np.where(upper, (u > e[:, None]) & (u <= t[:, None]), (u > t[:, None]) & (u <= e[:, None]))
        mats.append(seg)
    return np.concatenate(mats, axis=0).astype(np.float32), levels


def _split2(x):
    hi = x.astype(BF16)
    lo = (x - hi.astype(F32)).astype(BF16)
    return hi, lo


def _hgrn_prompt_kernel(q_ref, f_ref, i_ref, g_ref, lb_ref, gn_ref, mall_ref, o_ref, st_ref, s_scr, *, C, levels):
    cidx = pl.program_id(1)
    dk = s_scr.shape[-1]

    @pl.when(cidx == 0)
    def _():
        s_scr[...] = jnp.zeros_like(s_scr)

    lb = lb_ref[...]
    q = _silu(q_ref[0])
    fg = lb + (1.0 - lb) * _sigmoid(f_ref[0])
    kk = 1.0 - fg
    logf = jnp.log(fg)
    v = i_ref[0]
    v_bf = v.astype(BF16)

    mall = mall_ref[...]
    hi, lo = _split2(logf)
    sums = jnp.dot(mall, hi, preferred_element_type=F32)
    ends = sums[0:2 * C] + jnp.dot(mall[0:2 * C], lo, preferred_element_type=F32)
    b = ends[0:C]
    suf = ends[C:2 * C]
    qe = (q * jnp.exp(b)).astype(BF16)
    kt = (kk * jnp.exp(suf)).astype(BF16)
    dec_last = jnp.exp(b[C - 1:C, :])

    ti = lax.broadcasted_iota(jnp.int32, (C, C), 0)
    si = lax.broadcasted_iota(jnp.int32, (C, C), 1)
    txs = jnp.bitwise_xor(ti, si)
    lower = ti > si
    scores = [jnp.zeros((C, C), F32) for _ in range(HG_HEADS)]
    for lv in range(levels):
        e = jnp.exp(sums[(2 + lv) * C:(3 + lv) * C])
        a_bf = (q * e).astype(BF16)
        b_bf = (kk * e).astype(BF16)
        valid = jnp.logical_and(jnp.right_shift(txs, lv) == 1, lower)
        for h in range(HG_HEADS):
            hs = slice(h * dk, (h + 1) * dk)
            sl = lax.dot_general(a_bf[:, hs], b_bf[:, hs], _NT, preferred_element_type=F32)
            scores[h] = scores[h] + jnp.where(valid, sl, 0.0)

    qk = q * kk
    g = g_ref[0]
    gn = gn_ref[...]
    for h in range(HG_HEADS):
        hs = slice(h * dk, (h + 1) * dk)
        st = s_scr[h]
        diag = jnp.sum(qk[:, hs], axis=-1, keepdims=True)
        o = (jnp.dot(scores[h].astype(BF16), v_bf[:, hs], preferred_element_type=F32)
             + diag * v[:, hs]
             + lax.dot_general(qe[:, hs], st.astype(BF16), _NT, preferred_element_type=F32))
        s_scr[h] = st * dec_last[:, hs] + lax.dot_general(v_bf[:, hs], kt[:, hs], _TN, preferred_element_type=F32)
        o_ref[0, :, hs] = (_rms(o, gn[:, hs]) * _silu(g[:, hs])).astype(o_ref.dtype)

    @pl.when(cidx == pl.num_programs(1) - 1)
    def _():
        for h in range(HG_HEADS):
            st_ref[0, 0, h] = s_scr[h].T


def hgrn_prompt(q, f, i, g, lb, gnorm, *, C):
    bsz, s, d = q.shape
    dk = d // HG_HEADS
    mall_np, levels = _hgrn_sum_matrices(C)
    mall = jnp.asarray(mall_np, dtype=BF16)
    blk = pl.BlockSpec((1, C, d), lambda b, c: (b, c, 0))
    vec = pl.BlockSpec((1, d), lambda b, c: (0, 0))
    return pl.pallas_call(
        functools.partial(_hgrn_prompt_kernel, C=C, levels=levels),
        grid=(bsz, s // C),
        in_specs=[blk, blk, blk, blk, vec, vec,
                  pl.BlockSpec(mall.shape, lambda b, c: (0, 0))],
        out_specs=[pl.BlockSpec((1, C, d), lambda b, c: (b, c, 0)),
                   pl.BlockSpec((1, 1, HG_HEADS, dk, dk), lambda b, c: (0, b, 0, 0, 0))],
        out_shape=[jax.ShapeDtypeStruct((bsz, s, d), BF16),
                   jax.ShapeDtypeStruct((1, bsz, HG_HEADS, dk, dk), F32)],
        scratch_shapes=[pltpu.VMEM((HG_HEADS, dk, dk), F32)],
        compiler_params=_cparams(("parallel", "arbitrary")),
        name="hgrn_prompt",
    )(q, f, i, g, lb, gnorm.reshape(1, d), mall)


def _hgrn_step_kernel(q_ref, f_ref, i_ref, g_ref, lb_ref, gn_ref, s_ref, o_ref, so_ref):
    rows = q_ref.shape[0]
    lb = lb_ref[...]
    fg = lb + (1.0 - lb) * _sigmoid(f_ref[...])
    q_t = _silu(q_ref[...]).T
    fg_t = fg.T
    kk_t = (1.0 - fg).T
    v = i_ref[...]
    outs = []
    for r in range(rows):
        bl, h = divmod(r, HG_HEADS)
        s_new = fg_t[:, r:r + 1] * s_ref[0, bl, h] + kk_t[:, r:r + 1] * v[r:r + 1, :]
        so_ref[0, bl, h] = s_new
        outs.append(jnp.sum(q_t[:, r:r + 1] * s_new, axis=0, keepdims=True))
    o = jnp.concatenate(outs, axis=0)
    o_ref[...] = (_rms(o, gn_ref[...]) * _silu(g_ref[...])).astype(o_ref.dtype)


def hgrn_step(q, f, i, g, lb, gnorm, state, *, bb):
    bsz, d = q.shape
    dk = d // HG_HEADS
    rows = bb * HG_HEADS
    to_rows = lambda a: a.reshape(bsz * HG_HEADS, dk)
    tile = lambda p: jnp.tile(p.reshape(HG_HEADS, dk), (bb, 1))
    rblk = pl.BlockSpec((rows, dk), lambda b: (b, 0))
    pblk = pl.BlockSpec((rows, dk), lambda b: (0, 0))
    sblk = pl.BlockSpec((1, bb, HG_HEADS, dk, dk), lambda b: (0, b, 0, 0, 0))
    o, s_new = pl.pallas_call(
        _hgrn_step_kernel,
        grid=(bsz // bb,),
        in_specs=[rblk, rblk, rblk, rblk, pblk, pblk, sblk],
        out_specs=[rblk, sblk],
        out_shape=[jax.ShapeDtypeStruct((bsz * HG_HEADS, dk), BF16),
                   jax.ShapeDtypeStruct(state.shape, F32)],
        compiler_params=_cparams(("parallel",)),
        name="hgrn_step",
    )(to_rows(q), to_rows(f), to_rows(i), to_rows(g), tile(lb), tile(gnorm), state)
    return o.reshape(bsz, d), s_new


def _lambda(lq1_ref, lk1_ref, lq2_ref, lk2_ref, lam_init):
    s1 = jnp.sum(lq1_ref[...] * lk1_ref[...], axis=-1, keepdims=True)
    s2 = jnp.sum(lq2_ref[...] * lk2_ref[...], axis=-1, keepdims=True)
    return jnp.exp(s1) - jnp.exp(s2) + lam_init


def _attn_prompt_kernel(qi_ref, kj_ref, q_ref, k_ref, vp_ref, vd_ref, tiles_ref,
                        lq1_ref, lk1_ref, lq2_ref, lk2_ref, sub_ref, o_ref,
                        m_scr, l_scr, acc_scr, p_scr, a_scr, *, lam_init):
    step = pl.program_id(1)
    i = qi_ref[step]
    j = kj_ref[step]
    T = q_ref.shape[1]
    bt = tiles_ref.shape[-1]
    dv = acc_scr.shape[-1]
    dh = dv // 2
    assert T == 2 * bt

    @pl.when(j == 0)
    def _():
        m_scr[...] = jnp.full_like(m_scr, NEG_BIG)
        l_scr[...] = jnp.zeros_like(l_scr)
        acc_scr[...] = jnp.zeros_like(acc_scr)
        p_scr[...] = jnp.zeros_like(p_scr)
        a_scr[...] = jnp.zeros_like(a_scr)

    def diag_bias(h, s):
        t0 = tiles_ref[0, h]
        top = jnp.concatenate([s[:bt, :bt] + t0, jnp.full((bt, bt), NEG_BIG, F32)], axis=1)
        bot = jnp.concatenate([s[bt:, :bt] + tiles_ref[1, h], s[bt:, bt:] + t0], axis=1)
        return jnp.concatenate([top, bot], axis=0)

    def prev_bias(h, s):
        top = jnp.concatenate([s[:bt, :bt], s[:bt, bt:] + tiles_ref[1, h]], axis=1)
        return jnp.concatenate([top, s[bt:]], axis=0)

    def product(idx, vh):
        acc_scr[idx] = (_lane_tile(a_scr[idx], dv // LANES) * acc_scr[idx]
                        + jnp.dot(p_scr[idx], vh, preferred_element_type=F32))

    def logits(idx, qh, kh, bias):
        s = lax.dot_general(qh, kh, _NT, preferred_element_type=F32)
        if bias is not None:
            s = bias(idx // 2, s)
        m_prev = m_scr[idx]
        m_new = jnp.maximum(m_prev, jnp.max(s, axis=-1, keepdims=True))
        alpha = jnp.exp2(m_prev - m_new)
        chunks = [jnp.exp2(s[:, t * LANES:(t + 1) * LANES] - m_new) for t in range(T // LANES)]
        psum = chunks[0]
        for ch in chunks[1:]:
            psum = psum + ch
        p_scr[idx] = jnp.concatenate([ch.astype(BF16) for ch in chunks], axis=1)
        a_scr[idx] = alpha
        l_scr[idx] = alpha * l_scr[idx] + psum
        m_scr[idx] = m_new

    def sweep(bias, flush):
        q = q_ref[0]
        k = k_ref[0]
        vp = vp_ref[0]
        for h in range(DA_HEADS):
            for c in range(2):
                idx = 2 * h + c
                lo = h * dv + c * dh
                product(idx, vp[:, h * dv:(h + 1) * dv])
                logits(idx, q[:, lo:lo + dh], k[:, lo:lo + dh], bias)
        if flush:
            vd = vd_ref[0]
            for idx in range(2 * DA_HEADS):
                product(idx, vd[:, (idx // 2) * dv:(idx // 2 + 1) * dv])

    @pl.when(j == i - 1)
    def _():
        sweep(prev_bias, False)

    @pl.when(j < i - 1)
    def _():
        sweep(None, False)

    @pl.when(j == i)
    def _():
        sweep(diag_bias, True)
        lam = _lambda(lq1_ref, lk1_ref, lq2_ref, lk2_ref, lam_init)
        sub = sub_ref[...]
        for h in range(DA_HEADS):
            inv1 = 1.0 / jnp.sum(l_scr[2 * h], axis=-1, keepdims=True)
            inv2 = 1.0 / jnp.sum(l_scr[2 * h + 1], axis=-1, keepdims=True)
            o = acc_scr[2 * h] * inv1 - lam * (acc_scr[2 * h + 1] * inv2)
            o_ref[0, :, h * dv:(h + 1) * dv] = (_rms(o, sub) * (1.0 - lam_init)).astype(o_ref.dtype)


def attn_prompt(q, k, v, tiles, lams, subln, *, lam_init):
    bsz, s, d = q.shape
    T = 2 * tiles.shape[-1]
    nq = s // T
    dv = d // DA_HEADS
    qi = jnp.asarray([i for i in range(nq) for j in range(i + 1)], jnp.int32)
    kj = jnp.asarray([j for i in range(nq) for j in range(i + 1)], jnp.int32)
    dl = lams[0].shape[-1]
    lam_spec = pl.BlockSpec((1, dl), lambda b, st, qi, kj: (0, 0))
    grid_spec = pltpu.PrefetchScalarGridSpec(
        num_scalar_prefetch=2,
        grid=(bsz, int(qi.shape[0])),
        in_specs=[pl.BlockSpec((1, T, d), lambda b, st, qi, kj: (b, qi[st], 0)),
                  pl.BlockSpec((1, T, d), lambda b, st, qi, kj: (b, kj[st], 0)),
                  pl.BlockSpec((1, T, d), lambda b, st, qi, kj: (b, jnp.maximum(kj[st] - 1, 0), 0)),
                  pl.BlockSpec((1, T, d), lambda b, st, qi, kj: (b, qi[st], 0)),
                  pl.BlockSpec(tiles.shape, lambda b, st, qi, kj: (0, 0, 0, 0)),
                  lam_spec, lam_spec, lam_spec, lam_spec,
                  pl.BlockSpec((1, dv), lambda b, st, qi, kj: (0, 0))],
        out_specs=pl.BlockSpec((1, T, d), lambda b, st, qi, kj: (b, qi[st], 0)),
        scratch_shapes=[pltpu.VMEM((2 * DA_HEADS, T, LANES), F32),
                        pltpu.VMEM((2 * DA_HEADS, T, LANES), F32),
                        pltpu.VMEM((2 * DA_HEADS, T, dv), F32),
                        pltpu.VMEM((2 * DA_HEADS, T, T), BF16),
                        pltpu.VMEM((2 * DA_HEADS, T, LANES), F32)])
    return pl.pallas_call(
        functools.partial(_attn_prompt_kernel, lam_init=lam_init),
        grid_spec=grid_spec,
        out_shape=jax.ShapeDtypeStruct((bsz, s, d), BF16),
        compiler_params=_cparams(("parallel", "arbitrary")),
        name="attn_prompt",
    )(qi, kj, q, k, v, v, tiles, *lams, subln.reshape(1, dv))


def _attn_decode_kernel(pt_ref, q_ref, kn_ref, vn_ref, *refs, n_pages, lam_init):
    k_refs = refs[:n_pages]
    v_refs = refs[n_pages:2 * n_pages]
    dbias_ref, lq1_ref, lk1_ref, lq2_ref, lk2_ref, sub_ref, o_ref = refs[2 * n_pages:]
    d = q_ref.shape[-1]
    dv = d // DA_HEADS
    dh = dv // 2
    page = k_refs[0].shape[1]
    rows = page * DA_HEADS

    def rows2d(ref):
        return ref[0].reshape(rows, dv).astype(BF16)

    def head_rows(x):
        return jnp.concatenate([x[:, h * dv:(h + 1) * dv] for h in range(DA_HEADS)]
                               + [jnp.zeros((8 - DA_HEADS, dv), x.dtype)], axis=0)

    def own_head(n, n_valid):
        r = lax.broadcasted_iota(jnp.int32, (n, LANES), 0)
        j = lax.broadcasted_iota(jnp.int32, (n, LANES), 1)
        return jnp.logical_and(r % DA_HEADS == j // 2, jnp.logical_and(j < 2 * DA_HEADS, r < n_valid))

    q = q_ref[0].astype(F32)
    jrow = lax.broadcasted_iota(jnp.int32, (LANES, dv), 0)
    lane = lax.broadcasted_iota(jnp.int32, (LANES, dv), 1)
    qmat = jnp.zeros((LANES, dv), F32)
    for h in range(DA_HEADS):
        qh = jnp.broadcast_to(q[:, h * dv:(h + 1) * dv], (LANES, dv))
        qmat = jnp.where(jnp.logical_and(jrow // 2 == h, lane // dh == jrow % 2), qh, qmat)
    qmat = qmat.astype(BF16)

    valid = own_head(rows, rows)
    s_pages = []
    for r in range(n_pages):
        s = lax.dot_general(rows2d(k_refs[r]), qmat, _NT, preferred_element_type=F32)
        if r == n_pages - 1:
            s = s + dbias_ref[0:rows, :]
        s_pages.append(jnp.where(valid, s, NEG_BIG))
    kn8 = head_rows(kn_ref[0]).astype(BF16)
    s_new = lax.dot_general(kn8, qmat, _NT, preferred_element_type=F32) + dbias_ref[rows:rows + 8, :]
    s_new = jnp.where(own_head(8, DA_HEADS), s_new, NEG_BIG)

    smax = s_pages[0]
    for s in s_pages[1:]:
        smax = jnp.maximum(smax, s)
    m = jnp.maximum(jnp.max(smax, axis=0, keepdims=True), jnp.max(s_new, axis=0, keepdims=True))
    p_new = jnp.exp2(s_new - m)
    acc = lax.dot_general(p_new.astype(BF16), head_rows(vn_ref[0]).astype(BF16), _TN, preferred_element_type=F32)
    psum = jnp.zeros((rows, LANES), F32)
    for r in range(n_pages):
        p = jnp.exp2(s_pages[r] - m)
        psum = psum + p
        acc = acc + lax.dot_general(p.astype(BF16), rows2d(v_refs[r]), _TN, preferred_element_type=F32)
    lsum = jnp.sum(p_new, axis=0, keepdims=True) + jnp.sum(psum, axis=0, keepdims=True)
    eye = (lax.broadcasted_iota(jnp.int32, (LANES, LANES), 0) == lax.broadcasted_iota(jnp.int32, (LANES, LANES), 1))
    inv_col = jnp.sum(jnp.where(eye, jnp.broadcast_to(1.0 / lsum, (LANES, LANES)), 0.0), axis=1, keepdims=True)
    an = acc * inv_col
    lam = _lambda(lq1_ref, lk1_ref, lq2_ref, lk2_ref, lam_init)
    sub = sub_ref[...]
    for h in range(DA_HEADS):
        o = an[2 * h:2 * h + 1, :] - lam * an[2 * h + 1:2 * h + 2, :]
        o_ref[0, :, h * dv:(h + 1) * dv] = (_rms(o, sub) * (1.0 - lam_init)).astype(o_ref.dtype)


def attn_decode(q, k_new, v_new, cache_k, cache_v, page_table, dbias, lams, subln, *, lam_init):
    bsz, d = q.shape
    n_pages = page_table.shape[1]
    page = cache_k.shape[1]
    assert page == LANES
    dv = d // DA_HEADS
    dl = lams[0].shape[-1]
    row_spec = pl.BlockSpec((1, 1, d), lambda b, pt: (b, 0, 0))
    page_specs = [pl.BlockSpec((1, page, DA_HEADS, dv), functools.partial(lambda b, pt, r: (pt[b, r], 0, 0, 0), r=r))
                  for r in range(n_pages)]
    lam_spec = pl.BlockSpec((1, dl), lambda b, pt: (0, 0))
    grid_spec = pltpu.PrefetchScalarGridSpec(
        num_scalar_prefetch=1,
        grid=(bsz,),
        in_specs=[row_spec, row_spec, row_spec] + page_specs + page_specs
                 + [pl.BlockSpec(dbias.shape, lambda b, pt: (0, 0)),
                    lam_spec, lam_spec, lam_spec, lam_spec,
                    pl.BlockSpec((1, dv), lambda b, pt: (0, 0))],
        out_specs=pl.BlockSpec((1, 1, d), lambda b, pt: (b, 0, 0)))
    r3 = lambda a: a.reshape(bsz, 1, d)
    out = pl.pallas_call(
        functools.partial(_attn_decode_kernel, n_pages=n_pages, lam_init=lam_init),
        grid_spec=grid_spec,
        out_shape=jax.ShapeDtypeStruct((bsz, 1, d), BF16),
        compiler_params=_cparams(("arbitrary",)),
        name="attn_decode",
    )(page_table, r3(q), r3(k_new), r3(v_new), *([cache_k] * n_pages), *([cache_v] * n_pages),
      dbias, *lams, subln.reshape(1, dv))
    return out.reshape(bsz, d)


HGRN_C = 64
TM = 512


def _trunk(x, hg_state, kv_past, page_table, tiles, dbias, lb, p, *, batch_shape):
    m, d = x.shape
    prompt = hg_state is None
    dv = d // DA_HEADS
    scale = (dv // 2) ** -0.5
    lam_init = 0.8 - 0.6 * math.exp(-0.3 * 1)
    lams = [p[n][0].reshape(1, -1) for n in ("lambda_q1", "lambda_k1", "lambda_q2", "lambda_k2")]

    q, f, i, g = proj(x, [(p["norm_mix"][0], (p["w_in_a"], 0), 1.0, [[(F32, 0)]] * 4)], tm=TM, name="in_proj")
    if prompt:
        bsz, s = batch_shape
        r3 = lambda a: a.reshape(bsz, s, d)
        o, state = hgrn_prompt(r3(q), r3(f), r3(i), r3(g), lb, p["gnorm_a"][0], C=HGRN_C)
        o = o.reshape(m, d)
    else:
        o, state = hgrn_step(q, f, i, g, lb, p["gnorm_a"][0], hg_state, bb=16)
    x = mix_ffn(o, x, p["w_out_a"], p["norm_ffn"][0], p["w_gate_up"], p["w_down"], p["norm_final"],
                layer=0, tm=TM, final_norm=False, name="mix_ffn0")

    k32, k16, v32, v16, qa = proj(
        x, [(p["kv_norm"], (p["w_kv"], 0), 1.0, [[(F32, DA_HEADS), (BF16, 0)]] * 2),
            (p["norm_mix"][1], (p["w_q_b"], 0), scale * LOG2E, [[(BF16, 0)]])],
        tm=TM, name="kvq_proj")

    if prompt:
        o = attn_prompt(r3(qa), r3(k16), r3(v16), tiles, lams, p["subln_b"][0], lam_init=lam_init)
        o = o.reshape(m, d)
    else:
        o = attn_decode(qa, k32.reshape(m, d), v32.reshape(m, d), kv_past[0], kv_past[1], page_table, dbias,
                        lams, p["subln_b"][0], lam_init=lam_init)
    y = mix_ffn(o, x, p["w_out_b"], p["norm_ffn"][1], p["w_gate_up"], p["w_down"], p["norm_final"],
                layer=1, tm=TM, final_norm=True, name="mix_ffn1")
    return y, k32, v32, state


def kernel(x_prompt, x_sample, cache_k, cache_v, state_hgrn, page_table, w_in_a, lower_bound, gnorm_a, w_out_a,
           w_q_b, lambda_q1, lambda_k1, lambda_q2, lambda_k2, subln_b, w_out_b, kv_norm, w_kv, rel_bias,
           norm_mix, norm_ffn, w_gate_up, w_down, norm_final):
    bf = lambda w: (w if w.ndim == 3 else w[None]).astype(BF16)
    p = dict(w_in_a=bf(w_in_a), gnorm_a=gnorm_a, w_out_a=bf(w_out_a), w_q_b=bf(w_q_b), lambda_q1=lambda_q1,
             lambda_k1=lambda_k1, lambda_q2=lambda_q2, lambda_k2=lambda_k2, subln_b=subln_b, w_out_b=bf(w_out_b),
             kv_norm=kv_norm, w_kv=bf(w_kv), norm_mix=norm_mix, norm_ffn=norm_ffn, w_gate_up=bf(w_gate_up),
             w_down=bf(w_down), norm_final=norm_final)
    bsz, s, d = x_prompt.shape
    nb = x_sample.shape[0]
    hk = DA_HEADS, d // DA_HEADS
    tiles, dbias, lb = param_tables(rel_bias, lower_bound, cache_k.shape[1])

    y_p, k_p, v_p, st_p = _trunk(x_prompt.reshape(bsz * s, d), None, None, None, tiles, dbias, lb, p,
                                 batch_shape=(bsz, s))
    y_s, k_s, v_s, st_s = _trunk(x_sample.reshape(nb, d), state_hgrn, (cache_k, cache_v), page_table,
                                 tiles, dbias, lb, p, batch_shape=(nb, 1))
    return (y_p.reshape(bsz, s, d), y_s.reshape(nb, 1, d),
            k_p.reshape(bsz, s, *hk), v_p.reshape(bsz, s, *hk), st_p,
            k_s.reshape(nb, 1, *hk), v_s.reshape(nb, 1, *hk), st_s)
```

```python
import functools
import math

import numpy as np
import jax
import jax.numpy as jnp
from jax import lax
from jax.experimental import pallas as pl
from jax.experimental.pallas import tpu as pltpu

F32 = jnp.float32
BF16 = jnp.bfloat16
EPS = 1e-6
NEG_BIG = -1e30
LOG2E = math.log2(math.e)

HG_HEADS = 8
DA_HEADS = 4
N_BUCKETS = 32
MAX_DISTANCE = 128
LANES = 128
MXU_N = 256
VMEM_LIMIT = 56 * 1024 * 1024

_NT = (((1,), (1,)), ((), ()))
_TN = (((0,), (0,)), ((), ()))


def _cparams(sem):
    return pltpu.CompilerParams(dimension_semantics=sem, vmem_limit_bytes=VMEM_LIMIT)


def _resident(shape):
    return pl.BlockSpec(shape, lambda *_: (0,) * len(shape), pipeline_mode=pl.Buffered(1))


def _resident_layer(w, layer):
    return pl.BlockSpec((1,) + w.shape[1:], lambda *_: (layer, 0, 0), pipeline_mode=pl.Buffered(1))


def _sigmoid(x):
    return 1.0 / (1.0 + jnp.exp(-x))


def _silu(x):
    return x * _sigmoid(x)


def _rms(x, g):
    ms = jnp.mean(x * x, axis=-1, keepdims=True)
    return x * lax.rsqrt(ms + EPS) * g


def _lane_tile(x, reps):
    return x if reps == 1 else jnp.concatenate([x] * reps, axis=1)


def _proj_kernel(x_ref, *refs, branches):
    nb = len(branches)
    g_refs = refs[:nb]
    w_refs = refs[nb:2 * nb]
    out_refs = refs[2 * nb:]
    x = x_ref[...]
    inv = lax.rsqrt(jnp.mean(x * x, axis=-1, keepdims=True) + EPS)
    o = 0
    for bi, (scale, groups) in enumerate(branches):
        xn = (x * inv * g_refs[bi][...]).astype(BF16)
        ng = w_refs[bi].shape[2] // len(groups)
        for gi, outs in enumerate(groups):
            acc = jnp.dot(xn, w_refs[bi][0, :, gi * ng:(gi + 1) * ng], preferred_element_type=F32)
            if scale != 1.0:
                acc = acc * scale
            for _ in outs:
                ref = out_refs[o]
                if len(ref.shape) == 3:
                    hd = ref.shape[2]
                    for hh in range(ref.shape[1]):
                        ref[:, hh, :] = acc[:, hh * hd:(hh + 1) * hd].astype(ref.dtype)
                else:
                    ref[...] = acc.astype(ref.dtype)
                o += 1


def proj(x, branches, *, tm, name):
    m, d = x.shape
    tm = min(tm, m)
    gains = [b[0].reshape(1, d) for b in branches]
    weights = [b[1] for b in branches]
    out_shape, out_specs = [], []
    for _, (w, _), _, groups in branches:
        ng = w.shape[2] // len(groups)
        for outs in groups:
            for dt, heads in outs:
                if heads:
                    out_shape.append(jax.ShapeDtypeStruct((m, heads, ng // heads), dt))
                    out_specs.append(pl.BlockSpec((tm, heads, ng // heads), lambda i: (i, 0, 0)))
                else:
                    out_shape.append(jax.ShapeDtypeStruct((m, ng), dt))
                    out_specs.append(pl.BlockSpec((tm, ng), lambda i: (i, 0)))
    return pl.pallas_call(
        functools.partial(_proj_kernel, branches=[(b[2], b[3]) for b in branches]),
        grid=(m // tm,),
        in_specs=([pl.BlockSpec((tm, d), lambda i: (i, 0))]
                  + [_resident(g.shape) for g in gains] + [_resident_layer(w, l) for w, l in weights]),
        out_specs=out_specs,
        out_shape=out_shape,
        compiler_params=_cparams(("parallel",)),
        name=name,
    )(x, *gains, *[w for w, _ in weights])


def _mix_ffn_kernel(a_ref, x_ref, wo_ref, g_ref, wgu_ref, wd_ref, gf_ref, o_ref, x1_scr, hid_scr, *, final_norm):
    dff = wd_ref.shape[1]
    x1 = x_ref[...] + jnp.dot(a_ref[...], wo_ref[0], preferred_element_type=F32)
    x1_scr[...] = x1
    xn = _rms(x1, g_ref[...]).astype(BF16)
    for j in range(dff // MXU_N):
        cs = slice(j * MXU_N, (j + 1) * MXU_N)
        gt = jnp.dot(xn, wgu_ref[0, :, cs], preferred_element_type=F32)
        ut = jnp.dot(xn, wgu_ref[0, :, dff + j * MXU_N:dff + (j + 1) * MXU_N], preferred_element_type=F32)
        hid_scr[:, cs] = (_silu(gt) * ut).astype(BF16)
    y = x1_scr[...] + jnp.dot(hid_scr[...], wd_ref[0], preferred_element_type=F32)
    if final_norm:
        y = _rms(y, gf_ref[...])
    o_ref[...] = y


def mix_ffn(a, x, w_out, g, w_gate_up, w_down, g_final, *, layer, tm, final_norm, name):
    m, d = x.shape
    dff = w_down.shape[1]
    assert dff % MXU_N == 0
    tm = min(tm, m)
    row = lambda width: pl.BlockSpec((tm, width), lambda i: (i, 0))
    return pl.pallas_call(
        functools.partial(_mix_ffn_kernel, final_norm=final_norm),
        grid=(m // tm,),
        in_specs=[row(a.shape[1]), row(d), _resident_layer(w_out, 0), _resident((1, d)),
                  _resident_layer(w_gate_up, layer), _resident_layer(w_down, layer), _resident((1, d))],
        out_specs=row(d),
        out_shape=jax.ShapeDtypeStruct((m, d), F32),
        scratch_shapes=[pltpu.VMEM((tm, d), F32), pltpu.VMEM((tm, dff), BF16)],
        compiler_params=_cparams(("parallel",)),
        name=name,
    )(a, x, w_out, g.reshape(1, d), w_gate_up, w_down, g_final.reshape(1, d))


BIAS_T = 2 * MAX_DISTANCE


def _bucket_bias(n, rb_ref, n_heads):
    max_exact = N_BUCKETS // 2
    nf = jnp.maximum(n, 1).astype(F32)
    large = max_exact + (jnp.log(nf / max_exact) / math.log(MAX_DISTANCE / max_exact)
                         * (N_BUCKETS - max_exact)).astype(jnp.int32)
    large = jnp.minimum(large, N_BUCKETS - 1)
    bucket = jnp.where(n < max_exact, n, large)
    outs = []
    for h in range(n_heads):
        far = rb_ref[N_BUCKETS - 1, h]
        acc = jnp.zeros(n.shape, F32)
        for b in range(N_BUCKETS - 1):
            acc = jnp.where(bucket == b, (rb_ref[b, h] - far) * LOG2E, acc)
        outs.append(acc)
    return outs


def _tables_kernel(rb_ref, lb_in_ref, tiles_ref, dec_ref, lb_ref):
    bt = tiles_ref.shape[-1]
    r = lax.broadcasted_iota(jnp.int32, (bt, bt), 0)
    c = lax.broadcasted_iota(jnp.int32, (bt, bt), 1)
    n0 = r - c
    diag = _bucket_bias(jnp.maximum(n0, 0), rb_ref, DA_HEADS)
    prev = _bucket_bias(bt + n0, rb_ref, DA_HEADS)
    for h in range(DA_HEADS):
        tiles_ref[0, h] = jnp.where(n0 >= 0, diag[h], NEG_BIG)
        tiles_ref[1, h] = prev[h]
    page_rows = dec_ref.shape[0] - 8
    rr = lax.broadcasted_iota(jnp.int32, dec_ref.shape, 0)
    ll = lax.broadcasted_iota(jnp.int32, dec_ref.shape, 1)
    nd = jnp.where(rr < page_rows, page_rows // DA_HEADS - rr // DA_HEADS, 0)
    dec = _bucket_bias(nd, rb_ref, DA_HEADS)
    acc = jnp.zeros(dec_ref.shape, F32)
    for h in range(DA_HEADS):
        acc = jnp.where(ll // 2 == h, dec[h], acc)
    dec_ref[...] = acc
    lbi = lb_in_ref[...]
    mx = jnp.max(lbi, axis=0, keepdims=True)
    e = jnp.exp(lbi - mx)
    lb_ref[...] = e[0:1, :] / jnp.sum(e, axis=0, keepdims=True)


def param_tables(rel_bias, lower_bound, page):
    d = lower_bound.shape[1]
    return pl.pallas_call(
        _tables_kernel,
        in_specs=[pl.BlockSpec(memory_space=pltpu.SMEM),
                  pl.BlockSpec(memory_space=pltpu.VMEM)],
        out_specs=[pl.BlockSpec(memory_space=pltpu.VMEM)] * 3,
        out_shape=[jax.ShapeDtypeStruct((2, DA_HEADS, BIAS_T, BIAS_T), F32),
                   jax.ShapeDtypeStruct((page * DA_HEADS + 8, LANES), F32),
                   jax.ShapeDtypeStruct((1, d), F32)],
        compiler_params=pltpu.CompilerParams(vmem_limit_bytes=VMEM_LIMIT),
        name="param_tables",
    )(rel_bias, lower_bound)


def _hgrn_sum_matrices(C):
    levels = int(math.log2(C))
    t = np.arange(C)
    u = t[None, :]
    mats = [u <= t[:, None], u > t[:, None]]
    for lv in range(levels):
        c = 1 << lv
        e = (t // (2 * c)) * (2 * c) + c - 1
        upper = (t > e)[:, None]
        seg = np.where(upper, (u > e[:, None]) & (u <= t[:, None]), (u > t[:, None]) & (u <= e[:, None]))
        mats.append(seg)
    return np.concatenate(mats, axis=0).astype(np.float32), levels


def _split2(x):
    hi = x.astype(BF16)
    lo = (x - hi.astype(F32)).astype(BF16)
    return hi, lo


def _hgrn_prompt_kernel(q_ref, f_ref, i_ref, g_ref, lb_ref, gn_ref, mall_ref, o_ref, st_ref, s_scr,
                        *, C, levels, nsub):
    cidx = pl.program_id(1)
    dk = s_scr.shape[-1]

    @pl.when(cidx == 0)
    def _():
        s_scr[...] = jnp.zeros_like(s_scr)

    lb = lb_ref[...]
    gn = gn_ref[...]
    mall = mall_ref[...]
    ti = lax.broadcasted_iota(jnp.int32, (C, C), 0)
    si = lax.broadcasted_iota(jnp.int32, (C, C), 1)
    txs = jnp.bitwise_xor(ti, si)
    lower = ti > si
    valid = [jnp.logical_and(jnp.right_shift(txs, lv) == 1, lower) for lv in range(levels)]

    for sub in range(nsub):
        rows = slice(sub * C, (sub + 1) * C)
        q = _silu(q_ref[0, rows, :])
        fg = lb + (1.0 - lb) * _sigmoid(f_ref[0, rows, :])
        kk = 1.0 - fg
        logf = jnp.log(fg)
        v = i_ref[0, rows, :]
        v_bf = v.astype(BF16)

        hi, lo = _split2(logf)
        sums = jnp.dot(mall, hi, preferred_element_type=F32)
        ends = sums[0:2 * C] + jnp.dot(mall[0:2 * C], lo, preferred_element_type=F32)
        b = ends[0:C]
        suf = ends[C:2 * C]
        qe = (q * jnp.exp(b)).astype(BF16)
        kt = (kk * jnp.exp(suf)).astype(BF16)
        dec_last = jnp.exp(b[C - 1:C, :])

        scores = [jnp.zeros((C, C), F32) for _ in range(HG_HEADS)]
        for lv in range(levels):
            e = jnp.exp(sums[(2 + lv) * C:(3 + lv) * C])
            a_bf = (q * e).astype(BF16)
            b_bf = (kk * e).astype(BF16)
            for h in range(HG_HEADS):
                hs = slice(h * dk, (h + 1) * dk)
                sl = lax.dot_general(a_bf[:, hs], b_bf[:, hs], _NT, preferred_element_type=F32)
                scores[h] = scores[h] + jnp.where(valid[lv], sl, 0.0)

        qk = q * kk
        g = g_ref[0, rows, :]
        for h in range(HG_HEADS):
            hs = slice(h * dk, (h + 1) * dk)
            st = s_scr[h]
            diag = jnp.sum(qk[:, hs], axis=-1, keepdims=True)
            o = (jnp.dot(scores[h].astype(BF16), v_bf[:, hs], preferred_element_type=F32)
                 + diag * v[:, hs]
                 + lax.dot_general(qe[:, hs], st.astype(BF16), _NT, preferred_element_type=F32))
            s_scr[h] = st * dec_last[:, hs] + lax.dot_general(v_bf[:, hs], kt[:, hs], _TN,
                                                              preferred_element_type=F32)
            o_ref[0, rows, hs] = (_rms(o, gn[:, hs]) * _silu(g[:, hs])).astype(o_ref.dtype)

    @pl.when(cidx == pl.num_programs(1) - 1)
    def _():
        for h in range(HG_HEADS):
            st_ref[0, 0, h] = s_scr[h].T


def hgrn_prompt(q, f, i, g, lb, gnorm, *, C, nsub):
    bsz, s, d = q.shape
    dk = d // HG_HEADS
    mall_np, levels = _hgrn_sum_matrices(C)
    mall = jnp.asarray(mall_np, dtype=BF16)
    tb = C * nsub
    blk = pl.BlockSpec((1, tb, d), lambda b, c: (b, c, 0))
    vec = pl.BlockSpec((1, d), lambda b, c: (0, 0))
    return pl.pallas_call(
        functools.partial(_hgrn_prompt_kernel, C=C, levels=levels, nsub=nsub),
        grid=(bsz, s // tb),
        in_specs=[blk, blk, blk, blk, vec, vec,
                  pl.BlockSpec(mall.shape, lambda b, c: (0, 0))],
        out_specs=[pl.BlockSpec((1, tb, d), lambda b, c: (b, c, 0)),
                   pl.BlockSpec((1, 1, HG_HEADS, dk, dk), lambda b, c: (0, b, 0, 0, 0))],
        out_shape=[jax.ShapeDtypeStruct((bsz, s, d), BF16),
                   jax.ShapeDtypeStruct((1, bsz, HG_HEADS, dk, dk), F32)],
        scratch_shapes=[pltpu.VMEM((HG_HEADS, dk, dk), F32)],
        compiler_params=_cparams(("parallel", "arbitrary")),
        name="hgrn_prompt",
    )(q, f, i, g, lb, gnorm.reshape(1, d), mall)


def _hgrn_step_kernel(q_ref, f_ref, i_ref, g_ref, lb_ref, gn_ref, s_ref, o_ref, so_ref):
    rows = q_ref.shape[0]
    lb = lb_ref[...]
    fg = lb + (1.0 - lb) * _sigmoid(f_ref[...])
    q_t = _silu(q_ref[...]).T
    fg_t = fg.T
    kk_t = (1.0 - fg).T
    v = i_ref[...]
    outs = []
    for r in range(rows):
        bl, h = divmod(r, HG_HEADS)
        s_new = fg_t[:, r:r + 1] * s_ref[0, bl, h] + kk_t[:, r:r + 1] * v[r:r + 1, :]
        so_ref[0, bl, h] = s_new
        outs.append(jnp.sum(q_t[:, r:r + 1] * s_new, axis=0, keepdims=True))
    o = jnp.concatenate(outs, axis=0)
    o_ref[...] = (_rms(o, gn_ref[...]) * _silu(g_ref[...])).astype(o_ref.dtype)


def hgrn_step(q, f, i, g, lb, gnorm, state, *, bb):
    bsz, d = q.shape
    dk = d // HG_HEADS
    rows = bb * HG_HEADS
    to_rows = lambda a: a.reshape(bsz * HG_HEADS, dk)
    tile = lambda p: jnp.tile(p.reshape(HG_HEADS, dk), (bb, 1))
    rblk = pl.BlockSpec((rows, dk), lambda b: (b, 0))
    pblk = pl.BlockSpec((rows, dk), lambda b: (0, 0))
    sblk = pl.BlockSpec((1, bb, HG_HEADS, dk, dk), lambda b: (0, b, 0, 0, 0))
    o, s_new = pl.pallas_call(
        _hgrn_step_kernel,
        grid=(bsz // bb,),
        in_specs=[rblk, rblk, rblk, rblk, pblk, pblk, sblk],
        out_specs=[rblk, sblk],
        out_shape=[jax.ShapeDtypeStruct((bsz * HG_HEADS, dk), BF16),
                   jax.ShapeDtypeStruct(state.shape, F32)],
        compiler_params=_cparams(("parallel",)),
        name="hgrn_step",
    )(to_rows(q), to_rows(f), to_rows(i), to_rows(g), tile(lb), tile(gnorm), state)
    return o.reshape(bsz, d), s_new


def _lambda(lq1_ref, lk1_ref, lq2_ref, lk2_ref, lam_init):
    s1 = jnp.sum(lq1_ref[...] * lk1_ref[...], axis=-1, keepdims=True)
    s2 = jnp.sum(lq2_ref[...] * lk2_ref[...], axis=-1, keepdims=True)
    return jnp.exp(s1) - jnp.exp(s2) + lam_init


def _attn_prompt_kernel(qi_ref, kj_ref, q_ref, k_ref, vp_ref, vd_ref, tiles_ref,
                        lq1_ref, lk1_ref, lq2_ref, lk2_ref, sub_ref, o_ref,
                        m_scr, l_scr, acc_scr, p_scr, a_scr, *, lam_init):
    step = pl.program_id(1)
    i = qi_ref[step]
    j = kj_ref[step]
    T = q_ref.shape[1]
    bt = tiles_ref.shape[-1]
    dv = acc_scr.shape[-1]
    dh = dv // 2
    assert T == 2 * bt

    @pl.when(j == 0)
    def _():
        m_scr[...] = jnp.full_like(m_scr, NEG_BIG)
        l_scr[...] = jnp.zeros_like(l_scr)
        acc_scr[...] = jnp.zeros_like(acc_scr)
        p_scr[...] = jnp.zeros_like(p_scr)
        a_scr[...] = jnp.zeros_like(a_scr)

    def diag_bias(h, s):
        t0 = tiles_ref[0, h]
        top = jnp.concatenate([s[:bt, :bt] + t0, jnp.full((bt, bt), NEG_BIG, F32)], axis=1)
        bot = jnp.concatenate([s[bt:, :bt] + tiles_ref[1, h], s[bt:, bt:] + t0], axis=1)
        return jnp.concatenate([top, bot], axis=0)

    def prev_bias(h, s):
        top = jnp.concatenate([s[:bt, :bt], s[:bt, bt:] + tiles_ref[1, h]], axis=1)
        return jnp.concatenate([top, s[bt:]], axis=0)

    def product(idx, vh):
        acc_scr[idx] = (_lane_tile(a_scr[idx], dv // LANES) * acc_scr[idx]
                        + jnp.dot(p_scr[idx], vh, preferred_element_type=F32))

    def logits(idx, qh, kh, bias):
        s = lax.dot_general(qh, kh, _NT, preferred_element_type=F32)
        if bias is not None:
            s = bias(idx // 2, s)
        m_prev = m_scr[idx]
        m_new = jnp.maximum(m_prev, jnp.max(s, axis=-1, keepdims=True))
        alpha = jnp.exp2(m_prev - m_new)
        chunks = [jnp.exp2(s[:, t * LANES:(t + 1) * LANES] - m_new) for t in range(T // LANES)]
        psum = chunks[0]
        for ch in chunks[1:]:
            psum = psum + ch
        p_scr[idx] = jnp.concatenate([ch.astype(BF16) for ch in chunks], axis=1)
        a_scr[idx] = alpha
        l_scr[idx] = alpha * l_scr[idx] + psum
        m_scr[idx] = m_new

    def sweep(bias, flush):
        q = q_ref[0]
        k = k_ref[0]
        vp = vp_ref[0]
        for h in range(DA_HEADS):
            for c in range(2):
                idx = 2 * h + c
                lo = h * dv + c * dh
                product(idx, vp[:, h * dv:(h + 1) * dv])
                logits(idx, q[:, lo:lo + dh], k[:, lo:lo + dh], bias)
        if flush:
            vd = vd_ref[0]
            for idx in range(2 * DA_HEADS):
                product(idx, vd[:, (idx // 2) * dv:(idx // 2 + 1) * dv])

    @pl.when(j == i - 1)
    def _():
        sweep(prev_bias, False)

    @pl.when(j < i - 1)
    def _():
        sweep(None, False)

    @pl.when(j == i)
    def _():
        sweep(diag_bias, True)
        lam = _lambda(lq1_ref, lk1_ref, lq2_ref, lk2_ref, lam_init)
        sub = sub_ref[...]
        for h in range(DA_HEADS):
            inv1 = 1.0 / jnp.sum(l_scr[2 * h], axis=-1, keepdims=True)
            inv2 = 1.0 / jnp.sum(l_scr[2 * h + 1], axis=-1, keepdims=True)
            o = acc_scr[2 * h] * inv1 - lam * (acc_scr[2 * h + 1] * inv2)
            o_ref[0, :, h * dv:(h + 1) * dv] = (_rms(o, sub) * (1.0 - lam_init)).astype(o_ref.dtype)


def attn_prompt(q, k, v, tiles, lams, subln, *, lam_init):
    bsz, s, d = q.shape
    T = 2 * tiles.shape[-1]
    nq = s // T
    dv = d // DA_HEADS
    qi = jnp.asarray([i for i in range(nq) for j in range(i + 1)], jnp.int32)
    kj = jnp.asarray([j for i in range(nq) for j in range(i + 1)], jnp.int32)
    dl = lams[0].shape[-1]
    lam_spec = pl.BlockSpec((1, dl), lambda b, st, qi, kj: (0, 0))
    grid_spec = pltpu.PrefetchScalarGridSpec(
        num_scalar_prefetch=2,
        grid=(bsz, int(qi.shape[0])),
        in_specs=[pl.BlockSpec((1, T, d), lambda b, st, qi, kj: (b, qi[st], 0)),
                  pl.BlockSpec((1, T, d), lambda b, st, qi, kj: (b, kj[st], 0)),
                  pl.BlockSpec((1, T, d), lambda b, st, qi, kj: (b, jnp.maximum(kj[st] - 1, 0), 0)),
                  pl.BlockSpec((1, T, d), lambda b, st, qi, kj: (b, qi[st], 0)),
                  pl.BlockSpec(tiles.shape, lambda b, st, qi, kj: (0, 0, 0, 0)),
                  lam_spec, lam_spec, lam_spec, lam_spec,
                  pl.BlockSpec((1, dv), lambda b, st, qi, kj: (0, 0))],
        out_specs=pl.BlockSpec((1, T, d), lambda b, st, qi, kj: (b, qi[st], 0)),
        scratch_shapes=[pltpu.VMEM((2 * DA_HEADS, T, LANES), F32),
                        pltpu.VMEM((2 * DA_HEADS, T, LANES), F32),
                        pltpu.VMEM((2 * DA_HEADS, T, dv), F32),
                        pltpu.VMEM((2 * DA_HEADS, T, T), BF16),
                        pltpu.VMEM((2 * DA_HEADS, T, LANES), F32)])
    return pl.pallas_call(
        functools.partial(_attn_prompt_kernel, lam_init=lam_init),
        grid_spec=grid_spec,
        out_shape=jax.ShapeDtypeStruct((bsz, s, d), BF16),
        compiler_params=_cparams(("parallel", "arbitrary")),
        name="attn_prompt",
    )(qi, kj, q, k, v, v, tiles, *lams, subln.reshape(1, dv))


def _attn_decode_kernel(pt_ref, q_ref, kn_ref, vn_ref, *refs, n_pages, lam_init):
    k_refs = refs[:n_pages]
    v_refs = refs[n_pages:2 * n_pages]
    dbias_ref, lq1_ref, lk1_ref, lq2_ref, lk2_ref, sub_ref, o_ref = refs[2 * n_pages:]
    d = q_ref.shape[-1]
    dv = d // DA_HEADS
    dh = dv // 2
    page = k_refs[0].shape[1]
    rows = page * DA_HEADS

    def rows2d(ref):
        return ref[0].reshape(rows, dv).astype(BF16)

    def head_rows(x):
        return jnp.concatenate([x[:, h * dv:(h + 1) * dv] for h in range(DA_HEADS)]
                               + [jnp.zeros((8 - DA_HEADS, dv), x.dtype)], axis=0)

    def own_head(n, n_valid):
        r = lax.broadcasted_iota(jnp.int32, (n, LANES), 0)
        j = lax.broadcasted_iota(jnp.int32, (n, LANES), 1)
        return jnp.logical_and(r % DA_HEADS == j // 2, jnp.logical_and(j < 2 * DA_HEADS, r < n_valid))

    q = q_ref[0].astype(F32)
    jrow = lax.broadcasted_iota(jnp.int32, (LANES, dv), 0)
    lane = lax.broadcasted_iota(jnp.int32, (LANES, dv), 1)
    qmat = jnp.zeros((LANES, dv), F32)
    for h in range(DA_HEADS):
        qh = jnp.broadcast_to(q[:, h * dv:(h + 1) * dv], (LANES, dv))
        qmat = jnp.where(jnp.logical_and(jrow // 2 == h, lane // dh == jrow % 2), qh, qmat)
    qmat = qmat.astype(BF16)

    eye = (lax.broadcasted_iota(jnp.int32, (LANES, LANES), 0) == lax.broadcasted_iota(jnp.int32, (LANES, LANES), 1))

    def to_rows(x):
        return jnp.sum(jnp.where(eye, jnp.broadcast_to(x, (LANES, LANES)), 0.0), axis=1, keepdims=True)

    kn8 = head_rows(kn_ref[0]).astype(BF16)
    s_new = lax.dot_general(kn8, qmat, _NT, preferred_element_type=F32) + dbias_ref[rows:rows + 8, :]
    s_new = jnp.where(own_head(8, DA_HEADS), s_new, NEG_BIG)
    m = jnp.max(s_new, axis=0, keepdims=True)
    p_new = jnp.exp2(s_new - m)
    lsum = jnp.sum(p_new, axis=0, keepdims=True)
    acc = lax.dot_general(p_new.astype(BF16), head_rows(vn_ref[0]).astype(BF16), _TN, preferred_element_type=F32)

    valid = own_head(rows, rows)
    group = 4
    assert n_pages % group == 0
    for g0 in range(0, n_pages, group):
        s_grp = []
        for r in range(g0, g0 + group):
            s = lax.dot_general(rows2d(k_refs[r]), qmat, _NT, preferred_element_type=F32)
            if r == n_pages - 1:
                s = s + dbias_ref[0:rows, :]
            s_grp.append(jnp.where(valid, s, NEG_BIG))
        smax = s_grp[0]
        for s in s_grp[1:]:
            smax = jnp.maximum(smax, s)
        m_new = jnp.maximum(m, jnp.max(smax, axis=0, keepdims=True))
        alpha = jnp.exp2(m - m_new)
        psum = jnp.zeros((rows, LANES), F32)
        acc_g = jnp.zeros(acc.shape, F32)
        for r in range(g0, g0 + group):
            p = jnp.exp2(s_grp[r - g0] - m_new)
            psum = psum + p
            acc_g = acc_g + lax.dot_general(p.astype(BF16), rows2d(v_refs[r]), _TN, preferred_element_type=F32)
        lsum = alpha * lsum + jnp.sum(psum, axis=0, keepdims=True)
        acc = to_rows(alpha) * acc + acc_g
        m = m_new
    an = acc * to_rows(1.0 / lsum)
    lam = _lambda(lq1_ref, lk1_ref, lq2_ref, lk2_ref, lam_init)
    sub = sub_ref[...]
    for h in range(DA_HEADS):
        o = an[2 * h:2 * h + 1, :] - lam * an[2 * h + 1:2 * h + 2, :]
        o_ref[0, :, h * dv:(h + 1) * dv] = (_rms(o, sub) * (1.0 - lam_init)).astype(o_ref.dtype)


def attn_decode(q, k_new, v_new, cache_k, cache_v, page_table, dbias, lams, subln, *, lam_init):
    bsz, d = q.shape
    n_pages = page_table.shape[1]
    page = cache_k.shape[1]
    assert page == LANES
    dv = d // DA_HEADS
    dl = lams[0].shape[-1]
    row_spec = pl.BlockSpec((1, 1, d), lambda b, pt: (b, 0, 0))
    page_specs = [pl.BlockSpec((1, page, DA_HEADS, dv), functools.partial(lambda b, pt, r: (pt[b, r], 0, 0, 0), r=r))
                  for r in range(n_pages)]
    lam_spec = pl.BlockSpec((1, dl), lambda b, pt: (0, 0))
    grid_spec = pltpu.PrefetchScalarGridSpec(
        num_scalar_prefetch=1,
        grid=(bsz,),
        in_specs=[row_spec, row_spec, row_spec] + page_specs + page_specs
                 + [pl.BlockSpec(dbias.shape, lambda b, pt: (0, 0)),
                    lam_spec, lam_spec, lam_spec, lam_spec,
                    pl.BlockSpec((1, dv), lambda b, pt: (0, 0))],
        out_specs=pl.BlockSpec((1, 1, d), lambda b, pt: (b, 0, 0)))
    r3 = lambda a: a.reshape(bsz, 1, d)
    out = pl.pallas_call(
        functools.partial(_attn_decode_kernel, n_pages=n_pages, lam_init=lam_init),
        grid_spec=grid_spec,
        out_shape=jax.ShapeDtypeStruct((bsz, 1, d), BF16),
        compiler_params=_cparams(("arbitrary",)),
        name="attn_decode",
    )(page_table, r3(q), r3(k_new), r3(v_new), *([cache_k] * n_pages), *([cache_v] * n_pages),
      dbias, *lams, subln.reshape(1, dv))
    return out.reshape(bsz, d)


HGRN_C = 64
HGRN_NSUB = 8
TM = 512


def _trunk(x, hg_state, kv_past, page_table, tiles, dbias, lb, p, *, batch_shape):
    m, d = x.shape
    prompt = hg_state is None
    dv = d // DA_HEADS
    scale = (dv // 2) ** -0.5
    lam_init = 0.8 - 0.6 * math.exp(-0.3 * 1)
    lams = [p[n][0].reshape(1, -1) for n in ("lambda_q1", "lambda_k1", "lambda_q2", "lambda_k2")]

    q, f, i, g = proj(x, [(p["norm_mix"][0], (p["w_in_a"], 0), 1.0, [[(F32, 0)]] * 4)], tm=TM, name="in_proj")
    if prompt:
        bsz, s = batch_shape
        r3 = lambda a: a.reshape(bsz, s, d)
        o, state = hgrn_prompt(r3(q), r3(f), r3(i), r3(g), lb, p["gnorm_a"][0], C=HGRN_C, nsub=HGRN_NSUB)
        o = o.reshape(m, d)
    else:
        o, state = hgrn_step(q, f, i, g, lb, p["gnorm_a"][0], hg_state, bb=16)
    x = mix_ffn(o, x, p["w_out_a"], p["norm_ffn"][0], p["w_gate_up"], p["w_down"], p["norm_final"],
                layer=0, tm=TM, final_norm=False, name="mix_ffn0")

    k32, k16, v32, v16, qa = proj(
        x, [(p["kv_norm"], (p["w_kv"], 0), 1.0, [[(F32, DA_HEADS), (BF16, 0)]] * 2),
            (p["norm_mix"][1], (p["w_q_b"], 0), scale * LOG2E, [[(BF16, 0)]])],
        tm=TM, name="kvq_proj")

    if prompt:
        o = attn_prompt(r3(qa), r3(k16), r3(v16), tiles, lams, p["subln_b"][0], lam_init=lam_init)
        o = o.reshape(m, d)
    else:
        o = attn_decode(qa, k32.reshape(m, d), v32.reshape(m, d), kv_past[0], kv_past[1], page_table, dbias,
                        lams, p["subln_b"][0], lam_init=lam_init)
    y = mix_ffn(o, x, p["w_out_b"], p["norm_ffn"][1], p["w_gate_up"], p["w_down"], p["norm_final"],
                layer=1, tm=TM, final_norm=True, name="mix_ffn1")
    return y, k32, v32, state


def kernel(x_prompt, x_sample, cache_k, cache_v, state_hgrn, page_table, w_in_a, lower_bound, gnorm_a, w_out_a,
           w_q_b, lambda_q1, lambda_k1, lambda_q2, lambda_k2, subln_b, w_out_b, kv_norm, w_kv, rel_bias,
           norm_mix, norm_ffn, w_gate_up, w_down, norm_final):
    bf = lambda w: (w if w.ndim == 3 else w[None]).astype(BF16)
    p = dict(w_in_a=bf(w_in_a), gnorm_a=gnorm_a, w_out_a=bf(w_out_a), w_q_b=bf(w_q_b), lambda_q1=lambda_q1,
             lambda_k1=lambda_k1, lambda_q2=lambda_q2, lambda_k2=lambda_k2, subln_b=subln_b, w_out_b=bf(w_out_b),
             kv_norm=kv_norm, w_kv=bf(w_kv), norm_mix=norm_mix, norm_ffn=norm_ffn, w_gate_up=bf(w_gate_up),
             w_down=bf(w_down), norm_final=norm_final)
    bsz, s, d = x_prompt.shape
    nb = x_sample.shape[0]
    hk = DA_HEADS, d // DA_HEADS
    tiles, dbias, lb = param_tables(rel_bias, lower_bound, cache_k.shape[1])

    y_p, k_p, v_p, st_p = _trunk(x_prompt.reshape(bsz * s, d), None, None, None, tiles, dbias, lb, p,
                                 batch_shape=(bsz, s))
    y_s, k_s, v_s, st_s = _trunk(x_sample.reshape(nb, d), state_hgrn, (cache_k, cache_v), page_table,
                                 tiles, dbias, lb, p, batch_shape=(nb, 1))
    return (y_p.reshape(bsz, s, d), y_s.reshape(nb, 1, d),
            k_p.reshape(bsz, s, *hk), v_p.reshape(bsz, s, *hk), st_p,
            k_s.reshape(nb, 1, *hk), v_s.reshape(nb, 1, *hk), st_s)
```

```python
import functools
import math

import numpy as np
import jax
import jax.numpy as jnp
from jax import lax
from jax.experimental import pallas as pl
from jax.experimental.pallas import tpu as pltpu

F32 = jnp.float32
BF16 = jnp.bfloat16
EPS = 1e-6
NEG_BIG = -1e30
LOG2E = math.log2(math.e)

HG_HEADS = 8
DA_HEADS = 4
N_BUCKETS = 32
MAX_DISTANCE = 128
LANES = 128
SUBLANES = 8
MXU_N = 256
VMEM_LIMIT = 56 * 1024 * 1024

_NT = (((1,), (1,)), ((), ()))
_TN = (((0,), (0,)), ((), ()))


def _cparams(sem):
    return pltpu.CompilerParams(dimension_semantics=sem, vmem_limit_bytes=VMEM_LIMIT)


def _resident(shape):
    return pl.BlockSpec(shape, lambda *_: (0,) * len(shape), pipeline_mode=pl.Buffered(1))


def _resident_layer(w, layer):
    return pl.BlockSpec((1,) + w.shape[1:], lambda *_: (layer, 0, 0), pipeline_mode=pl.Buffered(1))


def _sigmoid(x):
    return 1.0 / (1.0 + jnp.exp(-x))


def _silu(x):
    return x * _sigmoid(x)


def _rms(x, g):
    ms = jnp.mean(x * x, axis=-1, keepdims=True)
    return x * lax.rsqrt(ms + EPS) * g


def _lane_tile(x, reps):
    return x if reps == 1 else jnp.concatenate([x] * reps, axis=1)


def _proj_kernel(x_ref, *refs, branches):
    nb = len(branches)
    g_refs = refs[:nb]
    w_refs = refs[nb:2 * nb]
    out_refs = refs[2 * nb:]
    x = x_ref[...]
    inv = lax.rsqrt(jnp.mean(x * x, axis=-1, keepdims=True) + EPS)
    o = 0
    for bi, (scale, groups) in enumerate(branches):
        xn = (x * inv * g_refs[bi][...]).astype(BF16)
        ng = w_refs[bi].shape[2] // len(groups)
        for gi, outs in enumerate(groups):
            acc = jnp.dot(xn, w_refs[bi][0, :, gi * ng:(gi + 1) * ng], preferred_element_type=F32)
            if scale != 1.0:
                acc = acc * scale
            for _ in outs:
                ref = out_refs[o]
                if len(ref.shape) == 3:
                    hd = ref.shape[2]
                    for hh in range(ref.shape[1]):
                        ref[:, hh, :] = acc[:, hh * hd:(hh + 1) * hd].astype(ref.dtype)
                else:
                    ref[...] = acc.astype(ref.dtype)
                o += 1


def proj(x, branches, *, tm, name):
    m, d = x.shape
    tm = min(tm, m)
    gains = [b[0].reshape(1, d) for b in branches]
    weights = [b[1] for b in branches]
    out_shape, out_specs = [], []
    for _, (w, _), _, groups in branches:
        ng = w.shape[2] // len(groups)
        for outs in groups:
            for dt, heads in outs:
                if heads:
                    out_shape.append(jax.ShapeDtypeStruct((m, heads, ng // heads), dt))
                    out_specs.append(pl.BlockSpec((tm, heads, ng // heads), lambda i: (i, 0, 0)))
                else:
                    out_shape.append(jax.ShapeDtypeStruct((m, ng), dt))
                    out_specs.append(pl.BlockSpec((tm, ng), lambda i: (i, 0)))
    return pl.pallas_call(
        functools.partial(_proj_kernel, branches=[(b[2], b[3]) for b in branches]),
        grid=(m // tm,),
        in_specs=([pl.BlockSpec((tm, d), lambda i: (i, 0))]
                  + [_resident(g.shape) for g in gains] + [_resident_layer(w, l) for w, l in weights]),
        out_specs=out_specs,
        out_shape=out_shape,
        compiler_params=_cparams(("parallel",)),
        name=name,
    )(x, *gains, *[w for w, _ in weights])


def _mix_ffn_kernel(a_ref, x_ref, wo_ref, g_ref, wgu_ref, wd_ref, gf_ref, o_ref, x1_scr, hid_scr, *, final_norm):
    dff = wd_ref.shape[1]
    x1 = x_ref[...] + jnp.dot(a_ref[...], wo_ref[0], preferred_element_type=F32)
    x1_scr[...] = x1
    xn = _rms(x1, g_ref[...]).astype(BF16)
    for j in range(dff // MXU_N):
        cs = slice(j * MXU_N, (j + 1) * MXU_N)
        gt = jnp.dot(xn, wgu_ref[0, :, cs], preferred_element_type=F32)
        ut = jnp.dot(xn, wgu_ref[0, :, dff + j * MXU_N:dff + (j + 1) * MXU_N], preferred_element_type=F32)
        hid_scr[:, cs] = (_silu(gt) * ut).astype(BF16)
    y = x1_scr[...] + jnp.dot(hid_scr[...], wd_ref[0], preferred_element_type=F32)
    if final_norm:
        y = _rms(y, gf_ref[...])
    o_ref[...] = y


def mix_ffn(a, x, w_out, g, w_gate_up, w_down, g_final, *, layer, tm, final_norm, name):
    m, d = x.shape
    dff = w_down.shape[1]
    assert dff % MXU_N == 0
    tm = min(tm, m)
    row = lambda width: pl.BlockSpec((tm, width), lambda i: (i, 0))
    return pl.pallas_call(
        functools.partial(_mix_ffn_kernel, final_norm=final_norm),
        grid=(m // tm,),
        in_specs=[row(a.shape[1]), row(d), _resident_layer(w_out, 0), _resident((1, d)),
                  _resident_layer(w_gate_up, layer), _resident_layer(w_down, layer), _resident((1, d))],
        out_specs=row(d),
        out_shape=jax.ShapeDtypeStruct((m, d), F32),
        scratch_shapes=[pltpu.VMEM((tm, d), F32), pltpu.VMEM((tm, dff), BF16)],
        compiler_params=_cparams(("parallel",)),
        name=name,
    )(a, x, w_out, g.reshape(1, d), w_gate_up, w_down, g_final.reshape(1, d))


BIAS_T = 2 * MAX_DISTANCE


def _bucket_bias(n, rb_ref, n_heads):
    max_exact = N_BUCKETS // 2
    nf = jnp.maximum(n, 1).astype(F32)
    large = max_exact + (jnp.log(nf / max_exact) / math.log(MAX_DISTANCE / max_exact)
                         * (N_BUCKETS - max_exact)).astype(jnp.int32)
    large = jnp.minimum(large, N_BUCKETS - 1)
    bucket = jnp.where(n < max_exact, n, large)
    outs = []
    for h in range(n_heads):
        far = rb_ref[N_BUCKETS - 1, h]
        acc = jnp.zeros(n.shape, F32)
        for b in range(N_BUCKETS - 1):
            acc = jnp.where(bucket == b, (rb_ref[b, h] - far) * LOG2E, acc)
        outs.append(acc)
    return outs


def _tables_kernel(rb_ref, lb_in_ref, tiles_ref, dec_ref, lb_ref):
    bt = tiles_ref.shape[-1]
    r = lax.broadcasted_iota(jnp.int32, (bt, bt), 0)
    c = lax.broadcasted_iota(jnp.int32, (bt, bt), 1)
    n0 = r - c
    diag = _bucket_bias(jnp.maximum(n0, 0), rb_ref, DA_HEADS)
    prev = _bucket_bias(bt + n0, rb_ref, DA_HEADS)
    for h in range(DA_HEADS):
        tiles_ref[0, h] = jnp.where(n0 >= 0, diag[h], NEG_BIG)
        tiles_ref[1, h] = prev[h]
    page_rows = dec_ref.shape[0] - 8
    rr = lax.broadcasted_iota(jnp.int32, dec_ref.shape, 0)
    ll = lax.broadcasted_iota(jnp.int32, dec_ref.shape, 1)
    nd = jnp.where(rr < page_rows, page_rows // DA_HEADS - rr // DA_HEADS, 0)
    dec = _bucket_bias(nd, rb_ref, DA_HEADS)
    acc = jnp.zeros(dec_ref.shape, F32)
    for h in range(DA_HEADS):
        acc = jnp.where(ll // 2 == h, dec[h], acc)
    dec_ref[...] = acc
    lbi = lb_in_ref[...]
    mx = jnp.max(lbi, axis=0, keepdims=True)
    e = jnp.exp(lbi - mx)
    lb_ref[...] = e[0:1, :] / jnp.sum(e, axis=0, keepdims=True)


def param_tables(rel_bias, lower_bound, page):
    d = lower_bound.shape[1]
    return pl.pallas_call(
        _tables_kernel,
        in_specs=[pl.BlockSpec(memory_space=pltpu.SMEM),
                  pl.BlockSpec(memory_space=pltpu.VMEM)],
        out_specs=[pl.BlockSpec(memory_space=pltpu.VMEM)] * 3,
        out_shape=[jax.ShapeDtypeStruct((2, DA_HEADS, BIAS_T, BIAS_T), F32),
                   jax.ShapeDtypeStruct((page * DA_HEADS + 8, LANES), F32),
                   jax.ShapeDtypeStruct((1, d), F32)],
        compiler_params=pltpu.CompilerParams(vmem_limit_bytes=VMEM_LIMIT),
        name="param_tables",
    )(rel_bias, lower_bound)


def _hgrn_sum_matrices(C):
    levels = int(math.log2(C))
    t = np.arange(C)
    u = t[None, :]
    mats = [u <= t[:, None], u > t[:, None]]
    for lv in range(levels):
        c = 1 << lv
        e = (t // (2 * c)) * (2 * c) + c - 1
        upper = (t > e)[:, None]
        seg = np.where(upper, (u > e[:, None]) & (u <= t[:, None]), (u > t[:, None]) & (u <= e[:, None]))
        mats.append(seg)
    return np.concatenate(mats, axis=0).astype(np.float32), levels


def _split2(x):
    hi = x.astype(BF16)
    lo = (x - hi.astype(F32)).astype(BF16)
    return hi, lo


def _hgrn_prompt_kernel(q_ref, f_ref, i_ref, g_ref, lb_ref, gn_ref, mall_ref, o_ref, st_ref, s_scr,
                        *, C, levels, nsub):
    cidx = pl.program_id(1)
    dk = s_scr.shape[-1]

    @pl.when(cidx == 0)
    def _():
        s_scr[...] = jnp.zeros_like(s_scr)

    lb = lb_ref[...]
    gn = gn_ref[...]
    mall = mall_ref[...]
    ti = lax.broadcasted_iota(jnp.int32, (C, C), 0)
    si = lax.broadcasted_iota(jnp.int32, (C, C), 1)
    txs = jnp.bitwise_xor(ti, si)
    lower = ti > si
    valid = [jnp.logical_and(jnp.right_shift(txs, lv) == 1, lower) for lv in range(levels)]
    trow = lax.broadcasted_iota(jnp.int32, (C, q_ref.shape[-1]), 0)
    upper = [jnp.bitwise_and(jnp.right_shift(trow, lv), 1) == 1 for lv in range(levels)]

    for sub in range(nsub):
        rows = slice(sub * C, (sub + 1) * C)
        fg_all = lb + (1.0 - lb) * _sigmoid(f_ref[0, rows, :])
        hi, lo = _split2(jnp.log(fg_all) * LOG2E)
        sums = jnp.dot(mall, hi, preferred_element_type=F32)
        ends = sums[0:2 * C] + jnp.dot(mall[0:2 * C], lo, preferred_element_type=F32)

        q = _silu(q_ref[0, rows, :])
        kk = 1.0 - fg_all
        v = i_ref[0, rows, :]
        v_bf = v.astype(BF16)
        b = ends[0:C]
        suf = ends[C:2 * C]
        qe = (q * jnp.exp2(b)).astype(BF16)
        kt = (kk * jnp.exp2(suf)).astype(BF16)
        dec_last = jnp.exp2(b[C - 1:C, :])

        scores = [jnp.zeros((C, C), F32) for _ in range(HG_HEADS)]
        for lv in range(levels):
            e = jnp.exp2(sums[(2 + lv) * C:(3 + lv) * C])
            if (1 << lv) >= SUBLANES:
                role = jnp.concatenate([(q if (r >> lv) & 1 else kk)[r:r + SUBLANES]
                                        for r in range(0, C, SUBLANES)], axis=0)
            else:
                role = jnp.where(upper[lv], q, kk)
            x_bf = (role * e).astype(BF16)
            for h in range(HG_HEADS):
                hs = slice(h * dk, (h + 1) * dk)
                sl = lax.dot_general(x_bf[:, hs], x_bf[:, hs], _NT, preferred_element_type=F32)
                scores[h] = jnp.where(valid[lv], sl, scores[h])

        qk = q * kk
        g = g_ref[0, rows, :]
        for h in range(HG_HEADS):
            hs = slice(h * dk, (h + 1) * dk)
            st = s_scr[h]
            diag = jnp.sum(qk[:, hs], axis=-1, keepdims=True)
            o = (jnp.dot(scores[h].astype(BF16), v_bf[:, hs], preferred_element_type=F32)
                 + diag * v[:, hs]
                 + lax.dot_general(qe[:, hs], st.astype(BF16), _NT, preferred_element_type=F32))
            s_scr[h] = st * dec_last[:, hs] + lax.dot_general(v_bf[:, hs], kt[:, hs], _TN,
                                                              preferred_element_type=F32)
            o_ref[0, rows, hs] = (_rms(o, gn[:, hs]) * _silu(g[:, hs])).astype(o_ref.dtype)

    @pl.when(cidx == pl.num_programs(1) - 1)
    def _():
        for h in range(HG_HEADS):
            st_ref[0, 0, h] = s_scr[h].T


def hgrn_prompt(q, f, i, g, lb, gnorm, *, C, nsub):
    bsz, s, d = q.shape
    dk = d // HG_HEADS
    mall_np, levels = _hgrn_sum_matrices(C)
    mall = jnp.asarray(mall_np, dtype=BF16)
    tb = C * nsub
    blk = pl.BlockSpec((1, tb, d), lambda b, c: (b, c, 0))
    vec = pl.BlockSpec((1, d), lambda b, c: (0, 0))
    return pl.pallas_call(
        functools.partial(_hgrn_prompt_kernel, C=C, levels=levels, nsub=nsub),
        grid=(bsz, s // tb),
        in_specs=[blk, blk, blk, blk, vec, vec,
                  pl.BlockSpec(mall.shape, lambda b, c: (0, 0))],
        out_specs=[pl.BlockSpec((1, tb, d), lambda b, c: (b, c, 0)),
                   pl.BlockSpec((1, 1, HG_HEADS, dk, dk), lambda b, c: (0, b, 0, 0, 0))],
        out_shape=[jax.ShapeDtypeStruct((bsz, s, d), BF16),
                   jax.ShapeDtypeStruct((1, bsz, HG_HEADS, dk, dk), F32)],
        scratch_shapes=[pltpu.VMEM((HG_HEADS, dk, dk), F32)],
        compiler_params=_cparams(("parallel", "arbitrary")),
        name="hgrn_prompt",
    )(q, f, i, g, lb, gnorm.reshape(1, d), mall)


def _hgrn_step_kernel(q_ref, f_ref, i_ref, g_ref, lb_ref, gn_ref, s_ref, o_ref, so_ref):
    rows = q_ref.shape[0]
    lb = lb_ref[...]
    fg = lb + (1.0 - lb) * _sigmoid(f_ref[...])
    q_t = _silu(q_ref[...]).T
    fg_t = fg.T
    kk_t = (1.0 - fg).T
    v = i_ref[...]
    outs = []
    for r in range(rows):
        bl, h = divmod(r, HG_HEADS)
        s_new = fg_t[:, r:r + 1] * s_ref[0, bl, h] + kk_t[:, r:r + 1] * v[r:r + 1, :]
        so_ref[0, bl, h] = s_new
        outs.append(jnp.sum(q_t[:, r:r + 1] * s_new, axis=0, keepdims=True))
    o = jnp.concatenate(outs, axis=0)
    o_ref[...] = (_rms(o, gn_ref[...]) * _silu(g_ref[...])).astype(o_ref.dtype)


def hgrn_step(q, f, i, g, lb, gnorm, state, *, bb):
    bsz, d = q.shape
    dk = d // HG_HEADS
    rows = bb * HG_HEADS
    to_rows = lambda a: a.reshape(bsz * HG_HEADS, dk)
    tile = lambda p: jnp.tile(p.reshape(HG_HEADS, dk), (bb, 1))
    rblk = pl.BlockSpec((rows, dk), lambda b: (b, 0))
    pblk = pl.BlockSpec((rows, dk), lambda b: (0, 0))
    sblk = pl.BlockSpec((1, bb, HG_HEADS, dk, dk), lambda b: (0, b, 0, 0, 0))
    o, s_new = pl.pallas_call(
        _hgrn_step_kernel,
        grid=(bsz // bb,),
        in_specs=[rblk, rblk, rblk, rblk, pblk, pblk, sblk],
        out_specs=[rblk, sblk],
        out_shape=[jax.ShapeDtypeStruct((bsz * HG_HEADS, dk), BF16),
                   jax.ShapeDtypeStruct(state.shape, F32)],
        compiler_params=_cparams(("parallel",)),
        name="hgrn_step",
    )(to_rows(q), to_rows(f), to_rows(i), to_rows(g), tile(lb), tile(gnorm), state)
    return o.reshape(bsz, d), s_new


def _lambda(lq1_ref, lk1_ref, lq2_ref, lk2_ref, lam_init):
    s1 = jnp.sum(lq1_ref[...] * lk1_ref[...], axis=-1, keepdims=True)
    s2 = jnp.sum(lq2_ref[...] * lk2_ref[...], axis=-1, keepdims=True)
    return jnp.exp(s1) - jnp.exp(s2) + lam_init


def _attn_prompt_kernel(qi_ref, kj_ref, var_ref, q_ref, k_ref, vp_ref, vd_ref, tiles_ref,
                        lq1_ref, lk1_ref, lq2_ref, lk2_ref, sub_ref, o_ref,
                        m_scr, l_scr, acc_scr, p_scr, a_scr, *, lam_init, variants):
    step = pl.program_id(1)
    j = kj_ref[step]
    var = var_ref[step]
    tq = q_ref.shape[1]
    tk = k_ref.shape[1]
    bt = tiles_ref.shape[-1]
    dv = acc_scr.shape[-1]
    dh = dv // 2

    @pl.when(j == 0)
    def _():
        m_scr[...] = jnp.full_like(m_scr, NEG_BIG)
        l_scr[...] = jnp.zeros_like(l_scr)
        acc_scr[...] = jnp.zeros_like(acc_scr)
        p_scr[...] = jnp.zeros_like(p_scr)
        a_scr[...] = jnp.zeros_like(a_scr)

    def add_bias(h, s, base):
        rows = []
        for r in range(tq // bt):
            cols = []
            for c in range(s.shape[1] // bt):
                dist = base + r - c
                blk = s[r * bt:(r + 1) * bt, c * bt:(c + 1) * bt]
                if dist < 0:
                    blk = jnp.full((bt, bt), NEG_BIG, F32)
                elif dist <= 1:
                    blk = blk + tiles_ref[dist, h]
                cols.append(blk)
            rows.append(jnp.concatenate(cols, axis=1))
        return jnp.concatenate(rows, axis=0)

    def product(idx, vh):
        kw = vh.shape[0]
        p = p_scr[idx] if kw == tk else p_scr[idx, :, 0:kw]
        acc_scr[idx] = (_lane_tile(a_scr[idx], dv // LANES) * acc_scr[idx]
                        + jnp.dot(p, vh, preferred_element_type=F32))

    def logits(idx, qh, kh, base):
        kw = kh.shape[0]
        s = lax.dot_general(qh, kh, _NT, preferred_element_type=F32)
        if base is not None:
            s = add_bias(idx // 2, s, base)
        m_prev = m_scr[idx]
        m_new = jnp.maximum(m_prev, jnp.max(s, axis=-1, keepdims=True))
        alpha = jnp.exp2(m_prev - m_new)
        chunks = [jnp.exp2(s[:, t * LANES:(t + 1) * LANES] - m_new) for t in range(kw // LANES)]
        psum = chunks[0]
        for ch in chunks[1:]:
            psum = psum + ch
        p = jnp.concatenate([ch.astype(BF16) for ch in chunks], axis=1)
        if kw == tk:
            p_scr[idx] = p
        else:
            p_scr[idx, :, 0:kw] = p
        a_scr[idx] = alpha
        l_scr[idx] = alpha * l_scr[idx] + psum
        m_scr[idx] = m_new

    def sweep(base, kw, last):
        q = q_ref[0]
        k = k_ref[0]
        vp = vp_ref[0]
        for h in range(DA_HEADS):
            for c in range(2):
                idx = 2 * h + c
                lo = h * dv + c * dh
                product(idx, vp[:, h * dv:(h + 1) * dv])
                logits(idx, q[:, lo:lo + dh], k[0:kw, lo:lo + dh], base)
        if not last:
            return
        vd = vd_ref[0]
        for idx in range(2 * DA_HEADS):
            product(idx, vd[0:kw, (idx // 2) * dv:(idx // 2 + 1) * dv])
        lam = _lambda(lq1_ref, lk1_ref, lq2_ref, lk2_ref, lam_init)
        sub = sub_ref[...]
        for h in range(DA_HEADS):
            inv1 = 1.0 / jnp.sum(l_scr[2 * h], axis=-1, keepdims=True)
            inv2 = 1.0 / jnp.sum(l_scr[2 * h + 1], axis=-1, keepdims=True)
            o = acc_scr[2 * h] * inv1 - lam * (acc_scr[2 * h + 1] * inv2)
            o_ref[0, :, h * dv:(h + 1) * dv] = (_rms(o, sub) * (1.0 - lam_init)).astype(o_ref.dtype)

    for vid, (base, kw, last) in enumerate(variants):
        pl.when(var == vid)(functools.partial(sweep, base, kw, last))


def attn_prompt(q, k, v, tiles, lams, subln, *, tq, tk, lam_init):
    bsz, s, d = q.shape
    bt = tiles.shape[-1]
    assert tq % bt == 0 and tk % bt == 0 and s % tq == 0 and s % tk == 0
    dv = d // DA_HEADS
    qi, kj, var, variants = [], [], [], []
    for i in range(s // tq):
        jlast = ((i + 1) * tq - 1) // tk
        for j in range(jlast + 1):
            base = (tq // bt) * i - (tk // bt) * j
            kw = min(tk, (i + 1) * tq - j * tk)
            key = (base if base - (kw // bt - 1) <= 1 else None, kw, j == jlast)
            if key not in variants:
                variants.append(key)
            qi.append(i)
            kj.append(j)
            var.append(variants.index(key))
    n_steps = len(qi)
    qi, kj, var = (jnp.asarray(a, jnp.int32) for a in (qi, kj, var))
    dl = lams[0].shape[-1]
    const = lambda shape: pl.BlockSpec(shape, lambda b, st, qi, kj, var: (0,) * len(shape))
    grid_spec = pltpu.PrefetchScalarGridSpec(
        num_scalar_prefetch=3,
        grid=(bsz, n_steps),
        in_specs=[pl.BlockSpec((1, tq, d), lambda b, st, qi, kj, var: (b, qi[st], 0)),
                  pl.BlockSpec((1, tk, d), lambda b, st, qi, kj, var: (b, kj[st], 0)),
                  pl.BlockSpec((1, tk, d), lambda b, st, qi, kj, var: (b, jnp.maximum(kj[st] - 1, 0), 0)),
                  pl.BlockSpec((1, tk, d), lambda b, st, qi, kj, var: (b, ((qi[st] + 1) * tq - 1) // tk, 0)),
                  const(tiles.shape), const((1, dl)), const((1, dl)), const((1, dl)), const((1, dl)),
                  const((1, dv))],
        out_specs=pl.BlockSpec((1, tq, d), lambda b, st, qi, kj, var: (b, qi[st], 0)),
        scratch_shapes=[pltpu.VMEM((2 * DA_HEADS, tq, LANES), F32),
                        pltpu.VMEM((2 * DA_HEADS, tq, LANES), F32),
                        pltpu.VMEM((2 * DA_HEADS, tq, dv), F32),
                        pltpu.VMEM((2 * DA_HEADS, tq, tk), BF16),
                        pltpu.VMEM((2 * DA_HEADS, tq, LANES), F32)])
    return pl.pallas_call(
        functools.partial(_attn_prompt_kernel, lam_init=lam_init, variants=tuple(variants)),
        grid_spec=grid_spec,
        out_shape=jax.ShapeDtypeStruct((bsz, s, d), BF16),
        compiler_params=_cparams(("parallel", "arbitrary")),
        name="attn_prompt",
    )(qi, kj, var, q, k, v, v, tiles, *lams, subln.reshape(1, dv))


def _attn_decode_kernel(pt_ref, q_ref, kn_ref, vn_ref, ck_hbm, cv_hbm, dbias_ref,
                        lq1_ref, lk1_ref, lq2_ref, lk2_ref, sub_ref, o_ref, kbuf, vbuf, sem,
                        *, n_rows, n_pages, lam_init):
    b = pl.program_id(0)
    d = q_ref.shape[-1]
    dv = d // DA_HEADS
    dh = dv // 2
    page = kbuf.shape[2]
    rows = page * DA_HEADS

    def page_copies(row, slot):
        cps = []
        for r in range(n_pages):
            pg = pt_ref[row, r]
            cps.append(pltpu.make_async_copy(ck_hbm.at[pg], kbuf.at[slot, r], sem.at[slot]))
            cps.append(pltpu.make_async_copy(cv_hbm.at[pg], vbuf.at[slot, r], sem.at[slot]))
        return cps

    @pl.when(b == 0)
    def _():
        for row in range(min(DECODE_BUFS - 1, n_rows)):
            for cp in page_copies(row, row):
                cp.start()

    @pl.when(b + (DECODE_BUFS - 1) < n_rows)
    def _():
        nxt = b + (DECODE_BUFS - 1)
        for cp in page_copies(nxt, nxt % DECODE_BUFS):
            cp.start()

    slot = b % DECODE_BUFS
    for cp in page_copies(b, slot):
        cp.wait()
    k_refs = [kbuf.at[slot, r] for r in range(n_pages)]
    v_refs = [vbuf.at[slot, r] for r in range(n_pages)]

    def rows2d(ref):
        return ref[...].reshape(rows, dv).astype(BF16)

    def head_rows(x):
        return jnp.concatenate([x[:, h * dv:(h + 1) * dv] for h in range(DA_HEADS)]
                               + [jnp.zeros((8 - DA_HEADS, dv), x.dtype)], axis=0)

    def own_head(n, n_valid):
        r = lax.broadcasted_iota(jnp.int32, (n, LANES), 0)
        j = lax.broadcasted_iota(jnp.int32, (n, LANES), 1)
        return jnp.logical_and(r % DA_HEADS == j // 2, jnp.logical_and(j < 2 * DA_HEADS, r < n_valid))

    q = q_ref[0].astype(F32)
    jrow = lax.broadcasted_iota(jnp.int32, (LANES, dv), 0)
    lane = lax.broadcasted_iota(jnp.int32, (LANES, dv), 1)
    qmat = jnp.zeros((LANES, dv), F32)
    for h in range(DA_HEADS):
        qh = jnp.broadcast_to(q[:, h * dv:(h + 1) * dv], (LANES, dv))
        qmat = jnp.where(jnp.logical_and(jrow // 2 == h, lane // dh == jrow % 2), qh, qmat)
    qmat = qmat.astype(BF16)

    eye = (lax.broadcasted_iota(jnp.int32, (LANES, LANES), 0) == lax.broadcasted_iota(jnp.int32, (LANES, LANES), 1))

    def to_rows(x):
        return jnp.sum(jnp.where(eye, jnp.broadcast_to(x, (LANES, LANES)), 0.0), axis=1, keepdims=True)

    kn8 = head_rows(kn_ref[0]).astype(BF16)
    s_new = lax.dot_general(kn8, qmat, _NT, preferred_element_type=F32) + dbias_ref[rows:rows + 8, :]
    s_new = jnp.where(own_head(8, DA_HEADS), s_new, NEG_BIG)
    m = jnp.max(s_new, axis=0, keepdims=True)
    p_new = jnp.exp2(s_new - m)
    lsum = jnp.sum(p_new, axis=0, keepdims=True)
    acc = lax.dot_general(p_new.astype(BF16), head_rows(vn_ref[0]).astype(BF16), _TN, preferred_element_type=F32)

    valid = own_head(rows, rows)
    group = 4
    assert n_pages % group == 0
    for g0 in range(0, n_pages, group):
        s_grp = []
        for r in range(g0, g0 + group):
            s = lax.dot_general(rows2d(k_refs[r]), qmat, _NT, preferred_element_type=F32)
            if r == n_pages - 1:
                s = s + dbias_ref[0:rows, :]
            s_grp.append(jnp.where(valid, s, NEG_BIG))
        smax = s_grp[0]
        for s in s_grp[1:]:
            smax = jnp.maximum(smax, s)
        m_new = jnp.maximum(m, jnp.max(smax, axis=0, keepdims=True))
        alpha = jnp.exp2(m - m_new)
        psum = jnp.zeros((rows, LANES), F32)
        acc_g = jnp.zeros(acc.shape, F32)
        for r in range(g0, g0 + group):
            p = jnp.exp2(s_grp[r - g0] - m_new)
            psum = psum + p
            acc_g = acc_g + lax.dot_general(p.astype(BF16), rows2d(v_refs[r]), _TN, preferred_element_type=F32)
        lsum = alpha * lsum + jnp.sum(psum, axis=0, keepdims=True)
        acc = to_rows(alpha) * acc + acc_g
        m = m_new
    an = acc * to_rows(1.0 / lsum)
    lam = _lambda(lq1_ref, lk1_ref, lq2_ref, lk2_ref, lam_init)
    sub = sub_ref[...]
    for h in range(DA_HEADS):
        o = an[2 * h:2 * h + 1, :] - lam * an[2 * h + 1:2 * h + 2, :]
        o_ref[0, :, h * dv:(h + 1) * dv] = (_rms(o, sub) * (1.0 - lam_init)).astype(o_ref.dtype)


def attn_decode(q, k_new, v_new, cache_k, cache_v, page_table, dbias, lams, subln, *, lam_init):
    bsz, d = q.shape
    n_pages = page_table.shape[1]
    page = cache_k.shape[1]
    assert page == LANES
    dv = d // DA_HEADS
    dl = lams[0].shape[-1]
    row_spec = pl.BlockSpec((1, 1, d), lambda b, pt: (b, 0, 0))
    hbm_spec = pl.BlockSpec(memory_space=pl.ANY)
    lam_spec = pl.BlockSpec((1, dl), lambda b, pt: (0, 0))
    page_buf = pltpu.VMEM((DECODE_BUFS, n_pages, page, DA_HEADS, dv), cache_k.dtype)
    grid_spec = pltpu.PrefetchScalarGridSpec(
        num_scalar_prefetch=1,
        grid=(bsz,),
        in_specs=[row_spec, row_spec, row_spec, hbm_spec, hbm_spec,
                  pl.BlockSpec(dbias.shape, lambda b, pt: (0, 0)),
                  lam_spec, lam_spec, lam_spec, lam_spec,
                  pl.BlockSpec((1, dv), lambda b, pt: (0, 0))],
        out_specs=pl.BlockSpec((1, 1, d), lambda b, pt: (b, 0, 0)),
        scratch_shapes=[page_buf, page_buf, pltpu.SemaphoreType.DMA((DECODE_BUFS,))])
    r3 = lambda a: a.reshape(bsz, 1, d)
    out = pl.pallas_call(
        functools.partial(_attn_decode_kernel, n_rows=bsz, n_pages=n_pages, lam_init=lam_init),
        grid_spec=grid_spec,
        out_shape=jax.ShapeDtypeStruct((bsz, 1, d), BF16),
        compiler_params=_cparams(("arbitrary",)),
        name="attn_decode",
    )(page_table, r3(q), r3(k_new), r3(v_new), cache_k, cache_v, dbias, *lams, subln.reshape(1, dv))
    return out.reshape(bsz, d)


DECODE_BUFS = 3
ATTN_TQ = 512
ATTN_TK = 512
HGRN_C = 64
HGRN_NSUB = 8
TM = 512


def _trunk(x, hg_state, kv_past, page_table, tiles, dbias, lb, p, *, batch_shape):
    m, d = x.shape
    prompt = hg_state is None
    dv = d // DA_HEADS
    scale = (dv // 2) ** -0.5
    lam_init = 0.8 - 0.6 * math.exp(-0.3 * 1)
    lams = [p[n][0].reshape(1, -1) for n in ("lambda_q1", "lambda_k1", "lambda_q2", "lambda_k2")]

    q, f, i, g = proj(x, [(p["norm_mix"][0], (p["w_in_a"], 0), 1.0, [[(F32, 0)]] * 4)], tm=TM, name="in_proj")
    if prompt:
        bsz, s = batch_shape
        r3 = lambda a: a.reshape(bsz, s, d)
        o, state = hgrn_prompt(r3(q), r3(f), r3(i), r3(g), lb, p["gnorm_a"][0], C=HGRN_C, nsub=HGRN_NSUB)
        o = o.reshape(m, d)
    else:
        o, state = hgrn_step(q, f, i, g, lb, p["gnorm_a"][0], hg_state, bb=16)
    x = mix_ffn(o, x, p["w_out_a"], p["norm_ffn"][0], p["w_gate_up"], p["w_down"], p["norm_final"],
                layer=0, tm=TM, final_norm=False, name="mix_ffn0")

    k32, k16, v32, v16, qa = proj(
        x, [(p["kv_norm"], (p["w_kv"], 0), 1.0, [[(F32, DA_HEADS), (BF16, 0)]] * 2),
            (p["norm_mix"][1], (p["w_q_b"], 0), scale * LOG2E, [[(BF16, 0)]])],
        tm=TM, name="kvq_proj")

    if prompt:
        o = attn_prompt(r3(qa), r3(k16), r3(v16), tiles, lams, p["subln_b"][0],
                        tq=min(ATTN_TQ, s), tk=min(ATTN_TK, s), lam_init=lam_init)
        o = o.reshape(m, d)
    else:
        o = attn_decode(qa, k32.reshape(m, d), v32.reshape(m, d), kv_past[0], kv_past[1], page_table, dbias,
                        lams, p["subln_b"][0], lam_init=lam_init)
    y = mix_ffn(o, x, p["w_out_b"], p["norm_ffn"][1], p["w_gate_up"], p["w_down"], p["norm_final"],
                layer=1, tm=TM, final_norm=True, name="mix_ffn1")
    return y, k32, v32, state


def kernel(x_prompt, x_sample, cache_k, cache_v, state_hgrn, page_table, w_in_a, lower_bound, gnorm_a, w_out_a,
           w_q_b, lambda_q1, lambda_k1, lambda_q2, lambda_k2, subln_b, w_out_b, kv_norm, w_kv, rel_bias,
           norm_mix, norm_ffn, w_gate_up, w_down, norm_final):
    bf = lambda w: (w if w.ndim == 3 else w[None]).astype(BF16)
    p = dict(w_in_a=bf(w_in_a), gnorm_a=gnorm_a, w_out_a=bf(w_out_a), w_q_b=bf(w_q_b), lambda_q1=lambda_q1,
             lambda_k1=lambda_k1, lambda_q2=lambda_q2, lambda_k2=lambda_k2, subln_b=subln_b, w_out_b=bf(w_out_b),
             kv_norm=kv_norm, w_kv=bf(w_kv), norm_mix=norm_mix, norm_ffn=norm_ffn, w_gate_up=bf(w_gate_up),
             w_down=bf(w_down), norm_final=norm_final)
    bsz, s, d = x_prompt.shape
    nb = x_sample.shape[0]
    hk = DA_HEADS, d // DA_HEADS
    tiles, dbias, lb = param_tables(rel_bias, lower_bound, cache_k.shape[1])

    y_p, k_p, v_p, st_p = _trunk(x_prompt.reshape(bsz * s, d), None, None, None, tiles, dbias, lb, p,
                                 batch_shape=(bsz, s))
    y_s, k_s, v_s, st_s = _trunk(x_sample.reshape(nb, d), state_hgrn, (cache_k, cache_v), page_table,
                                 tiles, dbias, lb, p, batch_shape=(nb, 1))
    return (y_p.reshape(bsz, s, d), y_s.reshape(nb, 1, d),
            k_p.reshape(bsz, s, *hk), v_p.reshape(bsz, s, *hk), st_p,
            k_s.reshape(nb, 1, *hk), v_s.reshape(nb, 1, *hk), st_s)
```

```python
import functools
import math

import numpy as np
import jax
import jax.numpy as jnp
from jax import lax
from jax.experimental import pallas as pl
from jax.experimental.pallas import tpu as pltpu

F32 = jnp.float32
BF16 = jnp.bfloat16
EPS = 1e-6
NEG_BIG = -1e30
LOG2E = math.log2(math.e)

HG_HEADS = 8
DA_HEADS = 4
N_BUCKETS = 32
MAX_DISTANCE = 128
LANES = 128
SUBLANES = 8
MXU_N = 256
VMEM_LIMIT = 56 * 1024 * 1024

_NT = (((1,), (1,)), ((), ()))
_TN = (((0,), (0,)), ((), ()))


def _cparams(sem):
    return pltpu.CompilerParams(dimension_semantics=sem, vmem_limit_bytes=VMEM_LIMIT)


def _resident(shape):
    return pl.BlockSpec(shape, lambda *_: (0,) * len(shape), pipeline_mode=pl.Buffered(1))


def _resident_layer(w, layer):
    return pl.BlockSpec((1,) + w.shape[1:], lambda *_: (layer, 0, 0), pipeline_mode=pl.Buffered(1))


def _sigmoid(x):
    return 1.0 / (1.0 + jnp.exp(-x))


def _silu(x):
    return x * _sigmoid(x)


def _rms(x, g):
    ms = jnp.mean(x * x, axis=-1, keepdims=True)
    return x * lax.rsqrt(ms + EPS) * g


def _lane_tile(x, reps):
    return x if reps == 1 else jnp.concatenate([x] * reps, axis=1)


def _proj_kernel(x_ref, *refs, branches):
    nb = len(branches)
    g_refs = refs[:nb]
    w_refs = refs[nb:2 * nb]
    out_refs = refs[2 * nb:]
    x = x_ref[...]
    inv = lax.rsqrt(jnp.mean(x * x, axis=-1, keepdims=True) + EPS)
    o = 0
    for bi, (scale, groups) in enumerate(branches):
        xn = (x * inv * g_refs[bi][...]).astype(BF16)
        ng = w_refs[bi].shape[2] // len(groups)
        for gi, outs in enumerate(groups):
            acc = jnp.dot(xn, w_refs[bi][0, :, gi * ng:(gi + 1) * ng], preferred_element_type=F32)
            if scale != 1.0:
                acc = acc * scale
            for _ in outs:
                ref = out_refs[o]
                if len(ref.shape) == 3:
                    hd = ref.shape[2]
                    for hh in range(ref.shape[1]):
                        ref[:, hh, :] = acc[:, hh * hd:(hh + 1) * hd].astype(ref.dtype)
                else:
                    ref[...] = acc.astype(ref.dtype)
                o += 1


def proj(x, branches, *, tm, name):
    m, d = x.shape
    tm = min(tm, m)
    gains = [b[0].reshape(1, d) for b in branches]
    weights = [b[1] for b in branches]
    out_shape, out_specs = [], []
    for _, (w, _), _, groups in branches:
        ng = w.shape[2] // len(groups)
        for outs in groups:
            for dt, heads in outs:
                if heads:
                    out_shape.append(jax.ShapeDtypeStruct((m, heads, ng // heads), dt))
                    out_specs.append(pl.BlockSpec((tm, heads, ng // heads), lambda i: (i, 0, 0)))
                else:
                    out_shape.append(jax.ShapeDtypeStruct((m, ng), dt))
                    out_specs.append(pl.BlockSpec((tm, ng), lambda i: (i, 0)))
    return pl.pallas_call(
        functools.partial(_proj_kernel, branches=[(b[2], b[3]) for b in branches]),
        grid=(m // tm,),
        in_specs=([pl.BlockSpec((tm, d), lambda i: (i, 0))]
                  + [_resident(g.shape) for g in gains] + [_resident_layer(w, l) for w, l in weights]),
        out_specs=out_specs,
        out_shape=out_shape,
        compiler_params=_cparams(("parallel",)),
        name=name,
    )(x, *gains, *[w for w, _ in weights])


def _mix_ffn_kernel(a_ref, x_ref, wo_ref, g_ref, wgu_ref, wd_ref, gf_ref, o_ref, x1_scr, hid_scr, *, final_norm):
    dff = wd_ref.shape[1]
    x1 = x_ref[...] + jnp.dot(a_ref[...], wo_ref[0], preferred_element_type=F32)
    x1_scr[...] = x1
    xn = _rms(x1, g_ref[...]).astype(BF16)
    for j in range(dff // MXU_N):
        cs = slice(j * MXU_N, (j + 1) * MXU_N)
        gt = jnp.dot(xn, wgu_ref[0, :, cs], preferred_element_type=F32)
        ut = jnp.dot(xn, wgu_ref[0, :, dff + j * MXU_N:dff + (j + 1) * MXU_N], preferred_element_type=F32)
        hid_scr[:, cs] = (_silu(gt) * ut).astype(BF16)
    y = x1_scr[...] + jnp.dot(hid_scr[...], wd_ref[0], preferred_element_type=F32)
    if final_norm:
        y = _rms(y, gf_ref[...])
    o_ref[...] = y


def mix_ffn(a, x, w_out, g, w_gate_up, w_down, g_final, *, layer, tm, final_norm, name):
    m, d = x.shape
    dff = w_down.shape[1]
    assert dff % MXU_N == 0
    tm = min(tm, m)
    row = lambda width: pl.BlockSpec((tm, width), lambda i: (i, 0))
    return pl.pallas_call(
        functools.partial(_mix_ffn_kernel, final_norm=final_norm),
        grid=(m // tm,),
        in_specs=[row(a.shape[1]), row(d), _resident_layer(w_out, 0), _resident((1, d)),
                  _resident_layer(w_gate_up, layer), _resident_layer(w_down, layer), _resident((1, d))],
        out_specs=row(d),
        out_shape=jax.ShapeDtypeStruct((m, d), F32),
        scratch_shapes=[pltpu.VMEM((tm, d), F32), pltpu.VMEM((tm, dff), BF16)],
        compiler_params=_cparams(("parallel",)),
        name=name,
    )(a, x, w_out, g.reshape(1, d), w_gate_up, w_down, g_final.reshape(1, d))


BIAS_T = 2 * MAX_DISTANCE


def _bucket_starts():
    max_exact = N_BUCKETS // 2
    n = np.arange(2 * MAX_DISTANCE, dtype=np.int32)
    nf = np.maximum(n, 1).astype(np.float32)
    large = max_exact + (np.log(nf / np.float32(max_exact)) / np.float32(math.log(MAX_DISTANCE / max_exact))
                         * np.float32(N_BUCKETS - max_exact)).astype(np.int32)
    bucket = np.where(n < max_exact, n, np.minimum(large, N_BUCKETS - 1))
    assert (np.diff(bucket) >= 0).all() and (np.diff(bucket) <= 1).all() and bucket[-1] == N_BUCKETS - 1
    return [int(np.argmax(bucket >= b)) for b in range(N_BUCKETS)]


def _bucket_bias(n, rb_ref, n_heads):
    starts = _bucket_starts()
    reached = [n >= starts[b] for b in range(1, N_BUCKETS - 1)]
    outs = []
    for h in range(n_heads):
        far = rb_ref[N_BUCKETS - 1, h]
        acc = jnp.full(n.shape, (rb_ref[0, h] - far) * LOG2E, F32)
        for b in range(1, N_BUCKETS - 1):
            acc = jnp.where(reached[b - 1], (rb_ref[b, h] - far) * LOG2E, acc)
        outs.append(jnp.where(n >= starts[N_BUCKETS - 1], 0.0, acc))
    return outs


def _tables_kernel(rb_ref, lb_in_ref, tiles_ref, dec_ref, lb_ref):
    bt = tiles_ref.shape[-1]
    r = lax.broadcasted_iota(jnp.int32, (bt, bt), 0)
    c = lax.broadcasted_iota(jnp.int32, (bt, bt), 1)
    n0 = r - c
    diag = _bucket_bias(jnp.maximum(n0, 0), rb_ref, DA_HEADS)
    prev = _bucket_bias(bt + n0, rb_ref, DA_HEADS)
    for h in range(DA_HEADS):
        tiles_ref[0, h] = jnp.where(n0 >= 0, diag[h], NEG_BIG)
        tiles_ref[1, h] = prev[h]
    page_rows = dec_ref.shape[0] - 8
    rr = lax.broadcasted_iota(jnp.int32, dec_ref.shape, 0)
    ll = lax.broadcasted_iota(jnp.int32, dec_ref.shape, 1)
    nd = jnp.where(rr < page_rows, page_rows // DA_HEADS - rr // DA_HEADS, 0)
    dec = _bucket_bias(nd, rb_ref, DA_HEADS)
    acc = jnp.zeros(dec_ref.shape, F32)
    for h in range(DA_HEADS):
        acc = jnp.where(ll // 2 == h, dec[h], acc)
    dec_ref[...] = acc
    lbi = lb_in_ref[...]
    mx = jnp.max(lbi, axis=0, keepdims=True)
    e = jnp.exp(lbi - mx)
    lb_ref[...] = e[0:1, :] / jnp.sum(e, axis=0, keepdims=True)


def param_tables(rel_bias, lower_bound, page):
    d = lower_bound.shape[1]
    return pl.pallas_call(
        _tables_kernel,
        in_specs=[pl.BlockSpec(memory_space=pltpu.SMEM),
                  pl.BlockSpec(memory_space=pltpu.VMEM)],
        out_specs=[pl.BlockSpec(memory_space=pltpu.VMEM)] * 3,
        out_shape=[jax.ShapeDtypeStruct((2, DA_HEADS, BIAS_T, BIAS_T), F32),
                   jax.ShapeDtypeStruct((page * DA_HEADS + 8, LANES), F32),
                   jax.ShapeDtypeStruct((1, d), F32)],
        compiler_params=pltpu.CompilerParams(vmem_limit_bytes=VMEM_LIMIT),
        name="param_tables",
    )(rel_bias, lower_bound)


def _hgrn_sum_matrices(C):
    levels = int(math.log2(C))
    t = np.arange(C)
    u = t[None, :]
    mats = [u <= t[:, None], u > t[:, None]]
    for lv in range(levels):
        c = 1 << lv
        e = (t // (2 * c)) * (2 * c) + c - 1
        upper = (t > e)[:, None]
        seg = np.where(upper, (u > e[:, None]) & (u <= t[:, None]), (u > t[:, None]) & (u <= e[:, None]))
        mats.append(seg)
    return np.concatenate(mats, axis=0).astype(np.float32), levels


def _split2(x):
    hi = x.astype(BF16)
    lo = (x - hi.astype(F32)).astype(BF16)
    return hi, lo


def _hgrn_chunks(q_ref, f_ref, i_ref, g_ref, lb_ref, gn_ref, mall_ref, o_ref, st_ref, s_scr,
                 *, first, last, C, levels, nsub):
    dk = s_scr.shape[-1]

    @pl.when(first)
    def _():
        s_scr[...] = jnp.zeros_like(s_scr)

    lb = lb_ref[...]
    gn = gn_ref[...]
    mall = mall_ref[...]
    ti = lax.broadcasted_iota(jnp.int32, (C, C), 0)
    si = lax.broadcasted_iota(jnp.int32, (C, C), 1)
    txs = jnp.bitwise_xor(ti, si)
    lower = ti > si
    valid = [jnp.logical_and(jnp.right_shift(txs, lv) == 1, lower) for lv in range(levels)]
    trow = lax.broadcasted_iota(jnp.int32, (C, q_ref.shape[-1]), 0)
    upper = [jnp.bitwise_and(jnp.right_shift(trow, lv), 1) == 1 for lv in range(levels)]

    for sub in range(nsub):
        rows = slice(sub * C, (sub + 1) * C)
        fg_all = lb + (1.0 - lb) * _sigmoid(f_ref[0, rows, :])
        hi, lo = _split2(jnp.log(fg_all) * LOG2E)
        sums = jnp.dot(mall, hi, preferred_element_type=F32)
        ends = sums[0:2 * C] + jnp.dot(mall[0:2 * C], lo, preferred_element_type=F32)

        q = _silu(q_ref[0, rows, :])
        kk = 1.0 - fg_all
        v = i_ref[0, rows, :]
        v_bf = v.astype(BF16)
        b = ends[0:C]
        suf = ends[C:2 * C]
        qe = (q * jnp.exp2(b)).astype(BF16)
        kt = (kk * jnp.exp2(suf)).astype(BF16)
        dec_last = jnp.exp2(b[C - 1:C, :])

        scores = [jnp.zeros((C, C), F32) for _ in range(HG_HEADS)]
        for lv in range(levels):
            e = jnp.exp2(sums[(2 + lv) * C:(3 + lv) * C])
            if (1 << lv) >= SUBLANES:
                role = jnp.concatenate([(q if (r >> lv) & 1 else kk)[r:r + SUBLANES]
                                        for r in range(0, C, SUBLANES)], axis=0)
            else:
                role = jnp.where(upper[lv], q, kk)
            x_bf = (role * e).astype(BF16)
            for h in range(HG_HEADS):
                hs = slice(h * dk, (h + 1) * dk)
                sl = lax.dot_general(x_bf[:, hs], x_bf[:, hs], _NT, preferred_element_type=F32)
                scores[h] = jnp.where(valid[lv], sl, scores[h])

        qk = q * kk
        g = g_ref[0, rows, :]
        for h in range(HG_HEADS):
            hs = slice(h * dk, (h + 1) * dk)
            st = s_scr[h]
            diag = jnp.sum(qk[:, hs], axis=-1, keepdims=True)
            o = (jnp.dot(scores[h].astype(BF16), v_bf[:, hs], preferred_element_type=F32)
                 + diag * v[:, hs]
                 + lax.dot_general(qe[:, hs], st.astype(BF16), _NT, preferred_element_type=F32))
            s_scr[h] = st * dec_last[:, hs] + lax.dot_general(v_bf[:, hs], kt[:, hs], _TN,
                                                              preferred_element_type=F32)
            o_ref[0, rows, hs] = (_rms(o, gn[:, hs]) * _silu(g[:, hs])).astype(o_ref.dtype)

    @pl.when(last)
    def _():
        for h in range(HG_HEADS):
            st_ref[0, 0, h] = s_scr[h].T


def _hgrn_prompt_kernel(*refs, C, levels, nsub):
    cidx = pl.program_id(1)
    _hgrn_chunks(*refs, first=cidx == 0, last=cidx == pl.num_programs(1) - 1, C=C, levels=levels, nsub=nsub)


def hgrn_prompt(q, f, i, g, lb, gnorm, *, C, nsub):
    bsz, s, d = q.shape
    dk = d // HG_HEADS
    mall_np, levels = _hgrn_sum_matrices(C)
    mall = jnp.asarray(mall_np, dtype=BF16)
    tb = C * nsub
    blk = pl.BlockSpec((1, tb, d), lambda b, c: (b, c, 0))
    vec = pl.BlockSpec((1, d), lambda b, c: (0, 0))
    return pl.pallas_call(
        functools.partial(_hgrn_prompt_kernel, C=C, levels=levels, nsub=nsub),
        grid=(bsz, s // tb),
        in_specs=[blk, blk, blk, blk, vec, vec,
                  pl.BlockSpec(mall.shape, lambda b, c: (0, 0))],
        out_specs=[pl.BlockSpec((1, tb, d), lambda b, c: (b, c, 0)),
                   pl.BlockSpec((1, 1, HG_HEADS, dk, dk), lambda b, c: (0, b, 0, 0, 0))],
        out_shape=[jax.ShapeDtypeStruct((bsz, s, d), BF16),
                   jax.ShapeDtypeStruct((1, bsz, HG_HEADS, dk, dk), F32)],
        scratch_shapes=[pltpu.VMEM((HG_HEADS, dk, dk), F32)],
        compiler_params=_cparams(("parallel", "arbitrary")),
        name="hgrn_prompt",
    )(q, f, i, g, lb, gnorm.reshape(1, d), mall)


def _hgrn_step_kernel(q_ref, f_ref, i_ref, g_ref, lb_ref, gn_ref, s_ref, o_ref, so_ref):
    rows = q_ref.shape[0]
    lb = lb_ref[...]
    fg = lb + (1.0 - lb) * _sigmoid(f_ref[...])
    q_t = _silu(q_ref[...]).T
    fg_t = fg.T
    kk_t = (1.0 - fg).T
    v = i_ref[...]
    outs = []
    for r in range(rows):
        bl, h = divmod(r, HG_HEADS)
        s_new = fg_t[:, r:r + 1] * s_ref[0, bl, h] + kk_t[:, r:r + 1] * v[r:r + 1, :]
        so_ref[0, bl, h] = s_new
        outs.append(jnp.sum(q_t[:, r:r + 1] * s_new, axis=0, keepdims=True))
    o = jnp.concatenate(outs, axis=0)
    o_ref[...] = (_rms(o, gn_ref[...]) * _silu(g_ref[...])).astype(o_ref.dtype)


def hgrn_step(q, f, i, g, lb, gnorm, state, *, bb):
    bsz, d = q.shape
    dk = d // HG_HEADS
    rows = bb * HG_HEADS
    to_rows = lambda a: a.reshape(bsz * HG_HEADS, dk)
    tile = lambda p: jnp.tile(p.reshape(HG_HEADS, dk), (bb, 1))
    rblk = pl.BlockSpec((rows, dk), lambda b: (b, 0))
    pblk = pl.BlockSpec((rows, dk), lambda b: (0, 0))
    sblk = pl.BlockSpec((1, bb, HG_HEADS, dk, dk), lambda b: (0, b, 0, 0, 0))
    o, s_new = pl.pallas_call(
        _hgrn_step_kernel,
        grid=(bsz // bb,),
        in_specs=[rblk, rblk, rblk, rblk, pblk, pblk, sblk],
        out_specs=[rblk, sblk],
        out_shape=[jax.ShapeDtypeStruct((bsz * HG_HEADS, dk), BF16),
                   jax.ShapeDtypeStruct(state.shape, F32)],
        compiler_params=_cparams(("parallel",)),
        name="hgrn_step",
    )(to_rows(q), to_rows(f), to_rows(i), to_rows(g), tile(lb), tile(gnorm), state)
    return o.reshape(bsz, d), s_new


def _lambda(lq1_ref, lk1_ref, lq2_ref, lk2_ref, lam_init):
    s1 = jnp.sum(lq1_ref[...] * lk1_ref[...], axis=-1, keepdims=True)
    s2 = jnp.sum(lq2_ref[...] * lk2_ref[...], axis=-1, keepdims=True)
    return jnp.exp(s1) - jnp.exp(s2) + lam_init


def _attn_prompt_kernel(qi_ref, kj_ref, var_ref, q_ref, k_ref, vp_ref, vd_ref, tiles_ref,
                        lq1_ref, lk1_ref, lq2_ref, lk2_ref, sub_ref, o_ref,
                        m_scr, l_scr, acc_scr, p_scr, a_scr, *, lam_init, variants):
    step = pl.program_id(1)
    j = kj_ref[step]
    var = var_ref[step]
    tq = q_ref.shape[1]
    tk = k_ref.shape[1]
    bt = tiles_ref.shape[-1]
    dv = acc_scr.shape[-1]
    dh = dv // 2

    @pl.when(j == 0)
    def _():
        m_scr[...] = jnp.full_like(m_scr, NEG_BIG)
        l_scr[...] = jnp.zeros_like(l_scr)
        acc_scr[...] = jnp.zeros_like(acc_scr)
        p_scr[...] = jnp.zeros_like(p_scr)
        a_scr[...] = jnp.zeros_like(a_scr)

    def add_bias(h, s, base):
        rows = []
        for r in range(tq // bt):
            cols = []
            for c in range(s.shape[1] // bt):
                dist = base + r - c
                blk = s[r * bt:(r + 1) * bt, c * bt:(c + 1) * bt]
                if dist < 0:
                    blk = jnp.full((bt, bt), NEG_BIG, F32)
                elif dist <= 1:
                    blk = blk + tiles_ref[dist, h]
                cols.append(blk)
            rows.append(jnp.concatenate(cols, axis=1))
        return jnp.concatenate(rows, axis=0)

    def product(idx, vh):
        kw = vh.shape[0]
        p = p_scr[idx] if kw == tk else p_scr[idx, :, 0:kw]
        acc_scr[idx] = (_lane_tile(a_scr[idx], dv // LANES) * acc_scr[idx]
                        + jnp.dot(p, vh, preferred_element_type=F32))

    def logits(idx, qh, kh, base):
        kw = kh.shape[0]
        s = lax.dot_general(qh, kh, _NT, preferred_element_type=F32)
        if base is not None:
            s = add_bias(idx // 2, s, base)
        m_prev = m_scr[idx]
        m_new = jnp.maximum(m_prev, jnp.max(s, axis=-1, keepdims=True))
        alpha = jnp.exp2(m_prev - m_new)
        chunks = [jnp.exp2(s[:, t * LANES:(t + 1) * LANES] - m_new) for t in range(kw // LANES)]
        psum = chunks[0]
        for ch in chunks[1:]:
            psum = psum + ch
        p = jnp.concatenate([ch.astype(BF16) for ch in chunks], axis=1)
        if kw == tk:
            p_scr[idx] = p
        else:
            p_scr[idx, :, 0:kw] = p
        a_scr[idx] = alpha
        l_scr[idx] = alpha * l_scr[idx] + psum
        m_scr[idx] = m_new

    def sweep(base, kw, last):
        q = q_ref[0]
        k = k_ref[0]
        vp = vp_ref[0]
        for h in range(DA_HEADS):
            for c in range(2):
                idx = 2 * h + c
                lo = h * dv + c * dh
                product(idx, vp[:, h * dv:(h + 1) * dv])
                logits(idx, q[:, lo:lo + dh], k[0:kw, lo:lo + dh], base)
        if not last:
            return
        vd = vd_ref[0]
        for idx in range(2 * DA_HEADS):
            product(idx, vd[0:kw, (idx // 2) * dv:(idx // 2 + 1) * dv])
        lam = _lambda(lq1_ref, lk1_ref, lq2_ref, lk2_ref, lam_init)
        sub = sub_ref[...]
        for h in range(DA_HEADS):
            inv1 = 1.0 / jnp.sum(l_scr[2 * h], axis=-1, keepdims=True)
            inv2 = 1.0 / jnp.sum(l_scr[2 * h + 1], axis=-1, keepdims=True)
            o = acc_scr[2 * h] * inv1 - lam * (acc_scr[2 * h + 1] * inv2)
            o_ref[0, :, h * dv:(h + 1) * dv] = (_rms(o, sub) * (1.0 - lam_init)).astype(o_ref.dtype)

    for vid, (base, kw, last) in enumerate(variants):
        pl.when(var == vid)(functools.partial(sweep, base, kw, last))


def attn_prompt(q, k, v, tiles, lams, subln, *, tq, tk, lam_init):
    bsz, s, d = q.shape
    bt = tiles.shape[-1]
    assert tq % bt == 0 and tk % bt == 0 and s % tq == 0 and s % tk == 0
    dv = d // DA_HEADS
    qi, kj, var, variants = [], [], [], []
    for i in range(s // tq):
        jlast = ((i + 1) * tq - 1) // tk
        for j in range(jlast + 1):
            base = (tq // bt) * i - (tk // bt) * j
            kw = min(tk, (i + 1) * tq - j * tk)
            key = (base if base - (kw // bt - 1) <= 1 else None, kw, j == jlast)
            if key not in variants:
                variants.append(key)
            qi.append(i)
            kj.append(j)
            var.append(variants.index(key))
    n_steps = len(qi)
    qi, kj, var = (jnp.asarray(a, jnp.int32) for a in (qi, kj, var))
    dl = lams[0].shape[-1]
    const = lambda shape: pl.BlockSpec(shape, lambda b, st, qi, kj, var: (0,) * len(shape))
    grid_spec = pltpu.PrefetchScalarGridSpec(
        num_scalar_prefetch=3,
        grid=(bsz, n_steps),
        in_specs=[pl.BlockSpec((1, tq, d), lambda b, st, qi, kj, var: (b, qi[st], 0)),
                  pl.BlockSpec((1, tk, d), lambda b, st, qi, kj, var: (b, kj[st], 0)),
                  pl.BlockSpec((1, tk, d), lambda b, st, qi, kj, var: (b, jnp.maximum(kj[st] - 1, 0), 0)),
                  pl.BlockSpec((1, tk, d), lambda b, st, qi, kj, var: (b, ((qi[st] + 1) * tq - 1) // tk, 0)),
                  const(tiles.shape), const((1, dl)), const((1, dl)), const((1, dl)), const((1, dl)),
                  const((1, dv))],
        out_specs=pl.BlockSpec((1, tq, d), lambda b, st, qi, kj, var: (b, qi[st], 0)),
        scratch_shapes=[pltpu.VMEM((2 * DA_HEADS, tq, LANES), F32),
                        pltpu.VMEM((2 * DA_HEADS, tq, LANES), F32),
                        pltpu.VMEM((2 * DA_HEADS, tq, dv), F32),
                        pltpu.VMEM((2 * DA_HEADS, tq, tk), BF16),
                        pltpu.VMEM((2 * DA_HEADS, tq, LANES), F32)])
    return pl.pallas_call(
        functools.partial(_attn_prompt_kernel, lam_init=lam_init, variants=tuple(variants)),
        grid_spec=grid_spec,
        out_shape=jax.ShapeDtypeStruct((bsz, s, d), BF16),
        compiler_params=_cparams(("parallel", "arbitrary")),
        name="attn_prompt",
    )(qi, kj, var, q, k, v, v, tiles, *lams, subln.reshape(1, dv))


def _decode_fetch(b, pt_ref, ck_hbm, cv_hbm, kbuf, vbuf, sem, *, n_rows):
    bufs, n_pages = kbuf.shape[0], kbuf.shape[1]

    def page_copies(row, slot):
        cps = []
        for r in range(n_pages):
            pg = pt_ref[row, r]
            cps.append(pltpu.make_async_copy(ck_hbm.at[pg], kbuf.at[slot, r], sem.at[slot]))
            cps.append(pltpu.make_async_copy(cv_hbm.at[pg], vbuf.at[slot, r], sem.at[slot]))
        return cps

    @pl.when(b == 0)
    def _():
        for row in range(min(bufs - 1, n_rows)):
            for cp in page_copies(row, row):
                cp.start()

    @pl.when(b + (bufs - 1) < n_rows)
    def _():
        nxt = b + (bufs - 1)
        for cp in page_copies(nxt, nxt % bufs):
            cp.start()

    slot = b % bufs
    return ([kbuf.at[slot, r] for r in range(n_pages)], [vbuf.at[slot, r] for r in range(n_pages)],
            page_copies(b, slot))


def _decode_row(k_refs, v_refs, q_ref, kn_ref, vn_ref, dbias_ref, lq1_ref, lk1_ref, lq2_ref, lk2_ref, sub_ref,
                o_ref, *, lam_init):
    n_pages = len(k_refs)
    d = q_ref.shape[-1]
    dv = d // DA_HEADS
    dh = dv // 2
    page = k_refs[0].shape[0]
    rows = page * DA_HEADS

    def rows2d(ref):
        return ref[...].reshape(rows, dv).astype(BF16)

    def head_rows(x):
        return jnp.concatenate([x[:, h * dv:(h + 1) * dv] for h in range(DA_HEADS)]
                               + [jnp.zeros((8 - DA_HEADS, dv), x.dtype)], axis=0)

    def own_head(n, n_valid):
        r = lax.broadcasted_iota(jnp.int32, (n, LANES), 0)
        j = lax.broadcasted_iota(jnp.int32, (n, LANES), 1)
        return jnp.logical_and(r % DA_HEADS == j // 2, jnp.logical_and(j < 2 * DA_HEADS, r < n_valid))

    q = q_ref[0].astype(F32)
    jrow = lax.broadcasted_iota(jnp.int32, (LANES, dv), 0)
    lane = lax.broadcasted_iota(jnp.int32, (LANES, dv), 1)
    qmat = jnp.zeros((LANES, dv), F32)
    for h in range(DA_HEADS):
        qh = jnp.broadcast_to(q[:, h * dv:(h + 1) * dv], (LANES, dv))
        qmat = jnp.where(jnp.logical_and(jrow // 2 == h, lane // dh == jrow % 2), qh, qmat)
    qmat = qmat.astype(BF16)

    eye = (lax.broadcasted_iota(jnp.int32, (LANES, LANES), 0) == lax.broadcasted_iota(jnp.int32, (LANES, LANES), 1))

    def to_rows(x):
        return jnp.sum(jnp.where(eye, jnp.broadcast_to(x, (LANES, LANES)), 0.0), axis=1, keepdims=True)

    kn8 = head_rows(kn_ref[0]).astype(BF16)
    s_new = lax.dot_general(kn8, qmat, _NT, preferred_element_type=F32) + dbias_ref[rows:rows + 8, :]
    s_new = jnp.where(own_head(8, DA_HEADS), s_new, NEG_BIG)
    m = jnp.max(s_new, axis=0, keepdims=True)
    p_new = jnp.exp2(s_new - m)
    lsum = jnp.sum(p_new, axis=0, keepdims=True)
    acc = lax.dot_general(p_new.astype(BF16), head_rows(vn_ref[0]).astype(BF16), _TN, preferred_element_type=F32)

    valid = own_head(rows, rows)
    group = 4
    assert n_pages % group == 0
    for g0 in range(0, n_pages, group):
        s_grp = []
        for r in range(g0, g0 + group):
            s = lax.dot_general(rows2d(k_refs[r]), qmat, _NT, preferred_element_type=F32)
            if r == n_pages - 1:
                s = s + dbias_ref[0:rows, :]
            s_grp.append(jnp.where(valid, s, NEG_BIG))
        smax = s_grp[0]
        for s in s_grp[1:]:
            smax = jnp.maximum(smax, s)
        m_new = jnp.maximum(m, jnp.max(smax, axis=0, keepdims=True))
        alpha = jnp.exp2(m - m_new)
        psum = jnp.zeros((rows, LANES), F32)
        acc_g = jnp.zeros(acc.shape, F32)
        for r in range(g0, g0 + group):
            p = jnp.exp2(s_grp[r - g0] - m_new)
            psum = psum + p
            acc_g = acc_g + lax.dot_general(p.astype(BF16), rows2d(v_refs[r]), _TN, preferred_element_type=F32)
        lsum = alpha * lsum + jnp.sum(psum, axis=0, keepdims=True)
        acc = to_rows(alpha) * acc + acc_g
        m = m_new
    an = acc * to_rows(1.0 / lsum)
    lam = _lambda(lq1_ref, lk1_ref, lq2_ref, lk2_ref, lam_init)
    sub = sub_ref[...]
    for h in range(DA_HEADS):
        o = an[2 * h:2 * h + 1, :] - lam * an[2 * h + 1:2 * h + 2, :]
        o_ref[0, :, h * dv:(h + 1) * dv] = (_rms(o, sub) * (1.0 - lam_init)).astype(o_ref.dtype)


def _attn_decode_kernel(pt_ref, q_ref, kn_ref, vn_ref, ck_hbm, cv_hbm, dbias_ref,
                        lq1_ref, lk1_ref, lq2_ref, lk2_ref, sub_ref, o_ref, kbuf, vbuf, sem, *, n_rows, lam_init):
    k_refs, v_refs, pending = _decode_fetch(pl.program_id(0), pt_ref, ck_hbm, cv_hbm, kbuf, vbuf, sem,
                                            n_rows=n_rows)
    for cp in pending:
        cp.wait()
    _decode_row(k_refs, v_refs, q_ref, kn_ref, vn_ref, dbias_ref, lq1_ref, lk1_ref, lq2_ref, lk2_ref, sub_ref,
                o_ref, lam_init=lam_init)


def attn_decode(q, k_new, v_new, cache_k, cache_v, page_table, dbias, lams, subln, *, lam_init):
    bsz, d = q.shape
    n_pages = page_table.shape[1]
    page = cache_k.shape[1]
    assert page == LANES
    dv = d // DA_HEADS
    dl = lams[0].shape[-1]
    row_spec = pl.BlockSpec((1, 1, d), lambda b, pt: (b, 0, 0))
    hbm_spec = pl.BlockSpec(memory_space=pl.ANY)
    lam_spec = pl.BlockSpec((1, dl), lambda b, pt: (0, 0))
    page_buf = pltpu.VMEM((DECODE_BUFS, n_pages, page, DA_HEADS, dv), cache_k.dtype)
    grid_spec = pltpu.PrefetchScalarGridSpec(
        num_scalar_prefetch=1,
        grid=(bsz,),
        in_specs=[row_spec, row_spec, row_spec, hbm_spec, hbm_spec,
                  pl.BlockSpec(dbias.shape, lambda b, pt: (0, 0)),
                  lam_spec, lam_spec, lam_spec, lam_spec,
                  pl.BlockSpec((1, dv), lambda b, pt: (0, 0))],
        out_specs=pl.BlockSpec((1, 1, d), lambda b, pt: (b, 0, 0)),
        scratch_shapes=[page_buf, page_buf, pltpu.SemaphoreType.DMA((DECODE_BUFS,))])
    r3 = lambda a: a.reshape(bsz, 1, d)
    out = pl.pallas_call(
        functools.partial(_attn_decode_kernel, n_rows=bsz, lam_init=lam_init),
        grid_spec=grid_spec,
        out_shape=jax.ShapeDtypeStruct((bsz, 1, d), BF16),
        compiler_params=_cparams(("arbitrary",)),
        name="attn_decode",
    )(page_table, r3(q), r3(k_new), r3(v_new), cache_k, cache_v, dbias, *lams, subln.reshape(1, dv))
    return out.reshape(bsz, d)


def _hgrn_decode_kernel(pt_ref, q_ref, f_ref, i_ref, g_ref, lb_ref, gn_ref, mall_ref,
                        dq_ref, kn_ref, vn_ref, ck_hbm, cv_hbm, dbias_ref, lq1_ref, lk1_ref, lq2_ref, lk2_ref,
                        sub_ref, o_ref, st_ref, do_ref, s_scr, kbuf, vbuf, sem,
                        *, C, levels, nsub, n_rows, lam_init):
    cidx = pl.program_id(1)
    row = pl.program_id(0) * pl.num_programs(1) + cidx
    k_refs, v_refs, pending = _decode_fetch(row, pt_ref, ck_hbm, cv_hbm, kbuf, vbuf, sem, n_rows=n_rows)
    _hgrn_chunks(q_ref, f_ref, i_ref, g_ref, lb_ref, gn_ref, mall_ref, o_ref, st_ref, s_scr,
                 first=cidx == 0, last=cidx == pl.num_programs(1) - 1, C=C, levels=levels, nsub=nsub)
    for cp in pending:
        cp.wait()
    _decode_row(k_refs, v_refs, dq_ref, kn_ref, vn_ref, dbias_ref, lq1_ref, lk1_ref, lq2_ref, lk2_ref, sub_ref,
                do_ref, lam_init=lam_init)


def hgrn_prompt_decode(q, f, i, g, lb, gnorm, dq, k_new, v_new, cache_k, cache_v, page_table, dbias, lams, subln,
                       *, C, nsub, lam_init):
    bsz, s, d = q.shape
    n_rows = dq.shape[0]
    dk = d // HG_HEADS
    dv = d // DA_HEADS
    tb = C * nsub
    steps = s // tb
    assert bsz * steps == n_rows
    n_pages = page_table.shape[1]
    page = cache_k.shape[1]
    mall_np, levels = _hgrn_sum_matrices(C)
    mall = jnp.asarray(mall_np, dtype=BF16)
    dl = lams[0].shape[-1]
    blk = pl.BlockSpec((1, tb, d), lambda b, c, pt: (b, c, 0))
    const = lambda shape: pl.BlockSpec(shape, lambda b, c, pt: (0,) * len(shape))
    row_spec = pl.BlockSpec((1, 1, d), lambda b, c, pt: (b * steps + c, 0, 0))
    hbm_spec = pl.BlockSpec(memory_space=pl.ANY)
    page_buf = pltpu.VMEM((2, n_pages, page, DA_HEADS, dv), cache_k.dtype)
    grid_spec = pltpu.PrefetchScalarGridSpec(
        num_scalar_prefetch=1,
        grid=(bsz, steps),
        in_specs=[blk, blk, blk, blk, const((1, d)), const((1, d)), const(mall.shape),
                  row_spec, row_spec, row_spec, hbm_spec, hbm_spec, const(dbias.shape),
                  const((1, dl)), const((1, dl)), const((1, dl)), const((1, dl)), const((1, dv))],
        out_specs=[pl.BlockSpec((1, tb, d), lambda b, c, pt: (b, c, 0)),
                   pl.BlockSpec((1, 1, HG_HEADS, dk, dk), lambda b, c, pt: (0, b, 0, 0, 0)),
                   row_spec],
        scratch_shapes=[pltpu.VMEM((HG_HEADS, dk, dk), F32), page_buf, page_buf,
                        pltpu.SemaphoreType.DMA((2,))])
    r3 = lambda a: a.reshape(n_rows, 1, d)
    o, state, o_dec = pl.pallas_call(
        functools.partial(_hgrn_decode_kernel, C=C, levels=levels, nsub=nsub, n_rows=n_rows, lam_init=lam_init),
        grid_spec=grid_spec,
        out_shape=[jax.ShapeDtypeStruct((bsz, s, d), BF16),
                   jax.ShapeDtypeStruct((1, bsz, HG_HEADS, dk, dk), F32),
                   jax.ShapeDtypeStruct((n_rows, 1, d), BF16)],
        compiler_params=_cparams(("arbitrary", "arbitrary")),
        name="hgrn_prompt_decode",
    )(page_table, q, f, i, g, lb, gnorm.reshape(1, d), mall, r3(dq), r3(k_new), r3(v_new), cache_k, cache_v,
      dbias, *lams, subln.reshape(1, dv))
    return o, state, o_dec.reshape(n_rows, d)


DECODE_BUFS = 3
ATTN_TQ = 512
ATTN_TK = 512
HGRN_C = 64
HGRN_NSUB = 8
TM = 512


def _in_proj(x, p):
    return proj(x, [(p["norm_mix"][0], (p["w_in_a"], 0), 1.0, [[(F32, 0)]] * 4)], tm=TM, name="in_proj")


def _after_mixer0(o, x, p, scale):
    x = mix_ffn(o, x, p["w_out_a"], p["norm_ffn"][0], p["w_gate_up"], p["w_down"], p["norm_final"],
                layer=0, tm=TM, final_norm=False, name="mix_ffn0")
    k32, k16, v32, v16, qa = proj(
        x, [(p["kv_norm"], (p["w_kv"], 0), 1.0, [[(F32, DA_HEADS), (BF16, 0)]] * 2),
            (p["norm_mix"][1], (p["w_q_b"], 0), scale * LOG2E, [[(BF16, 0)]])],
        tm=TM, name="kvq_proj")
    return x, k32, k16, v32, v16, qa


def _after_mixer1(o, x, p):
    return mix_ffn(o, x, p["w_out_b"], p["norm_ffn"][1], p["w_gate_up"], p["w_down"], p["norm_final"],
                   layer=1, tm=TM, final_norm=True, name="mix_ffn1")


def kernel(x_prompt, x_sample, cache_k, cache_v, state_hgrn, page_table, w_in_a, lower_bound, gnorm_a, w_out_a,
           w_q_b, lambda_q1, lambda_k1, lambda_q2, lambda_k2, subln_b, w_out_b, kv_norm, w_kv, rel_bias,
           norm_mix, norm_ffn, w_gate_up, w_down, norm_final):
    bf = lambda w: (w if w.ndim == 3 else w[None]).astype(BF16)
    p = dict(w_in_a=bf(w_in_a), w_out_a=bf(w_out_a), w_q_b=bf(w_q_b), w_out_b=bf(w_out_b), kv_norm=kv_norm,
             w_kv=bf(w_kv), norm_mix=norm_mix, norm_ffn=norm_ffn, w_gate_up=bf(w_gate_up), w_down=bf(w_down),
             norm_final=norm_final)
    bsz, s, d = x_prompt.shape
    nb = x_sample.shape[0]
    dv = d // DA_HEADS
    scale = (dv // 2) ** -0.5
    lam_init = 0.8 - 0.6 * math.exp(-0.3 * 1)
    lams = [a[0].reshape(1, -1) for a in (lambda_q1, lambda_k1, lambda_q2, lambda_k2)]
    gnorm, subln = gnorm_a[0], subln_b[0]
    tiles, dbias, lb = param_tables(rel_bias, lower_bound, cache_k.shape[1])
    xp = x_prompt.reshape(bsz * s, d)
    xs = x_sample.reshape(nb, d)
    r3 = lambda a: a.reshape(bsz, s, d)

    q, f, i, g = _in_proj(xs, p)
    o_s, st_s = hgrn_step(q, f, i, g, lb, gnorm, state_hgrn, bb=16)
    xs, k_s, _, v_s, _, qa_s = _after_mixer0(o_s, xs, p, scale)

    q, f, i, g = _in_proj(xp, p)
    tokens_per_row = (bsz * s) // nb
    if (bsz * s) % nb == 0 and tokens_per_row % HGRN_C == 0 and s % tokens_per_row == 0:
        o_p, st_p, o_dec = hgrn_prompt_decode(
            r3(q), r3(f), r3(i), r3(g), lb, gnorm, qa_s, k_s.reshape(nb, d), v_s.reshape(nb, d), cache_k, cache_v,
            page_table, dbias, lams, subln, C=HGRN_C, nsub=tokens_per_row // HGRN_C, lam_init=lam_init)
    else:
        o_p, st_p = hgrn_prompt(r3(q), r3(f), r3(i), r3(g), lb, gnorm, C=HGRN_C, nsub=HGRN_NSUB)
        o_dec = attn_decode(qa_s, k_s.reshape(nb, d), v_s.reshape(nb, d), cache_k, cache_v, page_table, dbias,
                            lams, subln, lam_init=lam_init)
    xp, k_p, k16, v_p, v16, qa_p = _after_mixer0(o_p.reshape(bsz * s, d), xp, p, scale)

    o_a = attn_prompt(r3(qa_p), r3(k16), r3(v16), tiles, lams, subln,
                      tq=min(ATTN_TQ, s), tk=min(ATTN_TK, s), lam_init=lam_init)
    y_p = _after_mixer1(o_a.reshape(bsz * s, d), xp, p)
    y_s = _after_mixer1(o_dec, xs, p)
    hk = DA_HEADS, dv
    return (y_p.reshape(bsz, s, d), y_s.reshape(nb, 1, d),
            k_p.reshape(bsz, s, *hk), v_p.reshape(bsz, s, *hk), st_p,
            k_s.reshape(nb, 1, *hk), v_s.reshape(nb, 1, *hk), st_s)
```

```python
import functools
import math

import numpy as np
import jax
import jax.numpy as jnp
from jax import lax
from jax.experimental import pallas as pl
from jax.experimental.pallas import tpu as pltpu

F32 = jnp.float32
BF16 = jnp.bfloat16
EPS = 1e-6
NEG_BIG = -1e30
LOG2E = math.log2(math.e)

HG_HEADS = 8
DA_HEADS = 4
N_BUCKETS = 32
MAX_DISTANCE = 128
LANES = 128
SUBLANES = 8
MXU_N = 256
VMEM_LIMIT = 56 * 1024 * 1024

_NT = (((1,), (1,)), ((), ()))
_TN = (((0,), (0,)), ((), ()))


def _cparams(sem):
    return pltpu.CompilerParams(dimension_semantics=sem, vmem_limit_bytes=VMEM_LIMIT)


def _resident(shape):
    return pl.BlockSpec(shape, lambda *_: (0,) * len(shape), pipeline_mode=pl.Buffered(1))


def _resident_layer(w, layer):
    return pl.BlockSpec((1,) + w.shape[1:], lambda *_: (layer, 0, 0), pipeline_mode=pl.Buffered(1))


def _sigmoid(x):
    return 1.0 / (1.0 + jnp.exp(-x))


def _silu(x):
    return x * _sigmoid(x)


def _rms(x, g):
    ms = jnp.mean(x * x, axis=-1, keepdims=True)
    return x * lax.rsqrt(ms + EPS) * g


def _lane_tile(x, reps):
    return x if reps == 1 else jnp.concatenate([x] * reps, axis=1)


def _proj_kernel(x_ref, *refs, branches):
    nb = len(branches)
    g_refs = refs[:nb]
    w_refs = refs[nb:2 * nb]
    out_refs = refs[2 * nb:]
    x = x_ref[...]
    inv = lax.rsqrt(jnp.mean(x * x, axis=-1, keepdims=True) + EPS)
    o = 0
    for bi, (scale, groups) in enumerate(branches):
        xn = (x * inv * g_refs[bi][...]).astype(BF16)
        ng = w_refs[bi].shape[2] // len(groups)
        for gi, outs in enumerate(groups):
            acc = jnp.dot(xn, w_refs[bi][0, :, gi * ng:(gi + 1) * ng], preferred_element_type=F32)
            if scale != 1.0:
                acc = acc * scale
            for _ in outs:
                ref = out_refs[o]
                if len(ref.shape) == 3:
                    hd = ref.shape[2]
                    for hh in range(ref.shape[1]):
                        ref[:, hh, :] = acc[:, hh * hd:(hh + 1) * hd].astype(ref.dtype)
                else:
                    ref[...] = acc.astype(ref.dtype)
                o += 1


def proj(x, branches, *, tm, name):
    m, d = x.shape
    tm = min(tm, m)
    gains = [b[0].reshape(1, d) for b in branches]
    weights = [b[1] for b in branches]
    out_shape, out_specs = [], []
    for _, (w, _), _, groups in branches:
        ng = w.shape[2] // len(groups)
        for outs in groups:
            for dt, heads in outs:
                if heads:
                    out_shape.append(jax.ShapeDtypeStruct((m, heads, ng // heads), dt))
                    out_specs.append(pl.BlockSpec((tm, heads, ng // heads), lambda i: (i, 0, 0)))
                else:
                    out_shape.append(jax.ShapeDtypeStruct((m, ng), dt))
                    out_specs.append(pl.BlockSpec((tm, ng), lambda i: (i, 0)))
    return pl.pallas_call(
        functools.partial(_proj_kernel, branches=[(b[2], b[3]) for b in branches]),
        grid=(m // tm,),
        in_specs=([pl.BlockSpec((tm, d), lambda i: (i, 0))]
                  + [_resident(g.shape) for g in gains] + [_resident_layer(w, l) for w, l in weights]),
        out_specs=out_specs,
        out_shape=out_shape,
        compiler_params=_cparams(("parallel",)),
        name=name,
    )(x, *gains, *[w for w, _ in weights])


def _mix_ffn_kernel(a_ref, x_ref, wo_ref, g_ref, wgu_ref, wd_ref, gf_ref, o_ref, x1_scr, hid_scr, *, final_norm):
    dff = wd_ref.shape[1]
    x1 = x_ref[...] + jnp.dot(a_ref[...], wo_ref[0], preferred_element_type=F32)
    x1_scr[...] = x1
    xn = _rms(x1, g_ref[...]).astype(BF16)
    for j in range(dff // MXU_N):
        cs = slice(j * MXU_N, (j + 1) * MXU_N)
        gt = jnp.dot(xn, wgu_ref[0, :, cs], preferred_element_type=F32)
        ut = jnp.dot(xn, wgu_ref[0, :, dff + j * MXU_N:dff + (j + 1) * MXU_N], preferred_element_type=F32)
        hid_scr[:, cs] = (_silu(gt) * ut).astype(BF16)
    y = x1_scr[...] + jnp.dot(hid_scr[...], wd_ref[0], preferred_element_type=F32)
    if final_norm:
        y = _rms(y, gf_ref[...])
    o_ref[...] = y


def mix_ffn(a, x, w_out, g, w_gate_up, w_down, g_final, *, layer, tm, final_norm, name):
    m, d = x.shape
    dff = w_down.shape[1]
    assert dff % MXU_N == 0
    tm = min(tm, m)
    row = lambda width: pl.BlockSpec((tm, width), lambda i: (i, 0))
    return pl.pallas_call(
        functools.partial(_mix_ffn_kernel, final_norm=final_norm),
        grid=(m // tm,),
        in_specs=[row(a.shape[1]), row(d), _resident_layer(w_out, 0), _resident((1, d)),
                  _resident_layer(w_gate_up, layer), _resident_layer(w_down, layer), _resident((1, d))],
        out_specs=row(d),
        out_shape=jax.ShapeDtypeStruct((m, d), F32),
        scratch_shapes=[pltpu.VMEM((tm, d), F32), pltpu.VMEM((tm, dff), BF16)],
        compiler_params=_cparams(("parallel",)),
        name=name,
    )(a, x, w_out, g.reshape(1, d), w_gate_up, w_down, g_final.reshape(1, d))


BIAS_T = 2 * MAX_DISTANCE


def _bucket_starts():
    max_exact = N_BUCKETS // 2
    n = np.arange(2 * MAX_DISTANCE, dtype=np.int32)
    nf = np.maximum(n, 1).astype(np.float32)
    large = max_exact + (np.log(nf / np.float32(max_exact)) / np.float32(math.log(MAX_DISTANCE / max_exact))
                         * np.float32(N_BUCKETS - max_exact)).astype(np.int32)
    bucket = np.where(n < max_exact, n, np.minimum(large, N_BUCKETS - 1))
    assert (np.diff(bucket) >= 0).all() and (np.diff(bucket) <= 1).all() and bucket[-1] == N_BUCKETS - 1
    return [int(np.argmax(bucket >= b)) for b in range(N_BUCKETS)]


def _bucket_bias(n, rb_ref, n_heads):
    starts = _bucket_starts()
    reached = [n >= starts[b] for b in range(1, N_BUCKETS - 1)]
    outs = []
    for h in range(n_heads):
        far = rb_ref[N_BUCKETS - 1, h]
        acc = jnp.full(n.shape, (rb_ref[0, h] - far) * LOG2E, F32)
        for b in range(1, N_BUCKETS - 1):
            acc = jnp.where(reached[b - 1], (rb_ref[b, h] - far) * LOG2E, acc)
        outs.append(jnp.where(n >= starts[N_BUCKETS - 1], 0.0, acc))
    return outs


def _tables_kernel(rb_ref, lb_in_ref, tiles_ref, dec_ref, lb_ref):
    bt = tiles_ref.shape[-1]
    r = lax.broadcasted_iota(jnp.int32, (bt, bt), 0)
    c = lax.broadcasted_iota(jnp.int32, (bt, bt), 1)
    n0 = r - c
    diag = _bucket_bias(jnp.maximum(n0, 0), rb_ref, DA_HEADS)
    prev = _bucket_bias(bt + n0, rb_ref, DA_HEADS)
    for h in range(DA_HEADS):
        tiles_ref[0, h] = jnp.where(n0 >= 0, diag[h], NEG_BIG)
        tiles_ref[1, h] = prev[h]
    page_rows = dec_ref.shape[0] - 8
    rr = lax.broadcasted_iota(jnp.int32, dec_ref.shape, 0)
    ll = lax.broadcasted_iota(jnp.int32, dec_ref.shape, 1)
    nd = jnp.where(rr < page_rows, page_rows // DA_HEADS - rr // DA_HEADS, 0)
    dec = _bucket_bias(nd, rb_ref, DA_HEADS)
    acc = jnp.zeros(dec_ref.shape, F32)
    for h in range(DA_HEADS):
        acc = jnp.where(ll // 2 == h, dec[h], acc)
    dec_ref[...] = acc
    lbi = lb_in_ref[...]
    mx = jnp.max(lbi, axis=0, keepdims=True)
    e = jnp.exp(lbi - mx)
    lb_ref[...] = e[0:1, :] / jnp.sum(e, axis=0, keepdims=True)


def param_tables(rel_bias, lower_bound, page):
    d = lower_bound.shape[1]
    return pl.pallas_call(
        _tables_kernel,
        in_specs=[pl.BlockSpec(memory_space=pltpu.SMEM),
                  pl.BlockSpec(memory_space=pltpu.VMEM)],
        out_specs=[pl.BlockSpec(memory_space=pltpu.VMEM)] * 3,
        out_shape=[jax.ShapeDtypeStruct((2, DA_HEADS, BIAS_T, BIAS_T), F32),
                   jax.ShapeDtypeStruct((page * DA_HEADS + 8, LANES), F32),
                   jax.ShapeDtypeStruct((1, d), F32)],
        compiler_params=pltpu.CompilerParams(vmem_limit_bytes=VMEM_LIMIT),
        name="param_tables",
    )(rel_bias, lower_bound)


def _hgrn_sum_matrices(C):
    levels = int(math.log2(C))
    t = np.arange(C)
    u = t[None, :]
    mats = [u <= t[:, None], u > t[:, None]]
    for lv in range(levels):
        c = 1 << lv
        e = (t // (2 * c)) * (2 * c) + c - 1
        upper = (t > e)[:, None]
        seg = np.where(upper, (u > e[:, None]) & (u <= t[:, None]), (u > t[:, None]) & (u <= e[:, None]))
        mats.append(seg)
    return np.concatenate(mats, axis=0).astype(np.float32), levels


def _split2(x):
    hi = x.astype(BF16)
    lo = (x - hi.astype(F32)).astype(BF16)
    return hi, lo


def _hgrn_chunks(q_ref, f_ref, i_ref, g_ref, lb_ref, gn_ref, mall_ref, o_ref, st_ref, s_scr,
                 *, first, last, C, levels, nsub):
    dk = s_scr.shape[-1]

    @pl.when(first)
    def _():
        s_scr[...] = jnp.zeros_like(s_scr)

    lb = lb_ref[...]
    gn = gn_ref[...]
    mall = mall_ref[...]
    ti = lax.broadcasted_iota(jnp.int32, (C, C), 0)
    si = lax.broadcasted_iota(jnp.int32, (C, C), 1)
    txs = jnp.bitwise_xor(ti, si)
    lower = ti > si
    valid = [jnp.logical_and(jnp.right_shift(txs, lv) == 1, lower) for lv in range(levels)]
    trow = lax.broadcasted_iota(jnp.int32, (C, q_ref.shape[-1]), 0)
    upper = [jnp.bitwise_and(jnp.right_shift(trow, lv), 1) == 1 for lv in range(levels)]

    for sub in range(nsub):
        rows = slice(sub * C, (sub + 1) * C)
        fg_all = lb + (1.0 - lb) * _sigmoid(f_ref[0, rows, :])
        hi, lo = _split2(jnp.log(fg_all) * LOG2E)
        sums = jnp.dot(mall, hi, preferred_element_type=F32)
        ends = sums[0:2 * C] + jnp.dot(mall[0:2 * C], lo, preferred_element_type=F32)

        q = _silu(q_ref[0, rows, :])
        kk = 1.0 - fg_all
        v = i_ref[0, rows, :]
        v_bf = v.astype(BF16)
        b = ends[0:C]
        suf = ends[C:2 * C]
        qe = (q * jnp.exp2(b)).astype(BF16)
        kt = (kk * jnp.exp2(suf)).astype(BF16)
        dec_last = jnp.exp2(b[C - 1:C, :])

        scores = [jnp.zeros((C, C), F32) for _ in range(HG_HEADS)]
        for lv in range(levels):
            e = jnp.exp2(sums[(2 + lv) * C:(3 + lv) * C])
            if (1 << lv) >= SUBLANES:
                role = jnp.concatenate([(q if (r >> lv) & 1 else kk)[r:r + SUBLANES]
                                        for r in range(0, C, SUBLANES)], axis=0)
            else:
                role = jnp.where(upper[lv], q, kk)
            x_bf = (role * e).astype(BF16)
            for h in range(HG_HEADS):
                hs = slice(h * dk, (h + 1) * dk)
                sl = lax.dot_general(x_bf[:, hs], x_bf[:, hs], _NT, preferred_element_type=F32)
                scores[h] = jnp.where(valid[lv], sl, scores[h])

        qk = q * kk
        g = g_ref[0, rows, :]
        for h in range(HG_HEADS):
            hs = slice(h * dk, (h + 1) * dk)
            st = s_scr[h]
            diag = jnp.sum(qk[:, hs], axis=-1, keepdims=True)
            o = (jnp.dot(scores[h].astype(BF16), v_bf[:, hs], preferred_element_type=F32)
                 + diag * v[:, hs]
                 + lax.dot_general(qe[:, hs], st.astype(BF16), _NT, preferred_element_type=F32))
            s_scr[h] = st * dec_last[:, hs] + lax.dot_general(v_bf[:, hs], kt[:, hs], _TN,
                                                              preferred_element_type=F32)
            o_ref[0, rows, hs] = (_rms(o, gn[:, hs]) * _silu(g[:, hs])).astype(o_ref.dtype)

    @pl.when(last)
    def _():
        for h in range(HG_HEADS):
            st_ref[0, 0, h] = s_scr[h].T


def _hgrn_prompt_kernel(*refs, C, levels, nsub):
    cidx = pl.program_id(1)
    _hgrn_chunks(*refs, first=cidx == 0, last=cidx == pl.num_programs(1) - 1, C=C, levels=levels, nsub=nsub)


def hgrn_prompt(q, f, i, g, lb, gnorm, *, C, nsub):
    bsz, s, d = q.shape
    dk = d // HG_HEADS
    mall_np, levels = _hgrn_sum_matrices(C)
    mall = jnp.asarray(mall_np, dtype=BF16)
    tb = C * nsub
    blk = pl.BlockSpec((1, tb, d), lambda b, c: (b, c, 0))
    vec = pl.BlockSpec((1, d), lambda b, c: (0, 0))
    return pl.pallas_call(
        functools.partial(_hgrn_prompt_kernel, C=C, levels=levels, nsub=nsub),
        grid=(bsz, s // tb),
        in_specs=[blk, blk, blk, blk, vec, vec,
                  pl.BlockSpec(mall.shape, lambda b, c: (0, 0))],
        out_specs=[pl.BlockSpec((1, tb, d), lambda b, c: (b, c, 0)),
                   pl.BlockSpec((1, 1, HG_HEADS, dk, dk), lambda b, c: (0, b, 0, 0, 0))],
        out_shape=[jax.ShapeDtypeStruct((bsz, s, d), BF16),
                   jax.ShapeDtypeStruct((1, bsz, HG_HEADS, dk, dk), F32)],
        scratch_shapes=[pltpu.VMEM((HG_HEADS, dk, dk), F32)],
        compiler_params=_cparams(("parallel", "arbitrary")),
        name="hgrn_prompt",
    )(q, f, i, g, lb, gnorm.reshape(1, d), mall)


def _hgrn_step_kernel(q_ref, f_ref, i_ref, g_ref, lb_ref, gn_ref, s_ref, o_ref, so_ref):
    rows = q_ref.shape[0]
    lb = lb_ref[...]
    fg = lb + (1.0 - lb) * _sigmoid(f_ref[...])
    q_t = _silu(q_ref[...]).T
    fg_t = fg.T
    kk_t = (1.0 - fg).T
    v = i_ref[...]
    outs = []
    for r in range(rows):
        bl, h = divmod(r, HG_HEADS)
        s_new = fg_t[:, r:r + 1] * s_ref[0, bl, h] + kk_t[:, r:r + 1] * v[r:r + 1, :]
        so_ref[0, bl, h] = s_new
        outs.append(jnp.sum(q_t[:, r:r + 1] * s_new, axis=0, keepdims=True))
    o = jnp.concatenate(outs, axis=0)
    o_ref[...] = (_rms(o, gn_ref[...]) * _silu(g_ref[...])).astype(o_ref.dtype)


def hgrn_step(q, f, i, g, lb, gnorm, state, *, bb):
    bsz, d = q.shape
    dk = d // HG_HEADS
    rows = bb * HG_HEADS
    to_rows = lambda a: a.reshape(bsz * HG_HEADS, dk)
    tile = lambda p: jnp.tile(p.reshape(HG_HEADS, dk), (bb, 1))
    rblk = pl.BlockSpec((rows, dk), lambda b: (b, 0))
    pblk = pl.BlockSpec((rows, dk), lambda b: (0, 0))
    sblk = pl.BlockSpec((1, bb, HG_HEADS, dk, dk), lambda b: (0, b, 0, 0, 0))
    o, s_new = pl.pallas_call(
        _hgrn_step_kernel,
        grid=(bsz // bb,),
        in_specs=[rblk, rblk, rblk, rblk, pblk, pblk, sblk],
        out_specs=[rblk, sblk],
        out_shape=[jax.ShapeDtypeStruct((bsz * HG_HEADS, dk), BF16),
                   jax.ShapeDtypeStruct(state.shape, F32)],
        compiler_params=_cparams(("parallel",)),
        name="hgrn_step",
    )(to_rows(q), to_rows(f), to_rows(i), to_rows(g), tile(lb), tile(gnorm), state)
    return o.reshape(bsz, d), s_new


def _lambda(lq1_ref, lk1_ref, lq2_ref, lk2_ref, lam_init):
    s1 = jnp.sum(lq1_ref[...] * lk1_ref[...], axis=-1, keepdims=True)
    s2 = jnp.sum(lq2_ref[...] * lk2_ref[...], axis=-1, keepdims=True)
    return jnp.exp(s1) - jnp.exp(s2) + lam_init


def _attn_prompt_kernel(qi_ref, kj_ref, var_ref, q_ref, k_ref, vp_ref, vd_ref, tiles_ref,
                        lq1_ref, lk1_ref, lq2_ref, lk2_ref, sub_ref, o_ref,
                        m_scr, l_scr, acc_scr, p_scr, a_scr, *, lam_init, variants):
    step = pl.program_id(1)
    j = kj_ref[step]
    var = var_ref[step]
    tq = q_ref.shape[1]
    tk = k_ref.shape[1]
    bt = tiles_ref.shape[-1]
    dv = acc_scr.shape[-1]
    dh = dv // 2

    @pl.when(j == 0)
    def _():
        m_scr[...] = jnp.full_like(m_scr, NEG_BIG)
        l_scr[...] = jnp.zeros_like(l_scr)

    @pl.when(step == 0)
    def _():
        acc_scr[...] = jnp.zeros_like(acc_scr)
        p_scr[...] = jnp.zeros_like(p_scr)
        a_scr[...] = jnp.zeros_like(a_scr)

    def add_bias(h, s, base):
        rows = []
        for r in range(tq // bt):
            cols = []
            for c in range(s.shape[1] // bt):
                dist = base + r - c
                blk = s[r * bt:(r + 1) * bt, c * bt:(c + 1) * bt]
                if dist < 0:
                    blk = jnp.full((bt, bt), NEG_BIG, F32)
                elif dist <= 1:
                    blk = blk + tiles_ref[dist, h]
                cols.append(blk)
            rows.append(jnp.concatenate(cols, axis=1))
        return jnp.concatenate(rows, axis=0)

    def product(idx, vh):
        kw = vh.shape[0]
        p = p_scr[idx] if kw == tk else p_scr[idx, :, 0:kw]
        acc_scr[idx] = (_lane_tile(a_scr[idx], dv // LANES) * acc_scr[idx]
                        + jnp.dot(p, vh, preferred_element_type=F32))

    def logits(idx, qh, kh, base):
        kw = kh.shape[0]
        s = lax.dot_general(qh, kh, _NT, preferred_element_type=F32)
        if base is not None:
            s = add_bias(idx // 2, s, base)
        m_prev = m_scr[idx]
        m_new = jnp.maximum(m_prev, jnp.max(s, axis=-1, keepdims=True))
        alpha = jnp.exp2(m_prev - m_new)
        chunks = [jnp.exp2(s[:, t * LANES:(t + 1) * LANES] - m_new) for t in range(kw // LANES)]
        psum = chunks[0]
        for ch in chunks[1:]:
            psum = psum + ch
        p = jnp.concatenate([ch.astype(BF16) for ch in chunks], axis=1)
        if kw == tk:
            p_scr[idx] = p
        else:
            p_scr[idx, :, 0:kw] = p
        a_scr[idx] = alpha
        l_scr[idx] = alpha * l_scr[idx] + psum
        m_scr[idx] = m_new

    def sweep(base, kw, last):
        q = q_ref[0]
        k = k_ref[0]
        vp = vp_ref[0]
        for h in range(DA_HEADS):
            for c in range(2):
                idx = 2 * h + c
                lo = h * dv + c * dh
                product(idx, vp[:, h * dv:(h + 1) * dv])
                logits(idx, q[:, lo:lo + dh], k[0:kw, lo:lo + dh], base)
        if not last:
            return
        vd = vd_ref[0]
        for idx in range(2 * DA_HEADS):
            product(idx, vd[0:kw, (idx // 2) * dv:(idx // 2 + 1) * dv])
        lam = _lambda(lq1_ref, lk1_ref, lq2_ref, lk2_ref, lam_init)
        sub = sub_ref[...]
        for h in range(DA_HEADS):
            inv1 = 1.0 / jnp.sum(l_scr[2 * h], axis=-1, keepdims=True)
            inv2 = 1.0 / jnp.sum(l_scr[2 * h + 1], axis=-1, keepdims=True)
            o = acc_scr[2 * h] * inv1 - lam * (acc_scr[2 * h + 1] * inv2)
            o_ref[0, :, h * dv:(h + 1) * dv] = (_rms(o, sub) * (1.0 - lam_init)).astype(o_ref.dtype)

    for vid, (base, kw, last) in enumerate(variants):
        pl.when(var == vid)(functools.partial(sweep, base, kw, last))

    @pl.when(j == 0)
    def _():
        acc_scr[...] = jnp.zeros_like(acc_scr)


def attn_prompt(q, k, v, tiles, lams, subln, *, tq, tk, lam_init):
    bsz, s, d = q.shape
    bt = tiles.shape[-1]
    assert tq % bt == 0 and tk % bt == 0 and s % tq == 0 and s % tk == 0
    dv = d // DA_HEADS
    qi, kj, var, variants = [], [], [], []
    for i in range(s // tq):
        jlast = ((i + 1) * tq - 1) // tk
        for j in range(jlast + 1):
            base = (tq // bt) * i - (tk // bt) * j
            kw = min(tk, (i + 1) * tq - j * tk)
            key = (base if base - (kw // bt - 1) <= 1 else None, kw, j == jlast)
            if key not in variants:
                variants.append(key)
            qi.append(i)
            kj.append(j)
            var.append(variants.index(key))
    n_steps = len(qi)
    qi, kj, var = (jnp.asarray(a, jnp.int32) for a in (qi, kj, var))
    dl = lams[0].shape[-1]
    const = lambda shape: pl.BlockSpec(shape, lambda b, st, qi, kj, var: (0,) * len(shape))
    grid_spec = pltpu.PrefetchScalarGridSpec(
        num_scalar_prefetch=3,
        grid=(bsz, n_steps),
        in_specs=[pl.BlockSpec((1, tq, d), lambda b, st, qi, kj, var: (b, qi[st], 0)),
                  pl.BlockSpec((1, tk, d), lambda b, st, qi, kj, var: (b, kj[st], 0)),
                  pl.BlockSpec((1, tk, d), lambda b, st, qi, kj, var: (b, jnp.maximum(kj[st] - 1, 0), 0)),
                  pl.BlockSpec((1, tk, d), lambda b, st, qi, kj, var: (b, ((qi[st] + 1) * tq - 1) // tk, 0)),
                  const(tiles.shape), const((1, dl)), const((1, dl)), const((1, dl)), const((1, dl)),
                  const((1, dv))],
        out_specs=pl.BlockSpec((1, tq, d), lambda b, st, qi, kj, var: (b, qi[st], 0)),
        scratch_shapes=[pltpu.VMEM((2 * DA_HEADS, tq, LANES), F32),
                        pltpu.VMEM((2 * DA_HEADS, tq, LANES), F32),
                        pltpu.VMEM((2 * DA_HEADS, tq, dv), F32),
                        pltpu.VMEM((2 * DA_HEADS, tq, tk), BF16),
                        pltpu.VMEM((2 * DA_HEADS, tq, LANES), F32)])
    return pl.pallas_call(
        functools.partial(_attn_prompt_kernel, lam_init=lam_init, variants=tuple(variants)),
        grid_spec=grid_spec,
        out_shape=jax.ShapeDtypeStruct((bsz, s, d), BF16),
        compiler_params=_cparams(("parallel", "arbitrary")),
        name="attn_prompt",
    )(qi, kj, var, q, k, v, v, tiles, *lams, subln.reshape(1, dv))


def _decode_fetch(b, pt_ref, ck_hbm, cv_hbm, kbuf, vbuf, sem, *, n_rows):
    bufs, n_pages = kbuf.shape[0], kbuf.shape[1]

    def page_copies(row, slot):
        cps = []
        for r in range(n_pages):
            pg = pt_ref[row, r]
            cps.append(pltpu.make_async_copy(ck_hbm.at[pg], kbuf.at[slot, r], sem.at[slot]))
            cps.append(pltpu.make_async_copy(cv_hbm.at[pg], vbuf.at[slot, r], sem.at[slot]))
        return cps

    @pl.when(b == 0)
    def _():
        for row in range(min(bufs - 1, n_rows)):
            for cp in page_copies(row, row):
                cp.start()

    @pl.when(b + (bufs - 1) < n_rows)
    def _():
        nxt = b + (bufs - 1)
        for cp in page_copies(nxt, nxt % bufs):
            cp.start()

    slot = b % bufs
    return ([kbuf.at[slot, r] for r in range(n_pages)], [vbuf.at[slot, r] for r in range(n_pages)],
            page_copies(b, slot))


def _decode_row(k_refs, v_refs, q_ref, kn_ref, vn_ref, dbias_ref, lq1_ref, lk1_ref, lq2_ref, lk2_ref, sub_ref,
                o_ref, *, lam_init):
    n_pages = len(k_refs)
    d = q_ref.shape[-1]
    dv = d // DA_HEADS
    dh = dv // 2
    page = k_refs[0].shape[0]
    rows = page * DA_HEADS

    def rows2d(ref):
        return ref[...].reshape(rows, dv).astype(BF16)

    def head_rows(x):
        return jnp.concatenate([x[:, h * dv:(h + 1) * dv] for h in range(DA_HEADS)]
                               + [jnp.zeros((8 - DA_HEADS, dv), x.dtype)], axis=0)

    def own_head(n, n_valid):
        r = lax.broadcasted_iota(jnp.int32, (n, LANES), 0)
        j = lax.broadcasted_iota(jnp.int32, (n, LANES), 1)
        return jnp.logical_and(r % DA_HEADS == j // 2, jnp.logical_and(j < 2 * DA_HEADS, r < n_valid))

    q = q_ref[0].astype(F32)
    jrow = lax.broadcasted_iota(jnp.int32, (LANES, dv), 0)
    lane = lax.broadcasted_iota(jnp.int32, (LANES, dv), 1)
    qmat = jnp.zeros((LANES, dv), F32)
    for h in range(DA_HEADS):
        qh = jnp.broadcast_to(q[:, h * dv:(h + 1) * dv], (LANES, dv))
        qmat = jnp.where(jnp.logical_and(jrow // 2 == h, lane // dh == jrow % 2), qh, qmat)
    qmat = qmat.astype(BF16)

    eye = (lax.broadcasted_iota(jnp.int32, (LANES, LANES), 0) == lax.broadcasted_iota(jnp.int32, (LANES, LANES), 1))

    def to_rows(x):
        return jnp.sum(jnp.where(eye, jnp.broadcast_to(x, (LANES, LANES)), 0.0), axis=1, keepdims=True)

    kn8 = head_rows(kn_ref[0]).astype(BF16)
    s_new = lax.dot_general(kn8, qmat, _NT, preferred_element_type=F32) + dbias_ref[rows:rows + 8, :]
    s_new = jnp.where(own_head(8, DA_HEADS), s_new, NEG_BIG)
    m = jnp.max(s_new, axis=0, keepdims=True)
    p_new = jnp.exp2(s_new - m)
    lsum = jnp.sum(p_new, axis=0, keepdims=True)
    acc = lax.dot_general(p_new.astype(BF16), head_rows(vn_ref[0]).astype(BF16), _TN, preferred_element_type=F32)

    valid = own_head(rows, rows)
    group = min(8, n_pages)
    assert n_pages % group == 0
    for g0 in range(0, n_pages, group):
        s_grp = []
        for r in range(g0, g0 + group):
            s = lax.dot_general(rows2d(k_refs[r]), qmat, _NT, preferred_element_type=F32)
            if r == n_pages - 1:
                s = s + dbias_ref[0:rows, :]
            s_grp.append(jnp.where(valid, s, NEG_BIG))
        smax = s_grp[0]
        for s in s_grp[1:]:
            smax = jnp.maximum(smax, s)
        m_new = jnp.maximum(m, jnp.max(smax, axis=0, keepdims=True))
        alpha = jnp.exp2(m - m_new)
        psum = jnp.zeros((rows, LANES), F32)
        acc_g = jnp.zeros(acc.shape, F32)
        for r in range(g0, g0 + group):
            p = jnp.exp2(s_grp[r - g0] - m_new)
            psum = psum + p
            acc_g = acc_g + lax.dot_general(p.astype(BF16), rows2d(v_refs[r]), _TN, preferred_element_type=F32)
        lsum = alpha * lsum + jnp.sum(psum, axis=0, keepdims=True)
        acc = to_rows(alpha) * acc + acc_g
        m = m_new
    an = acc * to_rows(1.0 / lsum)
    lam = _lambda(lq1_ref, lk1_ref, lq2_ref, lk2_ref, lam_init)
    sub = sub_ref[...]
    for h in range(DA_HEADS):
        o = an[2 * h:2 * h + 1, :] - lam * an[2 * h + 1:2 * h + 2, :]
        o_ref[0, :, h * dv:(h + 1) * dv] = (_rms(o, sub) * (1.0 - lam_init)).astype(o_ref.dtype)


def _attn_decode_kernel(pt_ref, q_ref, kn_ref, vn_ref, ck_hbm, cv_hbm, dbias_ref,
                        lq1_ref, lk1_ref, lq2_ref, lk2_ref, sub_ref, o_ref, kbuf, vbuf, sem, *, n_rows, lam_init):
    k_refs, v_refs, pending = _decode_fetch(pl.program_id(0), pt_ref, ck_hbm, cv_hbm, kbuf, vbuf, sem,
                                            n_rows=n_rows)
    for cp in pending:
        cp.wait()
    _decode_row(k_refs, v_refs, q_ref, kn_ref, vn_ref, dbias_ref, lq1_ref, lk1_ref, lq2_ref, lk2_ref, sub_ref,
                o_ref, lam_init=lam_init)


def attn_decode(q, k_new, v_new, cache_k, cache_v, page_table, dbias, lams, subln, *, lam_init):
    bsz, d = q.shape
    n_pages = page_table.shape[1]
    page = cache_k.shape[1]
    assert page == LANES
    dv = d // DA_HEADS
    dl = lams[0].shape[-1]
    row_spec = pl.BlockSpec((1, 1, d), lambda b, pt: (b, 0, 0))
    hbm_spec = pl.BlockSpec(memory_space=pl.ANY)
    lam_spec = pl.BlockSpec((1, dl), lambda b, pt: (0, 0))
    page_buf = pltpu.VMEM((DECODE_BUFS, n_pages, page, DA_HEADS, dv), cache_k.dtype)
    grid_spec = pltpu.PrefetchScalarGridSpec(
        num_scalar_prefetch=1,
        grid=(bsz,),
        in_specs=[row_spec, row_spec, row_spec, hbm_spec, hbm_spec,
                  pl.BlockSpec(dbias.shape, lambda b, pt: (0, 0)),
                  lam_spec, lam_spec, lam_spec, lam_spec,
                  pl.BlockSpec((1, dv), lambda b, pt: (0, 0))],
        out_specs=pl.BlockSpec((1, 1, d), lambda b, pt: (b, 0, 0)),
        scratch_shapes=[page_buf, page_buf, pltpu.SemaphoreType.DMA((DECODE_BUFS,))])
    r3 = lambda a: a.reshape(bsz, 1, d)
    out = pl.pallas_call(
        functools.partial(_attn_decode_kernel, n_rows=bsz, lam_init=lam_init),
        grid_spec=grid_spec,
        out_shape=jax.ShapeDtypeStruct((bsz, 1, d), BF16),
        compiler_params=_cparams(("arbitrary",)),
        name="attn_decode",
    )(page_table, r3(q), r3(k_new), r3(v_new), cache_k, cache_v, dbias, *lams, subln.reshape(1, dv))
    return out.reshape(bsz, d)


def _hgrn_decode_kernel(pt_ref, q_ref, f_ref, i_ref, g_ref, lb_ref, gn_ref, mall_ref,
                        dq_ref, kn_ref, vn_ref, ck_hbm, cv_hbm, dbias_ref, lq1_ref, lk1_ref, lq2_ref, lk2_ref,
                        sub_ref, o_ref, st_ref, do_ref, s_scr, kbuf, vbuf, sem,
                        *, C, levels, nsub, n_rows, lam_init):
    cidx = pl.program_id(1)
    row = pl.program_id(0) * pl.num_programs(1) + cidx
    k_refs, v_refs, pending = _decode_fetch(row, pt_ref, ck_hbm, cv_hbm, kbuf, vbuf, sem, n_rows=n_rows)
    _hgrn_chunks(q_ref, f_ref, i_ref, g_ref, lb_ref, gn_ref, mall_ref, o_ref, st_ref, s_scr,
                 first=cidx == 0, last=cidx == pl.num_programs(1) - 1, C=C, levels=levels, nsub=nsub)
    for cp in pending:
        cp.wait()
    _decode_row(k_refs, v_refs, dq_ref, kn_ref, vn_ref, dbias_ref, lq1_ref, lk1_ref, lq2_ref, lk2_ref, sub_ref,
                do_ref, lam_init=lam_init)


def hgrn_prompt_decode(q, f, i, g, lb, gnorm, dq, k_new, v_new, cache_k, cache_v, page_table, dbias, lams, subln,
                       *, C, nsub, lam_init):
    bsz, s, d = q.shape
    n_rows = dq.shape[0]
    dk = d // HG_HEADS
    dv = d // DA_HEADS
    tb = C * nsub
    steps = s // tb
    assert bsz * steps == n_rows
    n_pages = page_table.shape[1]
    page = cache_k.shape[1]
    mall_np, levels = _hgrn_sum_matrices(C)
    mall = jnp.asarray(mall_np, dtype=BF16)
    dl = lams[0].shape[-1]
    blk = pl.BlockSpec((1, tb, d), lambda b, c, pt: (b, c, 0))
    const = lambda shape: pl.BlockSpec(shape, lambda b, c, pt: (0,) * len(shape))
    row_spec = pl.BlockSpec((1, 1, d), lambda b, c, pt: (b * steps + c, 0, 0))
    hbm_spec = pl.BlockSpec(memory_space=pl.ANY)
    page_buf = pltpu.VMEM((2, n_pages, page, DA_HEADS, dv), cache_k.dtype)
    grid_spec = pltpu.PrefetchScalarGridSpec(
        num_scalar_prefetch=1,
        grid=(bsz, steps),
        in_specs=[blk, blk, blk, blk, const((1, d)), const((1, d)), const(mall.shape),
                  row_spec, row_spec, row_spec, hbm_spec, hbm_spec, const(dbias.shape),
                  const((1, dl)), const((1, dl)), const((1, dl)), const((1, dl)), const((1, dv))],
        out_specs=[pl.BlockSpec((1, tb, d), lambda b, c, pt: (b, c, 0)),
                   pl.BlockSpec((1, 1, HG_HEADS, dk, dk), lambda b, c, pt: (0, b, 0, 0, 0)),
                   row_spec],
        scratch_shapes=[pltpu.VMEM((HG_HEADS, dk, dk), F32), page_buf, page_buf,
                        pltpu.SemaphoreType.DMA((2,))])
    r3 = lambda a: a.reshape(n_rows, 1, d)
    o, state, o_dec = pl.pallas_call(
        functools.partial(_hgrn_decode_kernel, C=C, levels=levels, nsub=nsub, n_rows=n_rows, lam_init=lam_init),
        grid_spec=grid_spec,
        out_shape=[jax.ShapeDtypeStruct((bsz, s, d), BF16),
                   jax.ShapeDtypeStruct((1, bsz, HG_HEADS, dk, dk), F32),
                   jax.ShapeDtypeStruct((n_rows, 1, d), BF16)],
        compiler_params=_cparams(("arbitrary", "arbitrary")),
        name="hgrn_prompt_decode",
    )(page_table, q, f, i, g, lb, gnorm.reshape(1, d), mall, r3(dq), r3(k_new), r3(v_new), cache_k, cache_v,
      dbias, *lams, subln.reshape(1, dv))
    return o, state, o_dec.reshape(n_rows, d)


DECODE_BUFS = 3
ATTN_TQ = 512
ATTN_TK = 512
HGRN_C = 64
HGRN_NSUB = 8
TM = 512


def _in_proj(x, p):
    return proj(x, [(p["norm_mix"][0], (p["w_in_a"], 0), 1.0, [[(F32, 0)]] * 4)], tm=TM, name="in_proj")


def _after_mixer0(o, x, p, scale):
    x = mix_ffn(o, x, p["w_out_a"], p["norm_ffn"][0], p["w_gate_up"], p["w_down"], p["norm_final"],
                layer=0, tm=TM, final_norm=False, name="mix_ffn0")
    k32, k16, v32, v16, qa = proj(
        x, [(p["kv_norm"], (p["w_kv"], 0), 1.0, [[(F32, DA_HEADS), (BF16, 0)]] * 2),
            (p["norm_mix"][1], (p["w_q_b"], 0), scale * LOG2E, [[(BF16, 0)]])],
        tm=TM, name="kvq_proj")
    return x, k32, k16, v32, v16, qa


def _after_mixer1(o, x, p):
    return mix_ffn(o, x, p["w_out_b"], p["norm_ffn"][1], p["w_gate_up"], p["w_down"], p["norm_final"],
                   layer=1, tm=TM, final_norm=True, name="mix_ffn1")


def kernel(x_prompt, x_sample, cache_k, cache_v, state_hgrn, page_table, w_in_a, lower_bound, gnorm_a, w_out_a,
           w_q_b, lambda_q1, lambda_k1, lambda_q2, lambda_k2, subln_b, w_out_b, kv_norm, w_kv, rel_bias,
           norm_mix, norm_ffn, w_gate_up, w_down, norm_final):
    bf = lambda w: (w if w.ndim == 3 else w[None]).astype(BF16)
    p = dict(w_in_a=bf(w_in_a), w_out_a=bf(w_out_a), w_q_b=bf(w_q_b), w_out_b=bf(w_out_b), kv_norm=kv_norm,
             w_kv=bf(w_kv), norm_mix=norm_mix, norm_ffn=norm_ffn, w_gate_up=bf(w_gate_up), w_down=bf(w_down),
             norm_final=norm_final)
    bsz, s, d = x_prompt.shape
    nb = x_sample.shape[0]
    dv = d // DA_HEADS
    scale = (dv // 2) ** -0.5
    lam_init = 0.8 - 0.6 * math.exp(-0.3 * 1)
    lams = [a[0].reshape(1, -1) for a in (lambda_q1, lambda_k1, lambda_q2, lambda_k2)]
    gnorm, subln = gnorm_a[0], subln_b[0]
    tiles, dbias, lb = param_tables(rel_bias, lower_bound, cache_k.shape[1])
    xp = x_prompt.reshape(bsz * s, d)
    xs = x_sample.reshape(nb, d)
    r3 = lambda a: a.reshape(bsz, s, d)

    q, f, i, g = _in_proj(xs, p)
    o_s, st_s = hgrn_step(q, f, i, g, lb, gnorm, state_hgrn, bb=16)
    xs, k_s, _, v_s, _, qa_s = _after_mixer0(o_s, xs, p, scale)

    q, f, i, g = _in_proj(xp, p)
    tokens_per_row = (bsz * s) // nb
    if (bsz * s) % nb == 0 and tokens_per_row % HGRN_C == 0 and s % tokens_per_row == 0:
        o_p, st_p, o_dec = hgrn_prompt_decode(
            r3(q), r3(f), r3(i), r3(g), lb, gnorm, qa_s, k_s.reshape(nb, d), v_s.reshape(nb, d), cache_k, cache_v,
            page_table, dbias, lams, subln, C=HGRN_C, nsub=tokens_per_row // HGRN_C, lam_init=lam_init)
    else:
        o_p, st_p = hgrn_prompt(r3(q), r3(f), r3(i), r3(g), lb, gnorm, C=HGRN_C, nsub=HGRN_NSUB)
        o_dec = attn_decode(qa_s, k_s.reshape(nb, d), v_s.reshape(nb, d), cache_k, cache_v, page_table, dbias,
                            lams, subln, lam_init=lam_init)
    xp, k_p, k16, v_p, v16, qa_p = _after_mixer0(o_p.reshape(bsz * s, d), xp, p, scale)

    o_a = attn_prompt(r3(qa_p), r3(k16), r3(v16), tiles, lams, subln,
                      tq=min(ATTN_TQ, s), tk=min(ATTN_TK, s), lam_init=lam_init)
    y_p = _after_mixer1(o_a.reshape(bsz * s, d), xp, p)
    y_s = _after_mixer1(o_dec, xs, p)
    hk = DA_HEADS, dv
    return (y_p.reshape(bsz, s, d), y_s.reshape(nb, 1, d),
            k_p.reshape(bsz, s, *hk), v_p.reshape(bsz, s, *hk), st_p,
            k_s.reshape(nb, 1, *hk), v_s.reshape(nb, 1, *hk), st_s)
```

```python
import functools
import math

import numpy as np
import jax
import jax.numpy as jnp
from jax import lax
from jax.experimental import pallas as pl
from jax.experimental.pallas import tpu as pltpu

F32 = jnp.float32
BF16 = jnp.bfloat16
EPS = 1e-6
NEG_BIG = -1e30
LOG2E = math.log2(math.e)

HG_HEADS = 8
DA_HEADS = 4
N_BUCKETS = 32
MAX_DISTANCE = 128
LANES = 128
SUBLANES = 8
MXU_N = 256
VMEM_LIMIT = 56 * 1024 * 1024

_NT = (((1,), (1,)), ((), ()))
_TN = (((0,), (0,)), ((), ()))


def _cparams(sem):
    return pltpu.CompilerParams(dimension_semantics=sem, vmem_limit_bytes=VMEM_LIMIT)


def _resident(shape):
    return pl.BlockSpec(shape, lambda *_: (0,) * len(shape), pipeline_mode=pl.Buffered(1))


def _resident_layer(w, layer):
    return pl.BlockSpec((1,) + w.shape[1:], lambda *_: (layer, 0, 0), pipeline_mode=pl.Buffered(1))


def _sigmoid(x):
    return 1.0 / (1.0 + jnp.exp(-x))


def _silu(x):
    return x * _sigmoid(x)


def _rms(x, g):
    ms = jnp.mean(x * x, axis=-1, keepdims=True)
    return x * lax.rsqrt(ms + EPS) * g


def _lane_tile(x, reps):
    return x if reps == 1 else jnp.concatenate([x] * reps, axis=1)


def _proj_kernel(x_ref, *refs, branches):
    nb = len(branches)
    g_refs = refs[:nb]
    w_refs = refs[nb:2 * nb]
    out_refs = refs[2 * nb:]
    x = x_ref[...]
    inv = lax.rsqrt(jnp.mean(x * x, axis=-1, keepdims=True) + EPS)
    o = 0
    for bi, (scale, groups) in enumerate(branches):
        xn = (x * inv * g_refs[bi][...]).astype(BF16)
        ng = w_refs[bi].shape[2] // len(groups)
        for gi, outs in enumerate(groups):
            acc = jnp.dot(xn, w_refs[bi][0, :, gi * ng:(gi + 1) * ng], preferred_element_type=F32)
            if scale != 1.0:
                acc = acc * scale
            for _ in outs:
                ref = out_refs[o]
                if len(ref.shape) == 3:
                    hd = ref.shape[2]
                    for hh in range(ref.shape[1]):
                        ref[:, hh, :] = acc[:, hh * hd:(hh + 1) * hd].astype(ref.dtype)
                else:
                    ref[...] = acc.astype(ref.dtype)
                o += 1


def proj(x, branches, *, tm, name):
    m, d = x.shape
    tm = min(tm, m)
    gains = [b[0].reshape(1, d) for b in branches]
    weights = [b[1] for b in branches]
    out_shape, out_specs = [], []
    for _, (w, _), _, groups in branches:
        ng = w.shape[2] // len(groups)
        for outs in groups:
            for dt, heads in outs:
                if heads:
                    out_shape.append(jax.ShapeDtypeStruct((m, heads, ng // heads), dt))
                    out_specs.append(pl.BlockSpec((tm, heads, ng // heads), lambda i: (i, 0, 0)))
                else:
                    out_shape.append(jax.ShapeDtypeStruct((m, ng), dt))
                    out_specs.append(pl.BlockSpec((tm, ng), lambda i: (i, 0)))
    return pl.pallas_call(
        functools.partial(_proj_kernel, branches=[(b[2], b[3]) for b in branches]),
        grid=(m // tm,),
        in_specs=([pl.BlockSpec((tm, d), lambda i: (i, 0))]
                  + [_resident(g.shape) for g in gains] + [_resident_layer(w, l) for w, l in weights]),
        out_specs=out_specs,
        out_shape=out_shape,
        compiler_params=_cparams(("parallel",)),
        name=name,
    )(x, *gains, *[w for w, _ in weights])


def _mix_ffn_kernel(a_ref, x_ref, wo_ref, g_ref, wgu_ref, wd_ref, gf_ref, o_ref, x1_scr, hid_scr, *, final_norm):
    dff = wd_ref.shape[1]
    x1 = x_ref[...] + jnp.dot(a_ref[...], wo_ref[0], preferred_element_type=F32)
    x1_scr[...] = x1
    xn = _rms(x1, g_ref[...]).astype(BF16)
    for j in range(dff // MXU_N):
        cs = slice(j * MXU_N, (j + 1) * MXU_N)
        gt = jnp.dot(xn, wgu_ref[0, :, cs], preferred_element_type=F32)
        ut = jnp.dot(xn, wgu_ref[0, :, dff + j * MXU_N:dff + (j + 1) * MXU_N], preferred_element_type=F32)
        hid_scr[:, cs] = (_silu(gt) * ut).astype(BF16)
    y = x1_scr[...] + jnp.dot(hid_scr[...], wd_ref[0], preferred_element_type=F32)
    if final_norm:
        y = _rms(y, gf_ref[...])
    o_ref[...] = y


def mix_ffn(a, x, w_out, g, w_gate_up, w_down, g_final, *, layer, tm, final_norm, name):
    m, d = x.shape
    dff = w_down.shape[1]
    assert dff % MXU_N == 0
    tm = min(tm, m)
    row = lambda width: pl.BlockSpec((tm, width), lambda i: (i, 0))
    return pl.pallas_call(
        functools.partial(_mix_ffn_kernel, final_norm=final_norm),
        grid=(m // tm,),
        in_specs=[row(a.shape[1]), row(d), _resident_layer(w_out, 0), _resident((1, d)),
                  _resident_layer(w_gate_up, layer), _resident_layer(w_down, layer), _resident((1, d))],
        out_specs=row(d),
        out_shape=jax.ShapeDtypeStruct((m, d), F32),
        scratch_shapes=[pltpu.VMEM((tm, d), F32), pltpu.VMEM((tm, dff), BF16)],
        compiler_params=_cparams(("parallel",)),
        name=name,
    )(a, x, w_out, g.reshape(1, d), w_gate_up, w_down, g_final.reshape(1, d))


BIAS_T = 2 * MAX_DISTANCE


def _bucket_starts():
    max_exact = N_BUCKETS // 2
    n = np.arange(2 * MAX_DISTANCE, dtype=np.int32)
    nf = np.maximum(n, 1).astype(np.float32)
    large = max_exact + (np.log(nf / np.float32(max_exact)) / np.float32(math.log(MAX_DISTANCE / max_exact))
                         * np.float32(N_BUCKETS - max_exact)).astype(np.int32)
    bucket = np.where(n < max_exact, n, np.minimum(large, N_BUCKETS - 1))
    assert (np.diff(bucket) >= 0).all() and (np.diff(bucket) <= 1).all() and bucket[-1] == N_BUCKETS - 1
    return [int(np.argmax(bucket >= b)) for b in range(N_BUCKETS)]


def _bucket_bias(n, rb_ref, n_heads):
    starts = _bucket_starts()
    reached = [n >= starts[b] for b in range(1, N_BUCKETS - 1)]
    outs = []
    for h in range(n_heads):
        far = rb_ref[N_BUCKETS - 1, h]
        acc = jnp.full(n.shape, (rb_ref[0, h] - far) * LOG2E, F32)
        for b in range(1, N_BUCKETS - 1):
            acc = jnp.where(reached[b - 1], (rb_ref[b, h] - far) * LOG2E, acc)
        outs.append(jnp.where(n >= starts[N_BUCKETS - 1], 0.0, acc))
    return outs


def _tables_kernel(rb_ref, lb_in_ref, tiles_ref, dec_ref, lb_ref):
    bt = tiles_ref.shape[-1]
    r = lax.broadcasted_iota(jnp.int32, (bt, bt), 0)
    c = lax.broadcasted_iota(jnp.int32, (bt, bt), 1)
    n0 = r - c
    diag = _bucket_bias(jnp.maximum(n0, 0), rb_ref, DA_HEADS)
    prev = _bucket_bias(bt + n0, rb_ref, DA_HEADS)
    for h in range(DA_HEADS):
        tiles_ref[0, h] = jnp.where(n0 >= 0, diag[h], NEG_BIG)
        tiles_ref[1, h] = prev[h]
    page_cols = dec_ref.shape[1] - LANES
    rr = lax.broadcasted_iota(jnp.int32, dec_ref.shape, 0)
    cc = lax.broadcasted_iota(jnp.int32, dec_ref.shape, 1)
    nd = jnp.where(cc < page_cols, page_cols // DA_HEADS - cc // DA_HEADS, 0)
    dec = _bucket_bias(nd, rb_ref, DA_HEADS)
    acc = jnp.zeros(dec_ref.shape, F32)
    for h in range(DA_HEADS):
        acc = jnp.where(rr // 2 == h, dec[h], acc)
    dec_ref[...] = acc
    lbi = lb_in_ref[...]
    mx = jnp.max(lbi, axis=0, keepdims=True)
    e = jnp.exp(lbi - mx)
    lb_ref[...] = e[0:1, :] / jnp.sum(e, axis=0, keepdims=True)


def param_tables(rel_bias, lower_bound, page):
    d = lower_bound.shape[1]
    return pl.pallas_call(
        _tables_kernel,
        in_specs=[pl.BlockSpec(memory_space=pltpu.SMEM),
                  pl.BlockSpec(memory_space=pltpu.VMEM)],
        out_specs=[pl.BlockSpec(memory_space=pltpu.VMEM)] * 3,
        out_shape=[jax.ShapeDtypeStruct((2, DA_HEADS, BIAS_T, BIAS_T), F32),
                   jax.ShapeDtypeStruct((2 * DA_HEADS, page * DA_HEADS + LANES), F32),
                   jax.ShapeDtypeStruct((1, d), F32)],
        compiler_params=pltpu.CompilerParams(vmem_limit_bytes=VMEM_LIMIT),
        name="param_tables",
    )(rel_bias, lower_bound)


def _hgrn_sum_matrices(C):
    levels = int(math.log2(C))
    t = np.arange(C)
    u = t[None, :]
    mats = [u <= t[:, None], u > t[:, None]]
    for lv in range(levels):
        c = 1 << lv
        e = (t // (2 * c)) * (2 * c) + c - 1
        upper = (t > e)[:, None]
        seg = np.where(upper, (u > e[:, None]) & (u <= t[:, None]), (u > t[:, None]) & (u <= e[:, None]))
        mats.append(seg)
    return np.concatenate(mats, axis=0).astype(np.float32), levels


def _split2(x):
    hi = x.astype(BF16)
    lo = (x - hi.astype(F32)).astype(BF16)
    return hi, lo


def _hgrn_chunks(q_ref, f_ref, i_ref, g_ref, lb_ref, gn_ref, mall_ref, o_ref, st_ref, s_scr,
                 *, first, last, C, levels, nsub):
    dk = s_scr.shape[-1]

    @pl.when(first)
    def _():
        s_scr[...] = jnp.zeros_like(s_scr)

    lb = lb_ref[...]
    gn = gn_ref[...]
    mall = mall_ref[...]
    ti = lax.broadcasted_iota(jnp.int32, (C, C), 0)
    si = lax.broadcasted_iota(jnp.int32, (C, C), 1)
    txs = jnp.bitwise_xor(ti, si)
    lower = ti > si
    valid = [jnp.logical_and(jnp.right_shift(txs, lv) == 1, lower) for lv in range(levels)]
    trow = lax.broadcasted_iota(jnp.int32, (C, q_ref.shape[-1]), 0)
    upper = [jnp.bitwise_and(jnp.right_shift(trow, lv), 1) == 1 for lv in range(levels)]

    for sub in range(nsub):
        rows = slice(sub * C, (sub + 1) * C)
        fg_all = lb + (1.0 - lb) * _sigmoid(f_ref[0, rows, :])
        hi, lo = _split2(jnp.log(fg_all) * LOG2E)
        sums = jnp.dot(mall, hi, preferred_element_type=F32)
        ends = sums[0:2 * C] + jnp.dot(mall[0:2 * C], lo, preferred_element_type=F32)

        q = _silu(q_ref[0, rows, :])
        kk = 1.0 - fg_all
        v = i_ref[0, rows, :]
        v_bf = v.astype(BF16)
        b = ends[0:C]
        suf = ends[C:2 * C]
        qe = (q * jnp.exp2(b)).astype(BF16)
        kt = (kk * jnp.exp2(suf)).astype(BF16)
        dec_last = jnp.exp2(b[C - 1:C, :])

        scores = [jnp.zeros((C, C), F32) for _ in range(HG_HEADS)]
        for lv in range(levels):
            e = jnp.exp2(sums[(2 + lv) * C:(3 + lv) * C])
            if (1 << lv) >= SUBLANES:
                role = jnp.concatenate([(q if (r >> lv) & 1 else kk)[r:r + SUBLANES]
                                        for r in range(0, C, SUBLANES)], axis=0)
            else:
                role = jnp.where(upper[lv], q, kk)
            x_bf = (role * e).astype(BF16)
            for h in range(HG_HEADS):
                hs = slice(h * dk, (h + 1) * dk)
                sl = lax.dot_general(x_bf[:, hs], x_bf[:, hs], _NT, preferred_element_type=F32)
                scores[h] = jnp.where(valid[lv], sl, scores[h])

        qk = q * kk
        g = g_ref[0, rows, :]
        for h in range(HG_HEADS):
            hs = slice(h * dk, (h + 1) * dk)
            st = s_scr[h]
            diag = jnp.sum(qk[:, hs], axis=-1, keepdims=True)
            o = (jnp.dot(scores[h].astype(BF16), v_bf[:, hs], preferred_element_type=F32)
                 + diag * v[:, hs]
                 + lax.dot_general(qe[:, hs], st.astype(BF16), _NT, preferred_element_type=F32))
            s_scr[h] = st * dec_last[:, hs] + lax.dot_general(v_bf[:, hs], kt[:, hs], _TN,
                                                              preferred_element_type=F32)
            o_ref[0, rows, hs] = (_rms(o, gn[:, hs]) * _silu(g[:, hs])).astype(o_ref.dtype)

    @pl.when(last)
    def _():
        for h in range(HG_HEADS):
            st_ref[0, 0, h] = s_scr[h].T


def _hgrn_prompt_kernel(*refs, C, levels, nsub):
    cidx = pl.program_id(1)
    _hgrn_chunks(*refs, first=cidx == 0, last=cidx == pl.num_programs(1) - 1, C=C, levels=levels, nsub=nsub)


def hgrn_prompt(q, f, i, g, lb, gnorm, *, C, nsub):
    bsz, s, d = q.shape
    dk = d // HG_HEADS
    mall_np, levels = _hgrn_sum_matrices(C)
    mall = jnp.asarray(mall_np, dtype=BF16)
    tb = C * nsub
    blk = pl.BlockSpec((1, tb, d), lambda b, c: (b, c, 0))
    vec = pl.BlockSpec((1, d), lambda b, c: (0, 0))
    return pl.pallas_call(
        functools.partial(_hgrn_prompt_kernel, C=C, levels=levels, nsub=nsub),
        grid=(bsz, s // tb),
        in_specs=[blk, blk, blk, blk, vec, vec,
                  pl.BlockSpec(mall.shape, lambda b, c: (0, 0))],
        out_specs=[pl.BlockSpec((1, tb, d), lambda b, c: (b, c, 0)),
                   pl.BlockSpec((1, 1, HG_HEADS, dk, dk), lambda b, c: (0, b, 0, 0, 0))],
        out_shape=[jax.ShapeDtypeStruct((bsz, s, d), BF16),
                   jax.ShapeDtypeStruct((1, bsz, HG_HEADS, dk, dk), F32)],
        scratch_shapes=[pltpu.VMEM((HG_HEADS, dk, dk), F32)],
        compiler_params=_cparams(("parallel", "arbitrary")),
        name="hgrn_prompt",
    )(q, f, i, g, lb, gnorm.reshape(1, d), mall)


def _hgrn_step_kernel(q_ref, f_ref, i_ref, g_ref, lb_ref, gn_ref, s_ref, o_ref, so_ref):
    rows = q_ref.shape[0]
    lb = lb_ref[...]
    fg = lb + (1.0 - lb) * _sigmoid(f_ref[...])
    q_t = _silu(q_ref[...]).T
    fg_t = fg.T
    kk_t = (1.0 - fg).T
    v = i_ref[...]
    outs = []
    for r in range(rows):
        bl, h = divmod(r, HG_HEADS)
        s_new = fg_t[:, r:r + 1] * s_ref[0, bl, h] + kk_t[:, r:r + 1] * v[r:r + 1, :]
        so_ref[0, bl, h] = s_new
        outs.append(jnp.sum(q_t[:, r:r + 1] * s_new, axis=0, keepdims=True))
    o = jnp.concatenate(outs, axis=0)
    o_ref[...] = (_rms(o, gn_ref[...]) * _silu(g_ref[...])).astype(o_ref.dtype)


def hgrn_step(q, f, i, g, lb, gnorm, state, *, bb):
    bsz, d = q.shape
    dk = d // HG_HEADS
    rows = bb * HG_HEADS
    to_rows = lambda a: a.reshape(bsz * HG_HEADS, dk)
    tile = lambda p: jnp.tile(p.reshape(HG_HEADS, dk), (bb, 1))
    rblk = pl.BlockSpec((rows, dk), lambda b: (b, 0))
    pblk = pl.BlockSpec((rows, dk), lambda b: (0, 0))
    sblk = pl.BlockSpec((1, bb, HG_HEADS, dk, dk), lambda b: (0, b, 0, 0, 0))
    o, s_new = pl.pallas_call(
        _hgrn_step_kernel,
        grid=(bsz // bb,),
        in_specs=[rblk, rblk, rblk, rblk, pblk, pblk, sblk],
        out_specs=[rblk, sblk],
        out_shape=[jax.ShapeDtypeStruct((bsz * HG_HEADS, dk), BF16),
                   jax.ShapeDtypeStruct(state.shape, F32)],
        compiler_params=_cparams(("parallel",)),
        name="hgrn_step",
    )(to_rows(q), to_rows(f), to_rows(i), to_rows(g), tile(lb), tile(gnorm), state)
    return o.reshape(bsz, d), s_new


def _lambda(lq1_ref, lk1_ref, lq2_ref, lk2_ref, lam_init):
    s1 = jnp.sum(lq1_ref[...] * lk1_ref[...], axis=-1, keepdims=True)
    s2 = jnp.sum(lq2_ref[...] * lk2_ref[...], axis=-1, keepdims=True)
    return jnp.exp(s1) - jnp.exp(s2) + lam_init


def _attn_prompt_kernel(qi_ref, kj_ref, var_ref, q_ref, k_ref, vp_ref, vd_ref, tiles_ref,
                        lq1_ref, lk1_ref, lq2_ref, lk2_ref, sub_ref, o_ref,
                        m_scr, l_scr, acc_scr, p_scr, a_scr, *, lam_init, variants):
    step = pl.program_id(1)
    j = kj_ref[step]
    var = var_ref[step]
    tq = q_ref.shape[1]
    tk = k_ref.shape[1]
    bt = tiles_ref.shape[-1]
    dv = acc_scr.shape[-1]
    dh = dv // 2

    @pl.when(j == 0)
    def _():
        m_scr[...] = jnp.full_like(m_scr, NEG_BIG)
        l_scr[...] = jnp.zeros_like(l_scr)

    @pl.when(step == 0)
    def _():
        acc_scr[...] = jnp.zeros_like(acc_scr)
        p_scr[...] = jnp.zeros_like(p_scr)
        a_scr[...] = jnp.zeros_like(a_scr)

    def add_bias(h, s, base):
        rows = []
        for r in range(tq // bt):
            cols = []
            for c in range(s.shape[1] // bt):
                dist = base + r - c
                blk = s[r * bt:(r + 1) * bt, c * bt:(c + 1) * bt]
                if dist < 0:
                    blk = jnp.full((bt, bt), NEG_BIG, F32)
                elif dist <= 1:
                    blk = blk + tiles_ref[dist, h]
                cols.append(blk)
            rows.append(jnp.concatenate(cols, axis=1))
        return jnp.concatenate(rows, axis=0)

    def product(idx, vh):
        kw = vh.shape[0]
        p = p_scr[idx] if kw == tk else p_scr[idx, :, 0:kw]
        acc_scr[idx] = (_lane_tile(a_scr[idx], dv // LANES) * acc_scr[idx]
                        + jnp.dot(p, vh, preferred_element_type=F32))

    def logits(idx, qh, kh, base):
        kw = kh.shape[0]
        s = lax.dot_general(qh, kh, _NT, preferred_element_type=F32)
        if base is not None:
            s = add_bias(idx // 2, s, base)
        m_prev = m_scr[idx]
        m_new = jnp.maximum(m_prev, jnp.max(s, axis=-1, keepdims=True))
        alpha = jnp.exp2(m_prev - m_new)
        chunks = [jnp.exp2(s[:, t * LANES:(t + 1) * LANES] - m_new) for t in range(kw // LANES)]
        psum = chunks[0]
        for ch in chunks[1:]:
            psum = psum + ch
        p = jnp.concatenate([ch.astype(BF16) for ch in chunks], axis=1)
        if kw == tk:
            p_scr[idx] = p
        else:
            p_scr[idx, :, 0:kw] = p
        a_scr[idx] = alpha
        l_scr[idx] = alpha * l_scr[idx] + psum
        m_scr[idx] = m_new

    def sweep(base, kw, last):
        q = q_ref[0]
        k = k_ref[0]
        vp = vp_ref[0]
        for h in range(DA_HEADS):
            for c in range(2):
                idx = 2 * h + c
                lo = h * dv + c * dh
                product(idx, vp[:, h * dv:(h + 1) * dv])
                logits(idx, q[:, lo:lo + dh], k[0:kw, lo:lo + dh], base)
        if not last:
            return
        vd = vd_ref[0]
        for idx in range(2 * DA_HEADS):
            product(idx, vd[0:kw, (idx // 2) * dv:(idx // 2 + 1) * dv])
        lam = _lambda(lq1_ref, lk1_ref, lq2_ref, lk2_ref, lam_init)
        sub = sub_ref[...]
        for h in range(DA_HEADS):
            inv1 = 1.0 / jnp.sum(l_scr[2 * h], axis=-1, keepdims=True)
            inv2 = 1.0 / jnp.sum(l_scr[2 * h + 1], axis=-1, keepdims=True)
            o = acc_scr[2 * h] * inv1 - lam * (acc_scr[2 * h + 1] * inv2)
            o_ref[0, :, h * dv:(h + 1) * dv] = (_rms(o, sub) * (1.0 - lam_init)).astype(o_ref.dtype)

    for vid, (base, kw, last) in enumerate(variants):
        pl.when(var == vid)(functools.partial(sweep, base, kw, last))

    @pl.when(j == 0)
    def _():
        acc_scr[...] = jnp.zeros_like(acc_scr)


def attn_prompt(q, k, v, tiles, lams, subln, *, tq, tk, lam_init):
    bsz, s, d = q.shape
    bt = tiles.shape[-1]
    assert tq % bt == 0 and tk % bt == 0 and s % tq == 0 and s % tk == 0
    dv = d // DA_HEADS
    qi, kj, var, variants = [], [], [], []
    for i in range(s // tq):
        jlast = ((i + 1) * tq - 1) // tk
        for j in range(jlast + 1):
            base = (tq // bt) * i - (tk // bt) * j
            kw = min(tk, (i + 1) * tq - j * tk)
            key = (base if base - (kw // bt - 1) <= 1 else None, kw, j == jlast)
            if key not in variants:
                variants.append(key)
            qi.append(i)
            kj.append(j)
            var.append(variants.index(key))
    n_steps = len(qi)
    qi, kj, var = (jnp.asarray(a, jnp.int32) for a in (qi, kj, var))
    dl = lams[0].shape[-1]
    const = lambda shape: pl.BlockSpec(shape, lambda b, st, qi, kj, var: (0,) * len(shape))
    grid_spec = pltpu.PrefetchScalarGridSpec(
        num_scalar_prefetch=3,
        grid=(bsz, n_steps),
        in_specs=[pl.BlockSpec((1, tq, d), lambda b, st, qi, kj, var: (b, qi[st], 0)),
                  pl.BlockSpec((1, tk, d), lambda b, st, qi, kj, var: (b, kj[st], 0)),
                  pl.BlockSpec((1, tk, d), lambda b, st, qi, kj, var: (b, jnp.maximum(kj[st] - 1, 0), 0)),
                  pl.BlockSpec((1, tk, d), lambda b, st, qi, kj, var: (b, ((qi[st] + 1) * tq - 1) // tk, 0)),
                  const(tiles.shape), const((1, dl)), const((1, dl)), const((1, dl)), const((1, dl)),
                  const((1, dv))],
        out_specs=pl.BlockSpec((1, tq, d), lambda b, st, qi, kj, var: (b, qi[st], 0)),
        scratch_shapes=[pltpu.VMEM((2 * DA_HEADS, tq, LANES), F32),
                        pltpu.VMEM((2 * DA_HEADS, tq, LANES), F32),
                        pltpu.VMEM((2 * DA_HEADS, tq, dv), F32),
                        pltpu.VMEM((2 * DA_HEADS, tq, tk), BF16),
                        pltpu.VMEM((2 * DA_HEADS, tq, LANES), F32)])
    return pl.pallas_call(
        functools.partial(_attn_prompt_kernel, lam_init=lam_init, variants=tuple(variants)),
        grid_spec=grid_spec,
        out_shape=jax.ShapeDtypeStruct((bsz, s, d), BF16),
        compiler_params=_cparams(("parallel", "arbitrary")),
        name="attn_prompt",
    )(qi, kj, var, q, k, v, v, tiles, *lams, subln.reshape(1, dv))


def _decode_fetch(b, pt_ref, ck_hbm, cv_hbm, kbuf, vbuf, sem, *, n_rows):
    bufs, n_pages = kbuf.shape[0], kbuf.shape[1]

    def page_copies(row, slot):
        cps = []
        for r in range(n_pages):
            pg = pt_ref[row, r]
            cps.append(pltpu.make_async_copy(ck_hbm.at[pg], kbuf.at[slot, r], sem.at[slot]))
            cps.append(pltpu.make_async_copy(cv_hbm.at[pg], vbuf.at[slot, r], sem.at[slot]))
        return cps

    @pl.when(b == 0)
    def _():
        for row in range(min(bufs - 1, n_rows)):
            for cp in page_copies(row, row):
                cp.start()

    @pl.when(b + (bufs - 1) < n_rows)
    def _():
        nxt = b + (bufs - 1)
        for cp in page_copies(nxt, nxt % bufs):
            cp.start()

    slot = b % bufs
    return ([kbuf.at[slot, r] for r in range(n_pages)], [vbuf.at[slot, r] for r in range(n_pages)],
            page_copies(b, slot))


def _decode_row(k_refs, v_refs, q_ref, kn_ref, vn_ref, dbias_ref, lq1_ref, lk1_ref, lq2_ref, lk2_ref, sub_ref,
                o_ref, *, lam_init):
    n_pages = len(k_refs)
    d = q_ref.shape[-1]
    dv = d // DA_HEADS
    dh = dv // 2
    nmap = 2 * DA_HEADS
    page = k_refs[0].shape[0]
    rows = page * DA_HEADS

    def rows2d(ref):
        return ref[...].reshape(rows, dv).astype(BF16)

    def head_rows(x):
        return jnp.concatenate([x[:, h * dv:(h + 1) * dv] for h in range(DA_HEADS)]
                               + [jnp.zeros((nmap - DA_HEADS, dv), x.dtype)], axis=0)

    def own_head(n):
        j = lax.broadcasted_iota(jnp.int32, (nmap, n), 0)
        r = lax.broadcasted_iota(jnp.int32, (nmap, n), 1)
        return r % DA_HEADS == j // 2

    q = q_ref[0].astype(F32)
    jrow = lax.broadcasted_iota(jnp.int32, (nmap, dv), 0)
    lane = lax.broadcasted_iota(jnp.int32, (nmap, dv), 1)
    qmat = jnp.zeros((nmap, dv), F32)
    for h in range(DA_HEADS):
        qh = jnp.broadcast_to(q[:, h * dv:(h + 1) * dv], (nmap, dv))
        qmat = jnp.where(jnp.logical_and(jrow // 2 == h, lane // dh == jrow % 2), qh, qmat)
    qmat = qmat.astype(BF16)

    kn8 = head_rows(kn_ref[0]).astype(BF16)
    vn8 = head_rows(vn_ref[0])
    s_new = lax.dot_general(qmat, kn8, _NT, preferred_element_type=F32) + dbias_ref[:, rows:rows + nmap]
    col = lax.broadcasted_iota(jnp.int32, (nmap, nmap), 1)
    s_new = jnp.where(jnp.logical_and(own_head(nmap), col < DA_HEADS), s_new, NEG_BIG)
    m = jnp.max(s_new, axis=1, keepdims=True)
    p_new = jnp.exp2(s_new - m)
    lsum = jnp.sum(p_new, axis=1, keepdims=True)
    acc = jnp.zeros((nmap, dv), F32)
    for h in range(DA_HEADS):
        acc = acc + p_new[:, h:h + 1] * vn8[h:h + 1, :]

    valid = own_head(rows)
    group = min(8, n_pages)
    assert n_pages % group == 0
    for g0 in range(0, n_pages, group):
        s_grp = []
        for r in range(g0, g0 + group):
            s = lax.dot_general(qmat, rows2d(k_refs[r]), _NT, preferred_element_type=F32)
            if r == n_pages - 1:
                s = s + dbias_ref[:, 0:rows]
            s_grp.append(jnp.where(valid, s, NEG_BIG))
        smax = s_grp[0]
        for s in s_grp[1:]:
            smax = jnp.maximum(smax, s)
        m_new = jnp.maximum(m, jnp.max(smax, axis=1, keepdims=True))
        alpha = jnp.exp2(m - m_new)
        psum = jnp.zeros((nmap, rows), F32)
        acc_g = jnp.zeros(acc.shape, F32)
        for r in range(g0, g0 + group):
            p = jnp.exp2(s_grp[r - g0] - m_new)
            psum = psum + p
            acc_g = acc_g + jnp.dot(p.astype(BF16), rows2d(v_refs[r]), preferred_element_type=F32)
        lsum = alpha * lsum + jnp.sum(psum, axis=1, keepdims=True)
        acc = alpha * acc + acc_g
        m = m_new
    an = acc * (1.0 / lsum)
    lam = _lambda(lq1_ref, lk1_ref, lq2_ref, lk2_ref, lam_init)
    sub = sub_ref[...]
    for h in range(DA_HEADS):
        o = an[2 * h:2 * h + 1, :] - lam * an[2 * h + 1:2 * h + 2, :]
        o_ref[0, :, h * dv:(h + 1) * dv] = (_rms(o, sub) * (1.0 - lam_init)).astype(o_ref.dtype)


def _attn_decode_kernel(pt_ref, q_ref, kn_ref, vn_ref, ck_hbm, cv_hbm, dbias_ref,
                        lq1_ref, lk1_ref, lq2_ref, lk2_ref, sub_ref, o_ref, kbuf, vbuf, sem, *, n_rows, lam_init):
    k_refs, v_refs, pending = _decode_fetch(pl.program_id(0), pt_ref, ck_hbm, cv_hbm, kbuf, vbuf, sem,
                                            n_rows=n_rows)
    for cp in pending:
        cp.wait()
    _decode_row(k_refs, v_refs, q_ref, kn_ref, vn_ref, dbias_ref, lq1_ref, lk1_ref, lq2_ref, lk2_ref, sub_ref,
                o_ref, lam_init=lam_init)


def attn_decode(q, k_new, v_new, cache_k, cache_v, page_table, dbias, lams, subln, *, lam_init):
    bsz, d = q.shape
    n_pages = page_table.shape[1]
    page = cache_k.shape[1]
    assert page == LANES
    dv = d // DA_HEADS
    dl = lams[0].shape[-1]
    row_spec = pl.BlockSpec((1, 1, d), lambda b, pt: (b, 0, 0))
    hbm_spec = pl.BlockSpec(memory_space=pl.ANY)
    lam_spec = pl.BlockSpec((1, dl), lambda b, pt: (0, 0))
    page_buf = pltpu.VMEM((DECODE_BUFS, n_pages, page, DA_HEADS, dv), cache_k.dtype)
    grid_spec = pltpu.PrefetchScalarGridSpec(
        num_scalar_prefetch=1,
        grid=(bsz,),
        in_specs=[row_spec, row_spec, row_spec, hbm_spec, hbm_spec,
                  pl.BlockSpec(dbias.shape, lambda b, pt: (0, 0)),
                  lam_spec, lam_spec, lam_spec, lam_spec,
                  pl.BlockSpec((1, dv), lambda b, pt: (0, 0))],
        out_specs=pl.BlockSpec((1, 1, d), lambda b, pt: (b, 0, 0)),
        scratch_shapes=[page_buf, page_buf, pltpu.SemaphoreType.DMA((DECODE_BUFS,))])
    r3 = lambda a: a.reshape(bsz, 1, d)
    out = pl.pallas_call(
        functools.partial(_attn_decode_kernel, n_rows=bsz, lam_init=lam_init),
        grid_spec=grid_spec,
        out_shape=jax.ShapeDtypeStruct((bsz, 1, d), BF16),
        compiler_params=_cparams(("arbitrary",)),
        name="attn_decode",
    )(page_table, r3(q), r3(k_new), r3(v_new), cache_k, cache_v, dbias, *lams, subln.reshape(1, dv))
    return out.reshape(bsz, d)


def _hgrn_decode_kernel(pt_ref, q_ref, f_ref, i_ref, g_ref, lb_ref, gn_ref, mall_ref,
                        dq_ref, kn_ref, vn_ref, ck_hbm, cv_hbm, dbias_ref, lq1_ref, lk1_ref, lq2_ref, lk2_ref,
                        sub_ref, o_ref, st_ref, do_ref, s_scr, kbuf, vbuf, sem,
                        *, C, levels, nsub, n_rows, lam_init):
    cidx = pl.program_id(1)
    row = pl.program_id(0) * pl.num_programs(1) + cidx
    k_refs, v_refs, pending = _decode_fetch(row, pt_ref, ck_hbm, cv_hbm, kbuf, vbuf, sem, n_rows=n_rows)
    _hgrn_chunks(q_ref, f_ref, i_ref, g_ref, lb_ref, gn_ref, mall_ref, o_ref, st_ref, s_scr,
                 first=cidx == 0, last=cidx == pl.num_programs(1) - 1, C=C, levels=levels, nsub=nsub)
    for cp in pending:
        cp.wait()
    _decode_row(k_refs, v_refs, dq_ref, kn_ref, vn_ref, dbias_ref, lq1_ref, lk1_ref, lq2_ref, lk2_ref, sub_ref,
                do_ref, lam_init=lam_init)


def hgrn_prompt_decode(q, f, i, g, lb, gnorm, dq, k_new, v_new, cache_k, cache_v, page_table, dbias, lams, subln,
                       *, C, nsub, lam_init):
    bsz, s, d = q.shape
    n_rows = dq.shape[0]
    dk = d // HG_HEADS
    dv = d // DA_HEADS
    tb = C * nsub
    steps = s // tb
    assert bsz * steps == n_rows
    n_pages = page_table.shape[1]
    page = cache_k.shape[1]
    mall_np, levels = _hgrn_sum_matrices(C)
    mall = jnp.asarray(mall_np, dtype=BF16)
    dl = lams[0].shape[-1]
    blk = pl.BlockSpec((1, tb, d), lambda b, c, pt: (b, c, 0))
    const = lambda shape: pl.BlockSpec(shape, lambda b, c, pt: (0,) * len(shape))
    row_spec = pl.BlockSpec((1, 1, d), lambda b, c, pt: (b * steps + c, 0, 0))
    hbm_spec = pl.BlockSpec(memory_space=pl.ANY)
    page_buf = pltpu.VMEM((2, n_pages, page, DA_HEADS, dv), cache_k.dtype)
    grid_spec = pltpu.PrefetchScalarGridSpec(
        num_scalar_prefetch=1,
        grid=(bsz, steps),
        in_specs=[blk, blk, blk, blk, const((1, d)), const((1, d)), const(mall.shape),
                  row_spec, row_spec, row_spec, hbm_spec, hbm_spec, const(dbias.shape),
                  const((1, dl)), const((1, dl)), const((1, dl)), const((1, dl)), const((1, dv))],
        out_specs=[pl.BlockSpec((1, tb, d), lambda b, c, pt: (b, c, 0)),
                   pl.BlockSpec((1, 1, HG_HEADS, dk, dk), lambda b, c, pt: (0, b, 0, 0, 0)),
                   row_spec],
        scratch_shapes=[pltpu.VMEM((HG_HEADS, dk, dk), F32), page_buf, page_buf,
                        pltpu.SemaphoreType.DMA((2,))])
    r3 = lambda a: a.reshape(n_rows, 1, d)
    o, state, o_dec = pl.pallas_call(
        functools.partial(_hgrn_decode_kernel, C=C, levels=levels, nsub=nsub, n_rows=n_rows, lam_init=lam_init),
        grid_spec=grid_spec,
        out_shape=[jax.ShapeDtypeStruct((bsz, s, d), BF16),
                   jax.ShapeDtypeStruct((1, bsz, HG_HEADS, dk, dk), F32),
                   jax.ShapeDtypeStruct((n_rows, 1, d), BF16)],
        compiler_params=_cparams(("arbitrary", "arbitrary")),
        name="hgrn_prompt_decode",
    )(page_table, q, f, i, g, lb, gnorm.reshape(1, d), mall, r3(dq), r3(k_new), r3(v_new), cache_k, cache_v,
      dbias, *lams, subln.reshape(1, dv))
    return o, state, o_dec.reshape(n_rows, d)


DECODE_BUFS = 3
ATTN_TQ = 512
ATTN_TK = 512
HGRN_C = 64
HGRN_NSUB = 8
TM = 512


def _in_proj(x, p):
    return proj(x, [(p["norm_mix"][0], (p["w_in_a"], 0), 1.0, [[(F32, 0)]] * 4)], tm=TM, name="in_proj")


def _after_mixer0(o, x, p, scale):
    x = mix_ffn(o, x, p["w_out_a"], p["norm_ffn"][0], p["w_gate_up"], p["w_down"], p["norm_final"],
                layer=0, tm=TM, final_norm=False, name="mix_ffn0")
    k32, k16, v32, v16, qa = proj(
        x, [(p["kv_norm"], (p["w_kv"], 0), 1.0, [[(F32, DA_HEADS), (BF16, 0)]] * 2),
            (p["norm_mix"][1], (p["w_q_b"], 0), scale * LOG2E, [[(BF16, 0)]])],
        tm=TM, name="kvq_proj")
    return x, k32, k16, v32, v16, qa


def _after_mixer1(o, x, p):
    return mix_ffn(o, x, p["w_out_b"], p["norm_ffn"][1], p["w_gate_up"], p["w_down"], p["norm_final"],
                   layer=1, tm=TM, final_norm=True, name="mix_ffn1")


def kernel(x_prompt, x_sample, cache_k, cache_v, state_hgrn, page_table, w_in_a, lower_bound, gnorm_a, w_out_a,
           w_q_b, lambda_q1, lambda_k1, lambda_q2, lambda_k2, subln_b, w_out_b, kv_norm, w_kv, rel_bias,
           norm_mix, norm_ffn, w_gate_up, w_down, norm_final):
    bf = lambda w: (w if w.ndim == 3 else w[None]).astype(BF16)
    p = dict(w_in_a=bf(w_in_a), w_out_a=bf(w_out_a), w_q_b=bf(w_q_b), w_out_b=bf(w_out_b), kv_norm=kv_norm,
             w_kv=bf(w_kv), norm_mix=norm_mix, norm_ffn=norm_ffn, w_gate_up=bf(w_gate_up), w_down=bf(w_down),
             norm_final=norm_final)
    bsz, s, d = x_prompt.shape
    nb = x_sample.shape[0]
    dv = d // DA_HEADS
    scale = (dv // 2) ** -0.5
    lam_init = 0.8 - 0.6 * math.exp(-0.3 * 1)
    lams = [a[0].reshape(1, -1) for a in (lambda_q1, lambda_k1, lambda_q2, lambda_k2)]
    gnorm, subln = gnorm_a[0], subln_b[0]
    tiles, dbias, lb = param_tables(rel_bias, lower_bound, cache_k.shape[1])
    xp = x_prompt.reshape(bsz * s, d)
    xs = x_sample.reshape(nb, d)
    r3 = lambda a: a.reshape(bsz, s, d)

    q, f, i, g = _in_proj(xs, p)
    o_s, st_s = hgrn_step(q, f, i, g, lb, gnorm, state_hgrn, bb=16)
    xs, k_s, _, v_s, _, qa_s = _after_mixer0(o_s, xs, p, scale)

    q, f, i, g = _in_proj(xp, p)
    tokens_per_row = (bsz * s) // nb
    if (bsz * s) % nb == 0 and tokens_per_row % HGRN_C == 0 and s % tokens_per_row == 0:
        o_p, st_p, o_dec = hgrn_prompt_decode(
            r3(q), r3(f), r3(i), r3(g), lb, gnorm, qa_s, k_s.reshape(nb, d), v_s.reshape(nb, d), cache_k, cache_v,
            page_table, dbias, lams, subln, C=HGRN_C, nsub=tokens_per_row // HGRN_C, lam_init=lam_init)
    else:
        o_p, st_p = hgrn_prompt(r3(q), r3(f), r3(i), r3(g), lb, gnorm, C=HGRN_C, nsub=HGRN_NSUB)
        o_dec = attn_decode(qa_s, k_s.reshape(nb, d), v_s.reshape(nb, d), cache_k, cache_v, page_table, dbias,
                            lams, subln, lam_init=lam_init)
    xp, k_p, k16, v_p, v16, qa_p = _after_mixer0(o_p.reshape(bsz * s, d), xp, p, scale)

    o_a = attn_prompt(r3(qa_p), r3(k16), r3(v16), tiles, lams, subln,
                      tq=min(ATTN_TQ, s), tk=min(ATTN_TK, s), lam_init=lam_init)
    y_p = _after_mixer1(o_a.reshape(bsz * s, d), xp, p)
    y_s = _after_mixer1(o_dec, xs, p)
    hk = DA_HEADS, dv
    return (y_p.reshape(bsz, s, d), y_s.reshape(nb, 1, d),
            k_p.reshape(bsz, s, *hk), v_p.reshape(bsz, s, *hk), st_p,
            k_s.reshape(nb, 1, *hk), v_s.reshape(nb, 1, *hk), st_s)
```

```python
import functools
import math

import numpy as np
import jax
import jax.numpy as jnp
from jax import lax
from jax.experimental import pallas as pl
from jax.experimental.pallas import tpu as pltpu

F32 = jnp.float32
BF16 = jnp.bfloat16
EPS = 1e-6
NEG_BIG = -1e30
LOG2E = math.log2(math.e)

HG_HEADS = 8
DA_HEADS = 4
N_BUCKETS = 32
MAX_DISTANCE = 128
LANES = 128
SUBLANES = 8
MXU_N = 256
VMEM_LIMIT = 56 * 1024 * 1024

_NT = (((1,), (1,)), ((), ()))
_TN = (((0,), (0,)), ((), ()))


def _cparams(sem):
    return pltpu.CompilerParams(dimension_semantics=sem, vmem_limit_bytes=VMEM_LIMIT)


def _resident(shape):
    return pl.BlockSpec(shape, lambda *_: (0,) * len(shape), pipeline_mode=pl.Buffered(1))


def _resident_layer(w, layer):
    return pl.BlockSpec((1,) + w.shape[1:], lambda *_: (layer, 0, 0), pipeline_mode=pl.Buffered(1))


def _sigmoid(x):
    return 1.0 / (1.0 + jnp.exp(-x))


def _silu(x):
    return x * _sigmoid(x)


def _rms(x, g):
    ms = jnp.mean(x * x, axis=-1, keepdims=True)
    return x * lax.rsqrt(ms + EPS) * g


def _lane_tile(x, reps):
    return x if reps == 1 else jnp.concatenate([x] * reps, axis=1)


def _proj_kernel(x_ref, *refs, branches):
    nb = len(branches)
    g_refs = refs[:nb]
    w_refs = refs[nb:2 * nb]
    out_refs = refs[2 * nb:]
    x = x_ref[...]
    inv = lax.rsqrt(jnp.mean(x * x, axis=-1, keepdims=True) + EPS)
    o = 0
    for bi, (scale, groups) in enumerate(branches):
        xn = (x * inv * g_refs[bi][...]).astype(BF16)
        ng = w_refs[bi].shape[2] // len(groups)
        for gi, outs in enumerate(groups):
            acc = jnp.dot(xn, w_refs[bi][0, :, gi * ng:(gi + 1) * ng].astype(BF16), preferred_element_type=F32)
            if scale != 1.0:
                acc = acc * scale
            for _ in outs:
                ref = out_refs[o]
                if len(ref.shape) == 3:
                    hd = ref.shape[2]
                    for hh in range(ref.shape[1]):
                        ref[:, hh, :] = acc[:, hh * hd:(hh + 1) * hd].astype(ref.dtype)
                else:
                    ref[...] = acc.astype(ref.dtype)
                o += 1


def proj(x, branches, *, tm, name):
    m, d = x.shape
    tm = min(tm, m)
    gains = [b[0].reshape(1, d) for b in branches]
    weights = [b[1] for b in branches]
    out_shape, out_specs = [], []
    for _, (w, _), _, groups in branches:
        ng = w.shape[2] // len(groups)
        for outs in groups:
            for dt, heads in outs:
                if heads:
                    out_shape.append(jax.ShapeDtypeStruct((m, heads, ng // heads), dt))
                    out_specs.append(pl.BlockSpec((tm, heads, ng // heads), lambda i: (i, 0, 0)))
                else:
                    out_shape.append(jax.ShapeDtypeStruct((m, ng), dt))
                    out_specs.append(pl.BlockSpec((tm, ng), lambda i: (i, 0)))
    return pl.pallas_call(
        functools.partial(_proj_kernel, branches=[(b[2], b[3]) for b in branches]),
        grid=(m // tm,),
        in_specs=([pl.BlockSpec((tm, d), lambda i: (i, 0))]
                  + [_resident(g.shape) for g in gains] + [_resident_layer(w, l) for w, l in weights]),
        out_specs=out_specs,
        out_shape=out_shape,
        compiler_params=_cparams(("parallel",)),
        name=name,
    )(x, *gains, *[w for w, _ in weights])


def _mix_ffn_kernel(a_ref, x_ref, wo_ref, g_ref, wgu_ref, wd_ref, gf_ref, o_ref, x1_scr, hid_scr, *, final_norm):
    dff = wd_ref.shape[1]
    x1 = x_ref[...] + jnp.dot(a_ref[...], wo_ref[0], preferred_element_type=F32)
    x1_scr[...] = x1
    xn = _rms(x1, g_ref[...]).astype(BF16)
    for j in range(dff // MXU_N):
        cs = slice(j * MXU_N, (j + 1) * MXU_N)
        gt = jnp.dot(xn, wgu_ref[0, :, cs], preferred_element_type=F32)
        ut = jnp.dot(xn, wgu_ref[0, :, dff + j * MXU_N:dff + (j + 1) * MXU_N], preferred_element_type=F32)
        hid_scr[:, cs] = (_silu(gt) * ut).astype(BF16)
    y = x1_scr[...] + jnp.dot(hid_scr[...], wd_ref[0].astype(BF16), preferred_element_type=F32)
    if final_norm:
        y = _rms(y, gf_ref[...])
    o_ref[...] = y


def mix_ffn(a, x, w_out, g, w_gate_up, w_down, g_final, *, layer, tm, final_norm, name):
    m, d = x.shape
    dff = w_down.shape[1]
    assert dff % MXU_N == 0
    tm = min(tm, m)
    row = lambda width: pl.BlockSpec((tm, width), lambda i: (i, 0))
    return pl.pallas_call(
        functools.partial(_mix_ffn_kernel, final_norm=final_norm),
        grid=(m // tm,),
        in_specs=[row(a.shape[1]), row(d), _resident_layer(w_out, 0), _resident((1, d)),
                  _resident_layer(w_gate_up, layer), _resident_layer(w_down, layer), _resident((1, d))],
        out_specs=row(d),
        out_shape=jax.ShapeDtypeStruct((m, d), F32),
        scratch_shapes=[pltpu.VMEM((tm, d), F32), pltpu.VMEM((tm, dff), BF16)],
        compiler_params=_cparams(("parallel",)),
        name=name,
    )(a, x, w_out, g.reshape(1, d), w_gate_up, w_down, g_final.reshape(1, d))


BIAS_T = 2 * MAX_DISTANCE


def _bucket_starts():
    max_exact = N_BUCKETS // 2
    n = np.arange(2 * MAX_DISTANCE, dtype=np.int32)
    nf = np.maximum(n, 1).astype(np.float32)
    large = max_exact + (np.log(nf / np.float32(max_exact)) / np.float32(math.log(MAX_DISTANCE / max_exact))
                         * np.float32(N_BUCKETS - max_exact)).astype(np.int32)
    bucket = np.where(n < max_exact, n, np.minimum(large, N_BUCKETS - 1))
    assert (np.diff(bucket) >= 0).all() and (np.diff(bucket) <= 1).all() and bucket[-1] == N_BUCKETS - 1
    return [int(np.argmax(bucket >= b)) for b in range(N_BUCKETS)]


def _bucket_bias(n, rb_ref, n_heads):
    starts = _bucket_starts()
    reached = [n >= starts[b] for b in range(1, N_BUCKETS - 1)]
    outs = []
    for h in range(n_heads):
        far = rb_ref[N_BUCKETS - 1, h]
        acc = jnp.full(n.shape, (rb_ref[0, h] - far) * LOG2E, F32)
        for b in range(1, N_BUCKETS - 1):
            acc = jnp.where(reached[b - 1], (rb_ref[b, h] - far) * LOG2E, acc)
        outs.append(jnp.where(n >= starts[N_BUCKETS - 1], 0.0, acc))
    return outs


def _tables_kernel(rb_ref, lb_in_ref, tiles_ref, dec_ref, lb_ref):
    bt = tiles_ref.shape[-1]
    r = lax.broadcasted_iota(jnp.int32, (bt, bt), 0)
    c = lax.broadcasted_iota(jnp.int32, (bt, bt), 1)
    n0 = r - c
    diag = _bucket_bias(jnp.maximum(n0, 0), rb_ref, DA_HEADS)
    prev = _bucket_bias(bt + n0, rb_ref, DA_HEADS)
    for h in range(DA_HEADS):
        tiles_ref[0, h] = jnp.where(n0 >= 0, diag[h], NEG_BIG)
        tiles_ref[1, h] = prev[h]
    page_cols = dec_ref.shape[1] - LANES
    rr = lax.broadcasted_iota(jnp.int32, dec_ref.shape, 0)
    cc = lax.broadcasted_iota(jnp.int32, dec_ref.shape, 1)
    nd = jnp.where(cc < page_cols, page_cols // DA_HEADS - cc // DA_HEADS, 0)
    dec = _bucket_bias(nd, rb_ref, DA_HEADS)
    acc = jnp.zeros(dec_ref.shape, F32)
    for h in range(DA_HEADS):
        acc = jnp.where(rr // 2 == h, dec[h], acc)
    dec_ref[...] = acc
    lbi = lb_in_ref[...]
    mx = jnp.max(lbi, axis=0, keepdims=True)
    e = jnp.exp(lbi - mx)
    lb_ref[...] = e[0:1, :] / jnp.sum(e, axis=0, keepdims=True)


def param_tables(rel_bias, lower_bound, page):
    d = lower_bound.shape[1]
    return pl.pallas_call(
        _tables_kernel,
        in_specs=[pl.BlockSpec(memory_space=pltpu.SMEM),
                  pl.BlockSpec(memory_space=pltpu.VMEM)],
        out_specs=[pl.BlockSpec(memory_space=pltpu.VMEM)] * 3,
        out_shape=[jax.ShapeDtypeStruct((2, DA_HEADS, BIAS_T, BIAS_T), F32),
                   jax.ShapeDtypeStruct((2 * DA_HEADS, page * DA_HEADS + LANES), F32),
                   jax.ShapeDtypeStruct((1, d), F32)],
        compiler_params=pltpu.CompilerParams(vmem_limit_bytes=VMEM_LIMIT),
        name="param_tables",
    )(rel_bias, lower_bound)


def _hgrn_sum_matrices(C):
    levels = int(math.log2(C))
    t = np.arange(C)
    u = t[None, :]
    mats = [u <= t[:, None], u > t[:, None]]
    for lv in range(levels):
        c = 1 << lv
        e = (t // (2 * c)) * (2 * c) + c - 1
        upper = (t > e)[:, None]
        seg = np.where(upper, (u > e[:, None]) & (u <= t[:, None]), (u > t[:, None]) & (u <= e[:, None]))
        mats.append(seg)
    return np.concatenate(mats, axis=0).astype(np.float32), levels


def _split2(x):
    hi = x.astype(BF16)
    lo = (x - hi.astype(F32)).astype(BF16)
    return hi, lo


def _hgrn_chunks(q_ref, f_ref, i_ref, g_ref, lb_ref, gn_ref, mall_ref, o_ref, st_ref, s_scr,
                 *, first, last, C, levels, nsub):
    dk = s_scr.shape[-1]

    @pl.when(first)
    def _():
        s_scr[...] = jnp.zeros_like(s_scr)

    lb = lb_ref[...]
    gn = gn_ref[...]
    mall = mall_ref[...]
    ti = lax.broadcasted_iota(jnp.int32, (C, C), 0)
    si = lax.broadcasted_iota(jnp.int32, (C, C), 1)
    txs = jnp.bitwise_xor(ti, si)
    lower = ti > si
    valid = [jnp.logical_and(jnp.right_shift(txs, lv) == 1, lower) for lv in range(levels)]
    trow = lax.broadcasted_iota(jnp.int32, (C, q_ref.shape[-1]), 0)
    upper = [jnp.bitwise_and(jnp.right_shift(trow, lv), 1) == 1 for lv in range(levels)]

    for sub in range(nsub):
        rows = slice(sub * C, (sub + 1) * C)
        fg_all = lb + (1.0 - lb) * _sigmoid(f_ref[0, rows, :])
        hi, lo = _split2(jnp.log(fg_all) * LOG2E)
        sums = jnp.dot(mall, hi, preferred_element_type=F32)
        ends = sums[0:2 * C] + jnp.dot(mall[0:2 * C], lo, preferred_element_type=F32)

        q = _silu(q_ref[0, rows, :])
        kk = 1.0 - fg_all
        v = i_ref[0, rows, :]
        v_bf = v.astype(BF16)
        b = ends[0:C]
        suf = ends[C:2 * C]
        qe = (q * jnp.exp2(b)).astype(BF16)
        kt = (kk * jnp.exp2(suf)).astype(BF16)
        dec_last = jnp.exp2(b[C - 1:C, :])

        scores = [jnp.zeros((C, C), F32) for _ in range(HG_HEADS)]
        for lv in range(levels):
            e = jnp.exp2(sums[(2 + lv) * C:(3 + lv) * C])
            if (1 << lv) >= SUBLANES:
                role = jnp.concatenate([(q if (r >> lv) & 1 else kk)[r:r + SUBLANES]
                                        for r in range(0, C, SUBLANES)], axis=0)
            else:
                role = jnp.where(upper[lv], q, kk)
            x_bf = (role * e).astype(BF16)
            for h in range(HG_HEADS):
                hs = slice(h * dk, (h + 1) * dk)
                sl = lax.dot_general(x_bf[:, hs], x_bf[:, hs], _NT, preferred_element_type=F32)
                scores[h] = jnp.where(valid[lv], sl, scores[h])

        qk = q * kk
        g = g_ref[0, rows, :]
        for h in range(HG_HEADS):
            hs = slice(h * dk, (h + 1) * dk)
            st = s_scr[h]
            diag = jnp.sum(qk[:, hs], axis=-1, keepdims=True)
            o = (jnp.dot(scores[h].astype(BF16), v_bf[:, hs], preferred_element_type=F32)
                 + diag * v[:, hs]
                 + lax.dot_general(qe[:, hs], st.astype(BF16), _NT, preferred_element_type=F32))
            s_scr[h] = st * dec_last[:, hs] + lax.dot_general(v_bf[:, hs], kt[:, hs], _TN,
                                                              preferred_element_type=F32)
            o_ref[0, rows, hs] = (_rms(o, gn[:, hs]) * _silu(g[:, hs])).astype(o_ref.dtype)

    @pl.when(last)
    def _():
        for h in range(HG_HEADS):
            st_ref[0, 0, h] = s_scr[h].T


def _hgrn_prompt_kernel(*refs, C, levels, nsub):
    cidx = pl.program_id(1)
    _hgrn_chunks(*refs, first=cidx == 0, last=cidx == pl.num_programs(1) - 1, C=C, levels=levels, nsub=nsub)


def hgrn_prompt(q, f, i, g, lb, gnorm, *, C, nsub):
    bsz, s, d = q.shape
    dk = d // HG_HEADS
    mall_np, levels = _hgrn_sum_matrices(C)
    mall = jnp.asarray(mall_np, dtype=BF16)
    tb = C * nsub
    blk = pl.BlockSpec((1, tb, d), lambda b, c: (b, c, 0))
    vec = pl.BlockSpec((1, d), lambda b, c: (0, 0))
    return pl.pallas_call(
        functools.partial(_hgrn_prompt_kernel, C=C, levels=levels, nsub=nsub),
        grid=(bsz, s // tb),
        in_specs=[blk, blk, blk, blk, vec, vec,
                  pl.BlockSpec(mall.shape, lambda b, c: (0, 0))],
        out_specs=[pl.BlockSpec((1, tb, d), lambda b, c: (b, c, 0)),
                   pl.BlockSpec((1, 1, HG_HEADS, dk, dk), lambda b, c: (0, b, 0, 0, 0))],
        out_shape=[jax.ShapeDtypeStruct((bsz, s, d), BF16),
                   jax.ShapeDtypeStruct((1, bsz, HG_HEADS, dk, dk), F32)],
        scratch_shapes=[pltpu.VMEM((HG_HEADS, dk, dk), F32)],
        compiler_params=_cparams(("parallel", "arbitrary")),
        name="hgrn_prompt",
    )(q, f, i, g, lb, gnorm.reshape(1, d), mall)


def _hgrn_step_kernel(q_ref, f_ref, i_ref, g_ref, lb_ref, gn_ref, s_ref, o_ref, so_ref):
    rows = q_ref.shape[0]
    lb = lb_ref[...]
    fg = lb + (1.0 - lb) * _sigmoid(f_ref[...])
    q_t = _silu(q_ref[...]).T
    fg_t = fg.T
    kk_t = (1.0 - fg).T
    v = i_ref[...]
    outs = []
    for r in range(rows):
        bl, h = divmod(r, HG_HEADS)
        s_new = fg_t[:, r:r + 1] * s_ref[0, bl, h] + kk_t[:, r:r + 1] * v[r:r + 1, :]
        so_ref[0, bl, h] = s_new
        outs.append(jnp.sum(q_t[:, r:r + 1] * s_new, axis=0, keepdims=True))
    o = jnp.concatenate(outs, axis=0)
    o_ref[...] = (_rms(o, gn_ref[...]) * _silu(g_ref[...])).astype(o_ref.dtype)


def hgrn_step(q, f, i, g, lb, gnorm, state, *, bb):
    bsz, d = q.shape
    dk = d // HG_HEADS
    rows = bb * HG_HEADS
    to_rows = lambda a: a.reshape(bsz * HG_HEADS, dk)
    tile = lambda p: jnp.tile(p.reshape(HG_HEADS, dk), (bb, 1))
    rblk = pl.BlockSpec((rows, dk), lambda b: (b, 0))
    pblk = pl.BlockSpec((rows, dk), lambda b: (0, 0))
    sblk = pl.BlockSpec((1, bb, HG_HEADS, dk, dk), lambda b: (0, b, 0, 0, 0))
    o, s_new = pl.pallas_call(
        _hgrn_step_kernel,
        grid=(bsz // bb,),
        in_specs=[rblk, rblk, rblk, rblk, pblk, pblk, sblk],
        out_specs=[rblk, sblk],
        out_shape=[jax.ShapeDtypeStruct((bsz * HG_HEADS, dk), BF16),
                   jax.ShapeDtypeStruct(state.shape, F32)],
        compiler_params=_cparams(("parallel",)),
        name="hgrn_step",
    )(to_rows(q), to_rows(f), to_rows(i), to_rows(g), tile(lb), tile(gnorm), state)
    return o.reshape(bsz, d), s_new


def _lambda(lq1_ref, lk1_ref, lq2_ref, lk2_ref, lam_init):
    s1 = jnp.sum(lq1_ref[...] * lk1_ref[...], axis=-1, keepdims=True)
    s2 = jnp.sum(lq2_ref[...] * lk2_ref[...], axis=-1, keepdims=True)
    return jnp.exp(s1) - jnp.exp(s2) + lam_init


def _attn_prompt_kernel(qi_ref, kj_ref, var_ref, q_ref, k_ref, vp_ref, vd_ref, tiles_ref,
                        lq1_ref, lk1_ref, lq2_ref, lk2_ref, sub_ref, o_ref,
                        m_scr, l_scr, acc_scr, p_scr, a_scr, *, lam_init, variants):
    step = pl.program_id(1)
    j = kj_ref[step]
    var = var_ref[step]
    tq = q_ref.shape[1]
    tk = k_ref.shape[1]
    bt = tiles_ref.shape[-1]
    dv = acc_scr.shape[-1]
    dh = dv // 2

    @pl.when(j == 0)
    def _():
        m_scr[...] = jnp.full_like(m_scr, NEG_BIG)
        l_scr[...] = jnp.zeros_like(l_scr)

    @pl.when(step == 0)
    def _():
        acc_scr[...] = jnp.zeros_like(acc_scr)
        p_scr[...] = jnp.zeros_like(p_scr)
        a_scr[...] = jnp.zeros_like(a_scr)

    def add_bias(h, s, base):
        rows = []
        for r in range(tq // bt):
            cols = []
            for c in range(s.shape[1] // bt):
                dist = base + r - c
                blk = s[r * bt:(r + 1) * bt, c * bt:(c + 1) * bt]
                if dist < 0:
                    blk = jnp.full((bt, bt), NEG_BIG, F32)
                elif dist <= 1:
                    blk = blk + tiles_ref[dist, h]
                cols.append(blk)
            rows.append(jnp.concatenate(cols, axis=1))
        return jnp.concatenate(rows, axis=0)

    def product(idx, vh):
        kw = vh.shape[0]
        p = p_scr[idx] if kw == tk else p_scr[idx, :, 0:kw]
        acc_scr[idx] = (_lane_tile(a_scr[idx], dv // LANES) * acc_scr[idx]
                        + jnp.dot(p, vh, preferred_element_type=F32))

    def logits(idx, qh, kh, base):
        kw = kh.shape[0]
        s = lax.dot_general(qh, kh, _NT, preferred_element_type=F32)
        if base is not None:
            s = add_bias(idx // 2, s, base)
        m_prev = m_scr[idx]
        m_new = jnp.maximum(m_prev, jnp.max(s, axis=-1, keepdims=True))
        alpha = jnp.exp2(m_prev - m_new)
        chunks = [jnp.exp2(s[:, t * LANES:(t + 1) * LANES] - m_new) for t in range(kw // LANES)]
        psum = chunks[0]
        for ch in chunks[1:]:
            psum = psum + ch
        p = jnp.concatenate([ch.astype(BF16) for ch in chunks], axis=1)
        if kw == tk:
            p_scr[idx] = p
        else:
            p_scr[idx, :, 0:kw] = p
        a_scr[idx] = alpha
        l_scr[idx] = alpha * l_scr[idx] + psum
        m_scr[idx] = m_new

    def sweep(base, kw, last):
        q = q_ref[0]
        k = k_ref[0]
        vp = vp_ref[0]
        for h in range(DA_HEADS):
            for c in range(2):
                idx = 2 * h + c
                lo = h * dv + c * dh
                product(idx, vp[:, h * dv:(h + 1) * dv])
                logits(idx, q[:, lo:lo + dh], k[0:kw, lo:lo + dh], base)
        if not last:
            return
        vd = vd_ref[0]
        for idx in range(2 * DA_HEADS):
            product(idx, vd[0:kw, (idx // 2) * dv:(idx // 2 + 1) * dv])
        lam = _lambda(lq1_ref, lk1_ref, lq2_ref, lk2_ref, lam_init)
        sub = sub_ref[...]
        for h in range(DA_HEADS):
            inv1 = 1.0 / jnp.sum(l_scr[2 * h], axis=-1, keepdims=True)
            inv2 = 1.0 / jnp.sum(l_scr[2 * h + 1], axis=-1, keepdims=True)
            o = acc_scr[2 * h] * inv1 - lam * (acc_scr[2 * h + 1] * inv2)
            o_ref[0, :, h * dv:(h + 1) * dv] = (_rms(o, sub) * (1.0 - lam_init)).astype(o_ref.dtype)

    for vid, (base, kw, last) in enumerate(variants):
        pl.when(var == vid)(functools.partial(sweep, base, kw, last))

    @pl.when(j == 0)
    def _():
        acc_scr[...] = jnp.zeros_like(acc_scr)


def attn_prompt(q, k, v, tiles, lams, subln, *, tq, tk, lam_init):
    bsz, s, d = q.shape
    bt = tiles.shape[-1]
    assert tq % bt == 0 and tk % bt == 0 and s % tq == 0 and s % tk == 0
    dv = d // DA_HEADS
    qi, kj, var, variants = [], [], [], []
    for i in range(s // tq):
        jlast = ((i + 1) * tq - 1) // tk
        for j in range(jlast + 1):
            base = (tq // bt) * i - (tk // bt) * j
            kw = min(tk, (i + 1) * tq - j * tk)
            key = (base if base - (kw // bt - 1) <= 1 else None, kw, j == jlast)
            if key not in variants:
                variants.append(key)
            qi.append(i)
            kj.append(j)
            var.append(variants.index(key))
    n_steps = len(qi)
    qi, kj, var = (jnp.asarray(a, jnp.int32) for a in (qi, kj, var))
    dl = lams[0].shape[-1]
    const = lambda shape: pl.BlockSpec(shape, lambda b, st, qi, kj, var: (0,) * len(shape))
    grid_spec = pltpu.PrefetchScalarGridSpec(
        num_scalar_prefetch=3,
        grid=(bsz, n_steps),
        in_specs=[pl.BlockSpec((1, tq, d), lambda b, st, qi, kj, var: (b, qi[st], 0)),
                  pl.BlockSpec((1, tk, d), lambda b, st, qi, kj, var: (b, kj[st], 0)),
                  pl.BlockSpec((1, tk, d), lambda b, st, qi, kj, var: (b, jnp.maximum(kj[st] - 1, 0), 0)),
                  pl.BlockSpec((1, tk, d), lambda b, st, qi, kj, var: (b, ((qi[st] + 1) * tq - 1) // tk, 0)),
                  const(tiles.shape), const((1, dl)), const((1, dl)), const((1, dl)), const((1, dl)),
                  const((1, dv))],
        out_specs=pl.BlockSpec((1, tq, d), lambda b, st, qi, kj, var: (b, qi[st], 0)),
        scratch_shapes=[pltpu.VMEM((2 * DA_HEADS, tq, LANES), F32),
                        pltpu.VMEM((2 * DA_HEADS, tq, LANES), F32),
                        pltpu.VMEM((2 * DA_HEADS, tq, dv), F32),
                        pltpu.VMEM((2 * DA_HEADS, tq, tk), BF16),
                        pltpu.VMEM((2 * DA_HEADS, tq, LANES), F32)])
    return pl.pallas_call(
        functools.partial(_attn_prompt_kernel, lam_init=lam_init, variants=tuple(variants)),
        grid_spec=grid_spec,
        out_shape=jax.ShapeDtypeStruct((bsz, s, d), BF16),
        compiler_params=_cparams(("parallel", "arbitrary")),
        name="attn_prompt",
    )(qi, kj, var, q, k, v, v, tiles, *lams, subln.reshape(1, dv))


def _decode_fetch(b, pt_ref, ck_hbm, cv_hbm, kbuf, vbuf, sem, *, n_rows):
    bufs, n_pages = kbuf.shape[0], kbuf.shape[1]

    def page_copies(row, slot):
        cps = []
        for r in range(n_pages):
            pg = pt_ref[row, r]
            cps.append(pltpu.make_async_copy(ck_hbm.at[pg], kbuf.at[slot, r], sem.at[slot]))
            cps.append(pltpu.make_async_copy(cv_hbm.at[pg], vbuf.at[slot, r], sem.at[slot]))
        return cps

    @pl.when(b == 0)
    def _():
        for row in range(min(bufs - 1, n_rows)):
            for cp in page_copies(row, row):
                cp.start()

    @pl.when(b + (bufs - 1) < n_rows)
    def _():
        nxt = b + (bufs - 1)
        for cp in page_copies(nxt, nxt % bufs):
            cp.start()

    slot = b % bufs
    return ([kbuf.at[slot, r] for r in range(n_pages)], [vbuf.at[slot, r] for r in range(n_pages)],
            page_copies(b, slot))


def _decode_row(k_refs, v_refs, q_ref, kn_ref, vn_ref, dbias_ref, lq1_ref, lk1_ref, lq2_ref, lk2_ref, sub_ref,
                o_ref, *, lam_init):
    n_pages = len(k_refs)
    d = q_ref.shape[-1]
    dv = d // DA_HEADS
    dh = dv // 2
    nmap = 2 * DA_HEADS
    page = k_refs[0].shape[0]
    rows = page * DA_HEADS

    def rows2d(ref):
        return ref[...].reshape(rows, dv).astype(BF16)

    def head_rows(x):
        return jnp.concatenate([x[:, h * dv:(h + 1) * dv] for h in range(DA_HEADS)]
                               + [jnp.zeros((nmap - DA_HEADS, dv), x.dtype)], axis=0)

    def own_head(n):
        j = lax.broadcasted_iota(jnp.int32, (nmap, n), 0)
        r = lax.broadcasted_iota(jnp.int32, (nmap, n), 1)
        return r % DA_HEADS == j // 2

    q = q_ref[0].astype(F32)
    jrow = lax.broadcasted_iota(jnp.int32, (nmap, dv), 0)
    lane = lax.broadcasted_iota(jnp.int32, (nmap, dv), 1)
    qmat = jnp.zeros((nmap, dv), F32)
    for h in range(DA_HEADS):
        qh = jnp.broadcast_to(q[:, h * dv:(h + 1) * dv], (nmap, dv))
        qmat = jnp.where(jnp.logical_and(jrow // 2 == h, lane // dh == jrow % 2), qh, qmat)
    qmat = qmat.astype(BF16)

    kn8 = head_rows(kn_ref[0]).astype(BF16)
    vn8 = head_rows(vn_ref[0])
    s_new = lax.dot_general(qmat, kn8, _NT, preferred_element_type=F32) + dbias_ref[:, rows:rows + nmap]
    col = lax.broadcasted_iota(jnp.int32, (nmap, nmap), 1)
    s_new = jnp.where(jnp.logical_and(own_head(nmap), col < DA_HEADS), s_new, NEG_BIG)
    m = jnp.max(s_new, axis=1, keepdims=True)
    p_new = jnp.exp2(s_new - m)
    lsum = jnp.sum(p_new, axis=1, keepdims=True)
    acc = jnp.zeros((nmap, dv), F32)
    for h in range(DA_HEADS):
        acc = acc + p_new[:, h:h + 1] * vn8[h:h + 1, :]

    valid = own_head(rows)
    group = min(8, n_pages)
    assert n_pages % group == 0
    for g0 in range(0, n_pages, group):
        s_grp = []
        for r in range(g0, g0 + group):
            s = lax.dot_general(qmat, rows2d(k_refs[r]), _NT, preferred_element_type=F32)
            if r == n_pages - 1:
                s = s + dbias_ref[:, 0:rows]
            s_grp.append(jnp.where(valid, s, NEG_BIG))
        smax = s_grp[0]
        for s in s_grp[1:]:
            smax = jnp.maximum(smax, s)
        m_new = jnp.maximum(m, jnp.max(smax, axis=1, keepdims=True))
        alpha = jnp.exp2(m - m_new)
        psum = jnp.zeros((nmap, rows), F32)
        acc_g = jnp.zeros(acc.shape, F32)
        for r in range(g0, g0 + group):
            p = jnp.exp2(s_grp[r - g0] - m_new)
            psum = psum + p
            acc_g = acc_g + jnp.dot(p.astype(BF16), rows2d(v_refs[r]), preferred_element_type=F32)
        lsum = alpha * lsum + jnp.sum(psum, axis=1, keepdims=True)
        acc = alpha * acc + acc_g
        m = m_new
    an = acc * (1.0 / lsum)
    lam = _lambda(lq1_ref, lk1_ref, lq2_ref, lk2_ref, lam_init)
    sub = sub_ref[...]
    for h in range(DA_HEADS):
        o = an[2 * h:2 * h + 1, :] - lam * an[2 * h + 1:2 * h + 2, :]
        o_ref[0, :, h * dv:(h + 1) * dv] = (_rms(o, sub) * (1.0 - lam_init)).astype(o_ref.dtype)


def _attn_decode_kernel(pt_ref, q_ref, kn_ref, vn_ref, ck_hbm, cv_hbm, dbias_ref,
                        lq1_ref, lk1_ref, lq2_ref, lk2_ref, sub_ref, o_ref, kbuf, vbuf, sem, *, n_rows, lam_init):
    k_refs, v_refs, pending = _decode_fetch(pl.program_id(0), pt_ref, ck_hbm, cv_hbm, kbuf, vbuf, sem,
                                            n_rows=n_rows)
    for cp in pending:
        cp.wait()
    _decode_row(k_refs, v_refs, q_ref, kn_ref, vn_ref, dbias_ref, lq1_ref, lk1_ref, lq2_ref, lk2_ref, sub_ref,
                o_ref, lam_init=lam_init)


def attn_decode(q, k_new, v_new, cache_k, cache_v, page_table, dbias, lams, subln, *, lam_init):
    bsz, d = q.shape
    n_pages = page_table.shape[1]
    page = cache_k.shape[1]
    assert page == LANES
    dv = d // DA_HEADS
    dl = lams[0].shape[-1]
    row_spec = pl.BlockSpec((1, 1, d), lambda b, pt: (b, 0, 0))
    hbm_spec = pl.BlockSpec(memory_space=pl.ANY)
    lam_spec = pl.BlockSpec((1, dl), lambda b, pt: (0, 0))
    page_buf = pltpu.VMEM((DECODE_BUFS, n_pages, page, DA_HEADS, dv), cache_k.dtype)
    grid_spec = pltpu.PrefetchScalarGridSpec(
        num_scalar_prefetch=1,
        grid=(bsz,),
        in_specs=[row_spec, row_spec, row_spec, hbm_spec, hbm_spec,
                  pl.BlockSpec(dbias.shape, lambda b, pt: (0, 0)),
                  lam_spec, lam_spec, lam_spec, lam_spec,
                  pl.BlockSpec((1, dv), lambda b, pt: (0, 0))],
        out_specs=pl.BlockSpec((1, 1, d), lambda b, pt: (b, 0, 0)),
        scratch_shapes=[page_buf, page_buf, pltpu.SemaphoreType.DMA((DECODE_BUFS,))])
    r3 = lambda a: a.reshape(bsz, 1, d)
    out = pl.pallas_call(
        functools.partial(_attn_decode_kernel, n_rows=bsz, lam_init=lam_init),
        grid_spec=grid_spec,
        out_shape=jax.ShapeDtypeStruct((bsz, 1, d), BF16),
        compiler_params=_cparams(("arbitrary",)),
        name="attn_decode",
    )(page_table, r3(q), r3(k_new), r3(v_new), cache_k, cache_v, dbias, *lams, subln.reshape(1, dv))
    return out.reshape(bsz, d)


def _hgrn_decode_kernel(pt_ref, q_ref, f_ref, i_ref, g_ref, lb_ref, gn_ref, mall_ref,
                        dq_ref, kn_ref, vn_ref, ck_hbm, cv_hbm, dbias_ref, lq1_ref, lk1_ref, lq2_ref, lk2_ref,
                        sub_ref, o_ref, st_ref, do_ref, s_scr, kbuf, vbuf, sem,
                        *, C, levels, nsub, n_rows, lam_init):
    cidx = pl.program_id(1)
    row = pl.program_id(0) * pl.num_programs(1) + cidx
    k_refs, v_refs, pending = _decode_fetch(row, pt_ref, ck_hbm, cv_hbm, kbuf, vbuf, sem, n_rows=n_rows)
    _hgrn_chunks(q_ref, f_ref, i_ref, g_ref, lb_ref, gn_ref, mall_ref, o_ref, st_ref, s_scr,
                 first=cidx == 0, last=cidx == pl.num_programs(1) - 1, C=C, levels=levels, nsub=nsub)
    for cp in pending:
        cp.wait()
    _decode_row(k_refs, v_refs, dq_ref, kn_ref, vn_ref, dbias_ref, lq1_ref, lk1_ref, lq2_ref, lk2_ref, sub_ref,
                do_ref, lam_init=lam_init)


def hgrn_prompt_decode(q, f, i, g, lb, gnorm, dq, k_new, v_new, cache_k, cache_v, page_table, dbias, lams, subln,
                       *, C, nsub, lam_init):
    bsz, s, d = q.shape
    n_rows = dq.shape[0]
    dk = d // HG_HEADS
    dv = d // DA_HEADS
    tb = C * nsub
    steps = s // tb
    assert bsz * steps == n_rows
    n_pages = page_table.shape[1]
    page = cache_k.shape[1]
    mall_np, levels = _hgrn_sum_matrices(C)
    mall = jnp.asarray(mall_np, dtype=BF16)
    dl = lams[0].shape[-1]
    blk = pl.BlockSpec((1, tb, d), lambda b, c, pt: (b, c, 0))
    const = lambda shape: pl.BlockSpec(shape, lambda b, c, pt: (0,) * len(shape))
    row_spec = pl.BlockSpec((1, 1, d), lambda b, c, pt: (b * steps + c, 0, 0))
    hbm_spec = pl.BlockSpec(memory_space=pl.ANY)
    page_buf = pltpu.VMEM((2, n_pages, page, DA_HEADS, dv), cache_k.dtype)
    grid_spec = pltpu.PrefetchScalarGridSpec(
        num_scalar_prefetch=1,
        grid=(bsz, steps),
        in_specs=[blk, blk, blk, blk, const((1, d)), const((1, d)), const(mall.shape),
                  row_spec, row_spec, row_spec, hbm_spec, hbm_spec, const(dbias.shape),
                  const((1, dl)), const((1, dl)), const((1, dl)), const((1, dl)), const((1, dv))],
        out_specs=[pl.BlockSpec((1, tb, d), lambda b, c, pt: (b, c, 0)),
                   pl.BlockSpec((1, 1, HG_HEADS, dk, dk), lambda b, c, pt: (0, b, 0, 0, 0)),
                   row_spec],
        scratch_shapes=[pltpu.VMEM((HG_HEADS, dk, dk), F32), page_buf, page_buf,
                        pltpu.SemaphoreType.DMA((2,))])
    r3 = lambda a: a.reshape(n_rows, 1, d)
    o, state, o_dec = pl.pallas_call(
        functools.partial(_hgrn_decode_kernel, C=C, levels=levels, nsub=nsub, n_rows=n_rows, lam_init=lam_init),
        grid_spec=grid_spec,
        out_shape=[jax.ShapeDtypeStruct((bsz, s, d), BF16),
                   jax.ShapeDtypeStruct((1, bsz, HG_HEADS, dk, dk), F32),
                   jax.ShapeDtypeStruct((n_rows, 1, d), BF16)],
        compiler_params=_cparams(("arbitrary", "arbitrary")),
        name="hgrn_prompt_decode",
    )(page_table, q, f, i, g, lb, gnorm.reshape(1, d), mall, r3(dq), r3(k_new), r3(v_new), cache_k, cache_v,
      dbias, *lams, subln.reshape(1, dv))
    return o, state, o_dec.reshape(n_rows, d)


DECODE_BUFS = 3
ATTN_TQ = 512
ATTN_TK = 512
HGRN_C = 64
HGRN_NSUB = 8
TM = 512


def _in_proj(x, p):
    return proj(x, [(p["norm_mix"][0], (p["w_in_a"], 0), 1.0, [[(F32, 0)]] * 4)], tm=TM, name="in_proj")


def _after_mixer0(o, x, p, scale):
    x = mix_ffn(o, x, p["w_out_a"], p["norm_ffn"][0], p["w_gate_up"], p["w_down"], p["norm_final"],
                layer=0, tm=TM, final_norm=False, name="mix_ffn0")
    k32, k16, v32, v16, qa = proj(
        x, [(p["kv_norm"], (p["w_kv"], 0), 1.0, [[(F32, DA_HEADS), (BF16, 0)]] * 2),
            (p["norm_mix"][1], (p["w_q_b"], 0), scale * LOG2E, [[(BF16, 0)]])],
        tm=TM, name="kvq_proj")
    return x, k32, k16, v32, v16, qa


def _after_mixer1(o, x, p):
    return mix_ffn(o, x, p["w_out_b"], p["norm_ffn"][1], p["w_gate_up"], p["w_down"], p["norm_final"],
                   layer=1, tm=TM, final_norm=True, name="mix_ffn1")


def kernel(x_prompt, x_sample, cache_k, cache_v, state_hgrn, page_table, w_in_a, lower_bound, gnorm_a, w_out_a,
           w_q_b, lambda_q1, lambda_k1, lambda_q2, lambda_k2, subln_b, w_out_b, kv_norm, w_kv, rel_bias,
           norm_mix, norm_ffn, w_gate_up, w_down, norm_final):
    bf = lambda w: (w if w.ndim == 3 else w[None]).astype(BF16)
    p = dict(w_in_a=w_in_a, w_out_a=bf(w_out_a), w_q_b=bf(w_q_b), w_out_b=bf(w_out_b), kv_norm=kv_norm,
             w_kv=bf(w_kv), norm_mix=norm_mix, norm_ffn=norm_ffn, w_gate_up=bf(w_gate_up), w_down=w_down,
             norm_final=norm_final)
    bsz, s, d = x_prompt.shape
    nb = x_sample.shape[0]
    dv = d // DA_HEADS
    scale = (dv // 2) ** -0.5
    lam_init = 0.8 - 0.6 * math.exp(-0.3 * 1)
    lams = [a[0].reshape(1, -1) for a in (lambda_q1, lambda_k1, lambda_q2, lambda_k2)]
    gnorm, subln = gnorm_a[0], subln_b[0]
    tiles, dbias, lb = param_tables(rel_bias, lower_bound, cache_k.shape[1])
    xp = x_prompt.reshape(bsz * s, d)
    xs = x_sample.reshape(nb, d)
    r3 = lambda a: a.reshape(bsz, s, d)

    q, f, i, g = _in_proj(xs, p)
    o_s, st_s = hgrn_step(q, f, i, g, lb, gnorm, state_hgrn, bb=16)
    xs, k_s, _, v_s, _, qa_s = _after_mixer0(o_s, xs, p, scale)

    q, f, i, g = _in_proj(xp, p)
    tokens_per_row = (bsz * s) // nb
    if (bsz * s) % nb == 0 and tokens_per_row % HGRN_C == 0 and s % tokens_per_row == 0:
        o_p, st_p, o_dec = hgrn_prompt_decode(
            r3(q), r3(f), r3(i), r3(g), lb, gnorm, qa_s, k_s.reshape(nb, d), v_s.reshape(nb, d), cache_k, cache_v,
            page_table, dbias, lams, subln, C=HGRN_C, nsub=tokens_per_row // HGRN_C, lam_init=lam_init)
    else:
        o_p, st_p = hgrn_prompt(r3(q), r3(f), r3(i), r3(g), lb, gnorm, C=HGRN_C, nsub=HGRN_NSUB)
        o_dec = attn_decode(qa_s, k_s.reshape(nb, d), v_s.reshape(nb, d), cache_k, cache_v, page_table, dbias,
                            lams, subln, lam_init=lam_init)
    xp, k_p, k16, v_p, v16, qa_p = _after_mixer0(o_p.reshape(bsz * s, d), xp, p, scale)

    o_a = attn_prompt(r3(qa_p), r3(k16), r3(v16), tiles, lams, subln,
                      tq=min(ATTN_TQ, s), tk=min(ATTN_TK, s), lam_init=lam_init)
    y_p = _after_mixer1(o_a.reshape(bsz * s, d), xp, p)
    y_s = _after_mixer1(o_dec, xs, p)
    hk = DA_HEADS, dv
    return (y_p.reshape(bsz, s, d), y_s.reshape(nb, 1, d),
            k_p.reshape(bsz, s, *hk), v_p.reshape(bsz, s, *hk), st_p,
            k_s.reshape(nb, 1, *hk), v_s.reshape(nb, 1, *hk), st_s)
```

```python
import functools
import math

import numpy as np
import jax
import jax.numpy as jnp
from jax import lax
from jax.experimental import pallas as pl
from jax.experimental.pallas import tpu as pltpu

F32 = jnp.float32
BF16 = jnp.bfloat16
EPS = 1e-6
NEG_BIG = -1e30
LOG2E = math.log2(math.e)

HG_HEADS = 8
DA_HEADS = 4
N_BUCKETS = 32
MAX_DISTANCE = 128
LANES = 128
SUBLANES = 8
MXU_N = 256
VMEM_LIMIT = 56 * 1024 * 1024

_NT = (((1,), (1,)), ((), ()))
_TN = (((0,), (0,)), ((), ()))


def _cparams(sem):
    return pltpu.CompilerParams(dimension_semantics=sem, vmem_limit_bytes=VMEM_LIMIT)


def _resident(shape):
    return pl.BlockSpec(shape, lambda *_: (0,) * len(shape), pipeline_mode=pl.Buffered(1))


def _resident_layer(w, layer):
    return pl.BlockSpec((1,) + w.shape[1:], lambda *_: (layer, 0, 0), pipeline_mode=pl.Buffered(1))


def _sigmoid(x):
    return 1.0 / (1.0 + jnp.exp(-x))


def _silu(x):
    return x * _sigmoid(x)


def _rms(x, g):
    ms = jnp.mean(x * x, axis=-1, keepdims=True)
    return x * lax.rsqrt(ms + EPS) * g


def _lane_tile(x, reps):
    return x if reps == 1 else jnp.concatenate([x] * reps, axis=1)


def _proj_kernel(x_ref, *refs, branches):
    nb = len(branches)
    g_refs = refs[:nb]
    w_refs = refs[nb:2 * nb]
    out_refs = refs[2 * nb:]
    x = x_ref[...]
    inv = lax.rsqrt(jnp.mean(x * x, axis=-1, keepdims=True) + EPS)
    o = 0
    for bi, (scale, groups) in enumerate(branches):
        xn = (x * inv * g_refs[bi][...]).astype(BF16)
        ng = w_refs[bi].shape[2] // len(groups)
        for gi, outs in enumerate(groups):
            acc = jnp.dot(xn, w_refs[bi][0, :, gi * ng:(gi + 1) * ng].astype(BF16), preferred_element_type=F32)
            if scale != 1.0:
                acc = acc * scale
            for _ in outs:
                ref = out_refs[o]
                if len(ref.shape) == 3:
                    hd = ref.shape[2]
                    for hh in range(ref.shape[1]):
                        ref[:, hh, :] = acc[:, hh * hd:(hh + 1) * hd].astype(ref.dtype)
                else:
                    ref[...] = acc.astype(ref.dtype)
                o += 1


def proj(x, branches, *, tm, name):
    m, d = x.shape
    tm = min(tm, m)
    gains = [b[0].reshape(1, d) for b in branches]
    weights = [b[1] for b in branches]
    out_shape, out_specs = [], []
    for _, (w, _), _, groups in branches:
        ng = w.shape[2] // len(groups)
        for outs in groups:
            for dt, heads in outs:
                if heads:
                    out_shape.append(jax.ShapeDtypeStruct((m, heads, ng // heads), dt))
                    out_specs.append(pl.BlockSpec((tm, heads, ng // heads), lambda i: (i, 0, 0)))
                else:
                    out_shape.append(jax.ShapeDtypeStruct((m, ng), dt))
                    out_specs.append(pl.BlockSpec((tm, ng), lambda i: (i, 0)))
    return pl.pallas_call(
        functools.partial(_proj_kernel, branches=[(b[2], b[3]) for b in branches]),
        grid=(m // tm,),
        in_specs=([pl.BlockSpec((tm, d), lambda i: (i, 0))]
                  + [_resident(g.shape) for g in gains] + [_resident_layer(w, l) for w, l in weights]),
        out_specs=out_specs,
        out_shape=out_shape,
        compiler_params=_cparams(("parallel",)),
        name=name,
    )(x, *gains, *[w for w, _ in weights])


def _mix_ffn_kernel(a_ref, x_ref, wo_ref, g_ref, wgu_ref, wd_ref, gf_ref, o_ref, x1_scr, hid_scr, *, final_norm):
    dff = wd_ref.shape[1]
    x1 = x_ref[...] + jnp.dot(a_ref[...], wo_ref[0], preferred_element_type=F32)
    x1_scr[...] = x1
    xn = _rms(x1, g_ref[...]).astype(BF16)
    for j in range(dff // MXU_N):
        cs = slice(j * MXU_N, (j + 1) * MXU_N)
        gt = jnp.dot(xn, wgu_ref[0, :, cs].astype(BF16), preferred_element_type=F32)
        ut = jnp.dot(xn, wgu_ref[0, :, dff + j * MXU_N:dff + (j + 1) * MXU_N].astype(BF16),
                     preferred_element_type=F32)
        hid_scr[:, cs] = (_silu(gt) * ut).astype(BF16)
    y = x1_scr[...] + jnp.dot(hid_scr[...], wd_ref[0].astype(BF16), preferred_element_type=F32)
    if final_norm:
        y = _rms(y, gf_ref[...])
    o_ref[...] = y


def mix_ffn(a, x, w_out, g, w_gate_up, w_down, g_final, *, layer, tm, final_norm, name):
    m, d = x.shape
    dff = w_down.shape[1]
    assert dff % MXU_N == 0
    tm = min(tm, m)
    row = lambda width: pl.BlockSpec((tm, width), lambda i: (i, 0))
    return pl.pallas_call(
        functools.partial(_mix_ffn_kernel, final_norm=final_norm),
        grid=(m // tm,),
        in_specs=[row(a.shape[1]), row(d), _resident_layer(w_out, 0), _resident((1, d)),
                  _resident_layer(w_gate_up, layer), _resident_layer(w_down, layer), _resident((1, d))],
        out_specs=row(d),
        out_shape=jax.ShapeDtypeStruct((m, d), F32),
        scratch_shapes=[pltpu.VMEM((tm, d), F32), pltpu.VMEM((tm, dff), BF16)],
        compiler_params=_cparams(("parallel",)),
        name=name,
    )(a, x, w_out, g.reshape(1, d), w_gate_up, w_down, g_final.reshape(1, d))


BIAS_T = 2 * MAX_DISTANCE


def _bucket_starts():
    max_exact = N_BUCKETS // 2
    n = np.arange(2 * MAX_DISTANCE, dtype=np.int32)
    nf = np.maximum(n, 1).astype(np.float32)
    large = max_exact + (np.log(nf / np.float32(max_exact)) / np.float32(math.log(MAX_DISTANCE / max_exact))
                         * np.float32(N_BUCKETS - max_exact)).astype(np.int32)
    bucket = np.where(n < max_exact, n, np.minimum(large, N_BUCKETS - 1))
    assert (np.diff(bucket) >= 0).all() and (np.diff(bucket) <= 1).all() and bucket[-1] == N_BUCKETS - 1
    return [int(np.argmax(bucket >= b)) for b in range(N_BUCKETS)]


def _bucket_bias(n, rb_ref, n_heads):
    starts = _bucket_starts()
    reached = [n >= starts[b] for b in range(1, N_BUCKETS - 1)]
    outs = []
    for h in range(n_heads):
        far = rb_ref[N_BUCKETS - 1, h]
        acc = jnp.full(n.shape, (rb_ref[0, h] - far) * LOG2E, F32)
        for b in range(1, N_BUCKETS - 1):
            acc = jnp.where(reached[b - 1], (rb_ref[b, h] - far) * LOG2E, acc)
        outs.append(jnp.where(n >= starts[N_BUCKETS - 1], 0.0, acc))
    return outs


def _tables_kernel(rb_ref, lb_in_ref, tiles_ref, dec_ref, lb_ref):
    bt = tiles_ref.shape[-1]
    r = lax.broadcasted_iota(jnp.int32, (bt, bt), 0)
    c = lax.broadcasted_iota(jnp.int32, (bt, bt), 1)
    n0 = r - c
    diag = _bucket_bias(jnp.maximum(n0, 0), rb_ref, DA_HEADS)
    prev = _bucket_bias(bt + n0, rb_ref, DA_HEADS)
    for h in range(DA_HEADS):
        tiles_ref[0, h] = jnp.where(n0 >= 0, diag[h], NEG_BIG)
        tiles_ref[1, h] = prev[h]
    page_cols = dec_ref.shape[1] - LANES
    rr = lax.broadcasted_iota(jnp.int32, dec_ref.shape, 0)
    cc = lax.broadcasted_iota(jnp.int32, dec_ref.shape, 1)
    nd = jnp.where(cc < page_cols, page_cols // DA_HEADS - cc // DA_HEADS, 0)
    dec = _bucket_bias(nd, rb_ref, DA_HEADS)
    acc = jnp.zeros(dec_ref.shape, F32)
    for h in range(DA_HEADS):
        acc = jnp.where(rr // 2 == h, dec[h], acc)
    dec_ref[...] = acc
    lbi = lb_in_ref[...]
    mx = jnp.max(lbi, axis=0, keepdims=True)
    e = jnp.exp(lbi - mx)
    lb_ref[...] = e[0:1, :] / jnp.sum(e, axis=0, keepdims=True)


def param_tables(rel_bias, lower_bound, page):
    d = lower_bound.shape[1]
    return pl.pallas_call(
        _tables_kernel,
        in_specs=[pl.BlockSpec(memory_space=pltpu.SMEM),
                  pl.BlockSpec(memory_space=pltpu.VMEM)],
        out_specs=[pl.BlockSpec(memory_space=pltpu.VMEM)] * 3,
        out_shape=[jax.ShapeDtypeStruct((2, DA_HEADS, BIAS_T, BIAS_T), F32),
                   jax.ShapeDtypeStruct((2 * DA_HEADS, page * DA_HEADS + LANES), F32),
                   jax.ShapeDtypeStruct((1, d), F32)],
        compiler_params=pltpu.CompilerParams(vmem_limit_bytes=VMEM_LIMIT),
        name="param_tables",
    )(rel_bias, lower_bound)


def _hgrn_sum_matrices(C):
    levels = int(math.log2(C))
    t = np.arange(C)
    u = t[None, :]
    mats = [u <= t[:, None], u > t[:, None]]
    for lv in range(levels):
        c = 1 << lv
        e = (t // (2 * c)) * (2 * c) + c - 1
        upper = (t > e)[:, None]
        seg = np.where(upper, (u > e[:, None]) & (u <= t[:, None]), (u > t[:, None]) & (u <= e[:, None]))
        mats.append(seg)
    return np.concatenate(mats, axis=0).astype(np.float32), levels


def _split2(x):
    hi = x.astype(BF16)
    lo = (x - hi.astype(F32)).astype(BF16)
    return hi, lo


def _hgrn_chunks(q_ref, f_ref, i_ref, g_ref, lb_ref, gn_ref, mall_ref, o_ref, st_ref, s_scr,
                 *, first, last, C, levels, nsub):
    dk = s_scr.shape[-1]

    @pl.when(first)
    def _():
        s_scr[...] = jnp.zeros_like(s_scr)

    lb = lb_ref[...]
    gn = gn_ref[...]
    mall = mall_ref[...]
    ti = lax.broadcasted_iota(jnp.int32, (C, C), 0)
    si = lax.broadcasted_iota(jnp.int32, (C, C), 1)
    txs = jnp.bitwise_xor(ti, si)
    lower = ti > si
    valid = [jnp.logical_and(jnp.right_shift(txs, lv) == 1, lower) for lv in range(levels)]
    trow = lax.broadcasted_iota(jnp.int32, (C, q_ref.shape[-1]), 0)
    upper = [jnp.bitwise_and(jnp.right_shift(trow, lv), 1) == 1 for lv in range(levels)]

    for sub in range(nsub):
        rows = slice(sub * C, (sub + 1) * C)
        fg_all = lb + (1.0 - lb) * _sigmoid(f_ref[0, rows, :])
        hi, lo = _split2(jnp.log(fg_all) * LOG2E)
        sums = jnp.dot(mall, hi, preferred_element_type=F32)
        ends = sums[0:2 * C] + jnp.dot(mall[0:2 * C], lo, preferred_element_type=F32)

        q = _silu(q_ref[0, rows, :])
        kk = 1.0 - fg_all
        v = i_ref[0, rows, :]
        v_bf = v.astype(BF16)
        b = ends[0:C]
        suf = ends[C:2 * C]
        qe = (q * jnp.exp2(b)).astype(BF16)
        kt = (kk * jnp.exp2(suf)).astype(BF16)
        dec_last = jnp.exp2(b[C - 1:C, :])

        scores = [jnp.zeros((C, C), F32) for _ in range(HG_HEADS)]
        for lv in range(levels):
            e = jnp.exp2(sums[(2 + lv) * C:(3 + lv) * C])
            if (1 << lv) >= SUBLANES:
                role = jnp.concatenate([(q if (r >> lv) & 1 else kk)[r:r + SUBLANES]
                                        for r in range(0, C, SUBLANES)], axis=0)
            else:
                role = jnp.where(upper[lv], q, kk)
            x_bf = (role * e).astype(BF16)
            for h in range(HG_HEADS):
                hs = slice(h * dk, (h + 1) * dk)
                sl = lax.dot_general(x_bf[:, hs], x_bf[:, hs], _NT, preferred_element_type=F32)
                scores[h] = jnp.where(valid[lv], sl, scores[h])

        qk = q * kk
        g = g_ref[0, rows, :]
        for h in range(HG_HEADS):
            hs = slice(h * dk, (h + 1) * dk)
            st = s_scr[h]
            diag = jnp.sum(qk[:, hs], axis=-1, keepdims=True)
            o = (jnp.dot(scores[h].astype(BF16), v_bf[:, hs], preferred_element_type=F32)
                 + diag * v[:, hs]
                 + lax.dot_general(qe[:, hs], st.astype(BF16), _NT, preferred_element_type=F32))
            s_scr[h] = st * dec_last[:, hs] + lax.dot_general(v_bf[:, hs], kt[:, hs], _TN,
                                                              preferred_element_type=F32)
            o_ref[0, rows, hs] = (_rms(o, gn[:, hs]) * _silu(g[:, hs])).astype(o_ref.dtype)

    @pl.when(last)
    def _():
        for h in range(HG_HEADS):
            st_ref[0, 0, h] = s_scr[h].T


def _hgrn_prompt_kernel(*refs, C, levels, nsub):
    cidx = pl.program_id(1)
    _hgrn_chunks(*refs, first=cidx == 0, last=cidx == pl.num_programs(1) - 1, C=C, levels=levels, nsub=nsub)


def hgrn_prompt(q, f, i, g, lb, gnorm, *, C, nsub):
    bsz, s, d = q.shape
    dk = d // HG_HEADS
    mall_np, levels = _hgrn_sum_matrices(C)
    mall = jnp.asarray(mall_np, dtype=BF16)
    tb = C * nsub
    blk = pl.BlockSpec((1, tb, d), lambda b, c: (b, c, 0))
    vec = pl.BlockSpec((1, d), lambda b, c: (0, 0))
    return pl.pallas_call(
        functools.partial(_hgrn_prompt_kernel, C=C, levels=levels, nsub=nsub),
        grid=(bsz, s // tb),
        in_specs=[blk, blk, blk, blk, vec, vec,
                  pl.BlockSpec(mall.shape, lambda b, c: (0, 0))],
        out_specs=[pl.BlockSpec((1, tb, d), lambda b, c: (b, c, 0)),
                   pl.BlockSpec((1, 1, HG_HEADS, dk, dk), lambda b, c: (0, b, 0, 0, 0))],
        out_shape=[jax.ShapeDtypeStruct((bsz, s, d), BF16),
                   jax.ShapeDtypeStruct((1, bsz, HG_HEADS, dk, dk), F32)],
        scratch_shapes=[pltpu.VMEM((HG_HEADS, dk, dk), F32)],
        compiler_params=_cparams(("parallel", "arbitrary")),
        name="hgrn_prompt",
    )(q, f, i, g, lb, gnorm.reshape(1, d), mall)


def _hgrn_step_kernel(q_ref, f_ref, i_ref, g_ref, lb_ref, gn_ref, s_ref, o_ref, so_ref):
    rows = q_ref.shape[0]
    lb = lb_ref[...]
    fg = lb + (1.0 - lb) * _sigmoid(f_ref[...])
    q_t = _silu(q_ref[...]).T
    fg_t = fg.T
    kk_t = (1.0 - fg).T
    v = i_ref[...]
    outs = []
    for r in range(rows):
        bl, h = divmod(r, HG_HEADS)
        s_new = fg_t[:, r:r + 1] * s_ref[0, bl, h] + kk_t[:, r:r + 1] * v[r:r + 1, :]
        so_ref[0, bl, h] = s_new
        outs.append(jnp.sum(q_t[:, r:r + 1] * s_new, axis=0, keepdims=True))
    o = jnp.concatenate(outs, axis=0)
    o_ref[...] = (_rms(o, gn_ref[...]) * _silu(g_ref[...])).astype(o_ref.dtype)


def hgrn_step(q, f, i, g, lb, gnorm, state, *, bb):
    bsz, d = q.shape
    dk = d // HG_HEADS
    rows = bb * HG_HEADS
    to_rows = lambda a: a.reshape(bsz * HG_HEADS, dk)
    tile = lambda p: jnp.tile(p.reshape(HG_HEADS, dk), (bb, 1))
    rblk = pl.BlockSpec((rows, dk), lambda b: (b, 0))
    pblk = pl.BlockSpec((rows, dk), lambda b: (0, 0))
    sblk = pl.BlockSpec((1, bb, HG_HEADS, dk, dk), lambda b: (0, b, 0, 0, 0))
    o, s_new = pl.pallas_call(
        _hgrn_step_kernel,
        grid=(bsz // bb,),
        in_specs=[rblk, rblk, rblk, rblk, pblk, pblk, sblk],
        out_specs=[rblk, sblk],
        out_shape=[jax.ShapeDtypeStruct((bsz * HG_HEADS, dk), BF16),
                   jax.ShapeDtypeStruct(state.shape, F32)],
        compiler_params=_cparams(("parallel",)),
        name="hgrn_step",
    )(to_rows(q), to_rows(f), to_rows(i), to_rows(g), tile(lb), tile(gnorm), state)
    return o.reshape(bsz, d), s_new


def _lambda(lq1_ref, lk1_ref, lq2_ref, lk2_ref, lam_init):
    s1 = jnp.sum(lq1_ref[...] * lk1_ref[...], axis=-1, keepdims=True)
    s2 = jnp.sum(lq2_ref[...] * lk2_ref[...], axis=-1, keepdims=True)
    return jnp.exp(s1) - jnp.exp(s2) + lam_init


def _attn_prompt_kernel(qi_ref, kj_ref, var_ref, q_ref, k_ref, vp_ref, vd_ref, tiles_ref,
                        lq1_ref, lk1_ref, lq2_ref, lk2_ref, sub_ref, o_ref,
                        m_scr, l_scr, acc_scr, p_scr, a_scr, *, lam_init, variants):
    step = pl.program_id(1)
    j = kj_ref[step]
    var = var_ref[step]
    tq = q_ref.shape[1]
    tk = k_ref.shape[1]
    bt = tiles_ref.shape[-1]
    dv = acc_scr.shape[-1]
    dh = dv // 2

    @pl.when(j == 0)
    def _():
        m_scr[...] = jnp.full_like(m_scr, NEG_BIG)
        l_scr[...] = jnp.zeros_like(l_scr)

    @pl.when(step == 0)
    def _():
        acc_scr[...] = jnp.zeros_like(acc_scr)
        p_scr[...] = jnp.zeros_like(p_scr)
        a_scr[...] = jnp.zeros_like(a_scr)

    def add_bias(h, s, base):
        rows = []
        for r in range(tq // bt):
            cols = []
            for c in range(s.shape[1] // bt):
                dist = base + r - c
                blk = s[r * bt:(r + 1) * bt, c * bt:(c + 1) * bt]
                if dist < 0:
                    blk = jnp.full((bt, bt), NEG_BIG, F32)
                elif dist <= 1:
                    blk = blk + tiles_ref[dist, h]
                cols.append(blk)
            rows.append(jnp.concatenate(cols, axis=1))
        return jnp.concatenate(rows, axis=0)

    def product(idx, vh):
        kw = vh.shape[0]
        p = p_scr[idx] if kw == tk else p_scr[idx, :, 0:kw]
        acc_scr[idx] = (_lane_tile(a_scr[idx], dv // LANES) * acc_scr[idx]
                        + jnp.dot(p, vh, preferred_element_type=F32))

    def logits(idx, qh, kh, base):
        kw = kh.shape[0]
        s = lax.dot_general(qh, kh, _NT, preferred_element_type=F32)
        if base is not None:
            s = add_bias(idx // 2, s, base)
        m_prev = m_scr[idx]
        m_new = jnp.maximum(m_prev, jnp.max(s, axis=-1, keepdims=True))
        alpha = jnp.exp2(m_prev - m_new)
        chunks = [jnp.exp2(s[:, t * LANES:(t + 1) * LANES] - m_new) for t in range(kw // LANES)]
        psum = chunks[0]
        for ch in chunks[1:]:
            psum = psum + ch
        p = jnp.concatenate([ch.astype(BF16) for ch in chunks], axis=1)
        if kw == tk:
            p_scr[idx] = p
        else:
            p_scr[idx, :, 0:kw] = p
        a_scr[idx] = alpha
        l_scr[idx] = alpha * l_scr[idx] + psum
        m_scr[idx] = m_new

    def sweep(base, kw, last):
        q = q_ref[0]
        k = k_ref[0]
        vp = vp_ref[0]
        for h in range(DA_HEADS):
            for c in range(2):
                idx = 2 * h + c
                lo = h * dv + c * dh
                product(idx, vp[:, h * dv:(h + 1) * dv])
                logits(idx, q[:, lo:lo + dh], k[0:kw, lo:lo + dh], base)
        if not last:
            return
        vd = vd_ref[0]
        for idx in range(2 * DA_HEADS):
            product(idx, vd[0:kw, (idx // 2) * dv:(idx // 2 + 1) * dv])
        lam = _lambda(lq1_ref, lk1_ref, lq2_ref, lk2_ref, lam_init)
        sub = sub_ref[...]
        for h in range(DA_HEADS):
            inv1 = 1.0 / jnp.sum(l_scr[2 * h], axis=-1, keepdims=True)
            inv2 = 1.0 / jnp.sum(l_scr[2 * h + 1], axis=-1, keepdims=True)
            o = acc_scr[2 * h] * inv1 - lam * (acc_scr[2 * h + 1] * inv2)
            o_ref[0, :, h * dv:(h + 1) * dv] = (_rms(o, sub) * (1.0 - lam_init)).astype(o_ref.dtype)

    for vid, (base, kw, last) in enumerate(variants):
        pl.when(var == vid)(functools.partial(sweep, base, kw, last))

    @pl.when(j == 0)
    def _():
        acc_scr[...] = jnp.zeros_like(acc_scr)


def attn_prompt(q, k, v, tiles, lams, subln, *, tq, tk, lam_init):
    bsz, s, d = q.shape
    bt = tiles.shape[-1]
    assert tq % bt == 0 and tk % bt == 0 and s % tq == 0 and s % tk == 0
    dv = d // DA_HEADS
    qi, kj, var, variants = [], [], [], []
    for i in range(s // tq):
        jlast = ((i + 1) * tq - 1) // tk
        for j in range(jlast + 1):
            base = (tq // bt) * i - (tk // bt) * j
            kw = min(tk, (i + 1) * tq - j * tk)
            key = (base if base - (kw // bt - 1) <= 1 else None, kw, j == jlast)
            if key not in variants:
                variants.append(key)
            qi.append(i)
            kj.append(j)
            var.append(variants.index(key))
    n_steps = len(qi)
    qi, kj, var = (jnp.asarray(a, jnp.int32) for a in (qi, kj, var))
    dl = lams[0].shape[-1]
    const = lambda shape: pl.BlockSpec(shape, lambda b, st, qi, kj, var: (0,) * len(shape))
    grid_spec = pltpu.PrefetchScalarGridSpec(
        num_scalar_prefetch=3,
        grid=(bsz, n_steps),
        in_specs=[pl.BlockSpec((1, tq, d), lambda b, st, qi, kj, var: (b, qi[st], 0)),
                  pl.BlockSpec((1, tk, d), lambda b, st, qi, kj, var: (b, kj[st], 0)),
                  pl.BlockSpec((1, tk, d), lambda b, st, qi, kj, var: (b, jnp.maximum(kj[st] - 1, 0), 0)),
                  pl.BlockSpec((1, tk, d), lambda b, st, qi, kj, var: (b, ((qi[st] + 1) * tq - 1) // tk, 0)),
                  const(tiles.shape), const((1, dl)), const((1, dl)), const((1, dl)), const((1, dl)),
                  const((1, dv))],
        out_specs=pl.BlockSpec((1, tq, d), lambda b, st, qi, kj, var: (b, qi[st], 0)),
        scratch_shapes=[pltpu.VMEM((2 * DA_HEADS, tq, LANES), F32),
                        pltpu.VMEM((2 * DA_HEADS, tq, LANES), F32),
                        pltpu.VMEM((2 * DA_HEADS, tq, dv), F32),
                        pltpu.VMEM((2 * DA_HEADS, tq, tk), BF16),
                        pltpu.VMEM((2 * DA_HEADS, tq, LANES), F32)])
    return pl.pallas_call(
        functools.partial(_attn_prompt_kernel, lam_init=lam_init, variants=tuple(variants)),
        grid_spec=grid_spec,
        out_shape=jax.ShapeDtypeStruct((bsz, s, d), BF16),
        compiler_params=_cparams(("parallel", "arbitrary")),
        name="attn_prompt",
    )(qi, kj, var, q, k, v, v, tiles, *lams, subln.reshape(1, dv))


def _decode_fetch(b, pt_ref, ck_hbm, cv_hbm, kbuf, vbuf, sem, *, n_rows):
    bufs, n_pages = kbuf.shape[0], kbuf.shape[1]

    def page_copies(row, slot):
        cps = []
        for r in range(n_pages):
            pg = pt_ref[row, r]
            cps.append(pltpu.make_async_copy(ck_hbm.at[pg], kbuf.at[slot, r], sem.at[slot]))
            cps.append(pltpu.make_async_copy(cv_hbm.at[pg], vbuf.at[slot, r], sem.at[slot]))
        return cps

    @pl.when(b == 0)
    def _():
        for row in range(min(bufs - 1, n_rows)):
            for cp in page_copies(row, row):
                cp.start()

    @pl.when(b + (bufs - 1) < n_rows)
    def _():
        nxt = b + (bufs - 1)
        for cp in page_copies(nxt, nxt % bufs):
            cp.start()

    slot = b % bufs
    return ([kbuf.at[slot, r] for r in range(n_pages)], [vbuf.at[slot, r] for r in range(n_pages)],
            page_copies(b, slot))


def _decode_row(k_refs, v_refs, q_ref, kn_ref, vn_ref, dbias_ref, lq1_ref, lk1_ref, lq2_ref, lk2_ref, sub_ref,
                o_ref, *, lam_init):
    n_pages = len(k_refs)
    d = q_ref.shape[-1]
    dv = d // DA_HEADS
    dh = dv // 2
    nmap = 2 * DA_HEADS
    page = k_refs[0].shape[0]
    rows = page * DA_HEADS

    def rows2d(ref):
        return ref[...].reshape(rows, dv).astype(BF16)

    def head_rows(x):
        return jnp.concatenate([x[:, h * dv:(h + 1) * dv] for h in range(DA_HEADS)]
                               + [jnp.zeros((nmap - DA_HEADS, dv), x.dtype)], axis=0)

    def own_head(n):
        j = lax.broadcasted_iota(jnp.int32, (nmap, n), 0)
        r = lax.broadcasted_iota(jnp.int32, (nmap, n), 1)
        return r % DA_HEADS == j // 2

    q = q_ref[0].astype(F32)
    jrow = lax.broadcasted_iota(jnp.int32, (nmap, dv), 0)
    lane = lax.broadcasted_iota(jnp.int32, (nmap, dv), 1)
    qmat = jnp.zeros((nmap, dv), F32)
    for h in range(DA_HEADS):
        qh = jnp.broadcast_to(q[:, h * dv:(h + 1) * dv], (nmap, dv))
        qmat = jnp.where(jnp.logical_and(jrow // 2 == h, lane // dh == jrow % 2), qh, qmat)
    qmat = qmat.astype(BF16)

    kn8 = head_rows(kn_ref[0]).astype(BF16)
    vn8 = head_rows(vn_ref[0])
    s_new = lax.dot_general(qmat, kn8, _NT, preferred_element_type=F32) + dbias_ref[:, rows:rows + nmap]
    col = lax.broadcasted_iota(jnp.int32, (nmap, nmap), 1)
    s_new = jnp.where(jnp.logical_and(own_head(nmap), col < DA_HEADS), s_new, NEG_BIG)
    m = jnp.max(s_new, axis=1, keepdims=True)
    p_new = jnp.exp2(s_new - m)
    lsum = jnp.sum(p_new, axis=1, keepdims=True)
    acc = jnp.zeros((nmap, dv), F32)
    for h in range(DA_HEADS):
        acc = acc + p_new[:, h:h + 1] * vn8[h:h + 1, :]

    valid = own_head(rows)
    group = min(8, n_pages)
    assert n_pages % group == 0
    for g0 in range(0, n_pages, group):
        s_grp = []
        for r in range(g0, g0 + group):
            s = lax.dot_general(qmat, rows2d(k_refs[r]), _NT, preferred_element_type=F32)
            if r == n_pages - 1:
                s = s + dbias_ref[:, 0:rows]
            s_grp.append(jnp.where(valid, s, NEG_BIG))
        smax = s_grp[0]
        for s in s_grp[1:]:
            smax = jnp.maximum(smax, s)
        m_new = jnp.maximum(m, jnp.max(smax, axis=1, keepdims=True))
        alpha = jnp.exp2(m - m_new)
        psum = jnp.zeros((nmap, rows), F32)
        acc_g = jnp.zeros(acc.shape, F32)
        for r in range(g0, g0 + group):
            p = jnp.exp2(s_grp[r - g0] - m_new)
            psum = psum + p
            acc_g = acc_g + jnp.dot(p.astype(BF16), rows2d(v_refs[r]), preferred_element_type=F32)
        lsum = alpha * lsum + jnp.sum(psum, axis=1, keepdims=True)
        acc = alpha * acc + acc_g
        m = m_new
    an = acc * (1.0 / lsum)
    lam = _lambda(lq1_ref, lk1_ref, lq2_ref, lk2_ref, lam_init)
    sub = sub_ref[...]
    for h in range(DA_HEADS):
        o = an[2 * h:2 * h + 1, :] - lam * an[2 * h + 1:2 * h + 2, :]
        o_ref[0, :, h * dv:(h + 1) * dv] = (_rms(o, sub) * (1.0 - lam_init)).astype(o_ref.dtype)


def _attn_decode_kernel(pt_ref, q_ref, kn_ref, vn_ref, ck_hbm, cv_hbm, dbias_ref,
                        lq1_ref, lk1_ref, lq2_ref, lk2_ref, sub_ref, o_ref, kbuf, vbuf, sem, *, n_rows, lam_init):
    k_refs, v_refs, pending = _decode_fetch(pl.program_id(0), pt_ref, ck_hbm, cv_hbm, kbuf, vbuf, sem,
                                            n_rows=n_rows)
    for cp in pending:
        cp.wait()
    _decode_row(k_refs, v_refs, q_ref, kn_ref, vn_ref, dbias_ref, lq1_ref, lk1_ref, lq2_ref, lk2_ref, sub_ref,
                o_ref, lam_init=lam_init)


def attn_decode(q, k_new, v_new, cache_k, cache_v, page_table, dbias, lams, subln, *, lam_init):
    bsz, d = q.shape
    n_pages = page_table.shape[1]
    page = cache_k.shape[1]
    assert page == LANES
    dv = d // DA_HEADS
    dl = lams[0].shape[-1]
    row_spec = pl.BlockSpec((1, 1, d), lambda b, pt: (b, 0, 0))
    hbm_spec = pl.BlockSpec(memory_space=pl.ANY)
    lam_spec = pl.BlockSpec((1, dl), lambda b, pt: (0, 0))
    page_buf = pltpu.VMEM((DECODE_BUFS, n_pages, page, DA_HEADS, dv), cache_k.dtype)
    grid_spec = pltpu.PrefetchScalarGridSpec(
        num_scalar_prefetch=1,
        grid=(bsz,),
        in_specs=[row_spec, row_spec, row_spec, hbm_spec, hbm_spec,
                  pl.BlockSpec(dbias.shape, lambda b, pt: (0, 0)),
                  lam_spec, lam_spec, lam_spec, lam_spec,
                  pl.BlockSpec((1, dv), lambda b, pt: (0, 0))],
        out_specs=pl.BlockSpec((1, 1, d), lambda b, pt: (b, 0, 0)),
        scratch_shapes=[page_buf, page_buf, pltpu.SemaphoreType.DMA((DECODE_BUFS,))])
    r3 = lambda a: a.reshape(bsz, 1, d)
    out = pl.pallas_call(
        functools.partial(_attn_decode_kernel, n_rows=bsz, lam_init=lam_init),
        grid_spec=grid_spec,
        out_shape=jax.ShapeDtypeStruct((bsz, 1, d), BF16),
        compiler_params=_cparams(("arbitrary",)),
        name="attn_decode",
    )(page_table, r3(q), r3(k_new), r3(v_new), cache_k, cache_v, dbias, *lams, subln.reshape(1, dv))
    return out.reshape(bsz, d)


def _hgrn_decode_kernel(pt_ref, q_ref, f_ref, i_ref, g_ref, lb_ref, gn_ref, mall_ref,
                        dq_ref, kn_ref, vn_ref, ck_hbm, cv_hbm, dbias_ref, lq1_ref, lk1_ref, lq2_ref, lk2_ref,
                        sub_ref, o_ref, st_ref, do_ref, s_scr, kbuf, vbuf, sem,
                        *, C, levels, nsub, n_rows, lam_init):
    cidx = pl.program_id(1)
    row = pl.program_id(0) * pl.num_programs(1) + cidx
    k_refs, v_refs, pending = _decode_fetch(row, pt_ref, ck_hbm, cv_hbm, kbuf, vbuf, sem, n_rows=n_rows)
    _hgrn_chunks(q_ref, f_ref, i_ref, g_ref, lb_ref, gn_ref, mall_ref, o_ref, st_ref, s_scr,
                 first=cidx == 0, last=cidx == pl.num_programs(1) - 1, C=C, levels=levels, nsub=nsub)
    for cp in pending:
        cp.wait()
    _decode_row(k_refs, v_refs, dq_ref, kn_ref, vn_ref, dbias_ref, lq1_ref, lk1_ref, lq2_ref, lk2_ref, sub_ref,
                do_ref, lam_init=lam_init)


def hgrn_prompt_decode(q, f, i, g, lb, gnorm, dq, k_new, v_new, cache_k, cache_v, page_table, dbias, lams, subln,
                       *, C, nsub, lam_init):
    bsz, s, d = q.shape
    n_rows = dq.shape[0]
    dk = d // HG_HEADS
    dv = d // DA_HEADS
    tb = C * nsub
    steps = s // tb
    assert bsz * steps == n_rows
    n_pages = page_table.shape[1]
    page = cache_k.shape[1]
    mall_np, levels = _hgrn_sum_matrices(C)
    mall = jnp.asarray(mall_np, dtype=BF16)
    dl = lams[0].shape[-1]
    blk = pl.BlockSpec((1, tb, d), lambda b, c, pt: (b, c, 0))
    const = lambda shape: pl.BlockSpec(shape, lambda b, c, pt: (0,) * len(shape))
    row_spec = pl.BlockSpec((1, 1, d), lambda b, c, pt: (b * steps + c, 0, 0))
    hbm_spec = pl.BlockSpec(memory_space=pl.ANY)
    page_buf = pltpu.VMEM((2, n_pages, page, DA_HEADS, dv), cache_k.dtype)
    grid_spec = pltpu.PrefetchScalarGridSpec(
        num_scalar_prefetch=1,
        grid=(bsz, steps),
        in_specs=[blk, blk, blk, blk, const((1, d)), const((1, d)), const(mall.shape),
                  row_spec, row_spec, row_spec, hbm_spec, hbm_spec, const(dbias.shape),
                  const((1, dl)), const((1, dl)), const((1, dl)), const((1, dl)), const((1, dv))],
        out_specs=[pl.BlockSpec((1, tb, d), lambda b, c, pt: (b, c, 0)),
                   pl.BlockSpec((1, 1, HG_HEADS, dk, dk), lambda b, c, pt: (0, b, 0, 0, 0)),
                   row_spec],
        scratch_shapes=[pltpu.VMEM((HG_HEADS, dk, dk), F32), page_buf, page_buf,
                        pltpu.SemaphoreType.DMA((2,))])
    r3 = lambda a: a.reshape(n_rows, 1, d)
    o, state, o_dec = pl.pallas_call(
        functools.partial(_hgrn_decode_kernel, C=C, levels=levels, nsub=nsub, n_rows=n_rows, lam_init=lam_init),
        grid_spec=grid_spec,
        out_shape=[jax.ShapeDtypeStruct((bsz, s, d), BF16),
                   jax.ShapeDtypeStruct((1, bsz, HG_HEADS, dk, dk), F32),
                   jax.ShapeDtypeStruct((n_rows, 1, d), BF16)],
        compiler_params=_cparams(("arbitrary", "arbitrary")),
        name="hgrn_prompt_decode",
    )(page_table, q, f, i, g, lb, gnorm.reshape(1, d), mall, r3(dq), r3(k_new), r3(v_new), cache_k, cache_v,
      dbias, *lams, subln.reshape(1, dv))
    return o, state, o_dec.reshape(n_rows, d)


DECODE_BUFS = 3
ATTN_TQ = 512
ATTN_TK = 512
HGRN_C = 64
HGRN_NSUB = 8
TM = 512


def _in_proj(x, p):
    return proj(x, [(p["norm_mix"][0], (p["w_in_a"], 0), 1.0, [[(F32, 0)]] * 4)], tm=TM, name="in_proj")


def _after_mixer0(o, x, p, scale):
    x = mix_ffn(o, x, p["w_out_a"], p["norm_ffn"][0], p["w_gate_up"], p["w_down"], p["norm_final"],
                layer=0, tm=TM, final_norm=False, name="mix_ffn0")
    k32, k16, v32, v16, qa = proj(
        x, [(p["kv_norm"], (p["w_kv"], 0), 1.0, [[(F32, DA_HEADS), (BF16, 0)]] * 2),
            (p["norm_mix"][1], (p["w_q_b"], 0), scale * LOG2E, [[(BF16, 0)]])],
        tm=TM, name="kvq_proj")
    return x, k32, k16, v32, v16, qa


def _after_mixer1(o, x, p):
    return mix_ffn(o, x, p["w_out_b"], p["norm_ffn"][1], p["w_gate_up"], p["w_down"], p["norm_final"],
                   layer=1, tm=TM, final_norm=True, name="mix_ffn1")


def kernel(x_prompt, x_sample, cache_k, cache_v, state_hgrn, page_table, w_in_a, lower_bound, gnorm_a, w_out_a,
           w_q_b, lambda_q1, lambda_k1, lambda_q2, lambda_k2, subln_b, w_out_b, kv_norm, w_kv, rel_bias,
           norm_mix, norm_ffn, w_gate_up, w_down, norm_final):
    bf = lambda w: (w if w.ndim == 3 else w[None]).astype(BF16)
    p = dict(w_in_a=w_in_a, w_out_a=bf(w_out_a), w_q_b=bf(w_q_b), w_out_b=bf(w_out_b), kv_norm=kv_norm,
             w_kv=bf(w_kv), norm_mix=norm_mix, norm_ffn=norm_ffn, w_gate_up=w_gate_up, w_down=w_down,
             norm_final=norm_final)
    bsz, s, d = x_prompt.shape
    nb = x_sample.shape[0]
    dv = d // DA_HEADS
    scale = (dv // 2) ** -0.5
    lam_init = 0.8 - 0.6 * math.exp(-0.3 * 1)
    lams = [a[0].reshape(1, -1) for a in (lambda_q1, lambda_k1, lambda_q2, lambda_k2)]
    gnorm, subln = gnorm_a[0], subln_b[0]
    tiles, dbias, lb = param_tables(rel_bias, lower_bound, cache_k.shape[1])
    xp = x_prompt.reshape(bsz * s, d)
    xs = x_sample.reshape(nb, d)
    r3 = lambda a: a.reshape(bsz, s, d)

    q, f, i, g = _in_proj(xs, p)
    o_s, st_s = hgrn_step(q, f, i, g, lb, gnorm, state_hgrn, bb=16)
    xs, k_s, _, v_s, _, qa_s = _after_mixer0(o_s, xs, p, scale)

    q, f, i, g = _in_proj(xp, p)
    tokens_per_row = (bsz * s) // nb
    if (bsz * s) % nb == 0 and tokens_per_row % HGRN_C == 0 and s % tokens_per_row == 0:
        o_p, st_p, o_dec = hgrn_prompt_decode(
            r3(q), r3(f), r3(i), r3(g), lb, gnorm, qa_s, k_s.reshape(nb, d), v_s.reshape(nb, d), cache_k, cache_v,
            page_table, dbias, lams, subln, C=HGRN_C, nsub=tokens_per_row // HGRN_C, lam_init=lam_init)
    else:
        o_p, st_p = hgrn_prompt(r3(q), r3(f), r3(i), r3(g), lb, gnorm, C=HGRN_C, nsub=HGRN_NSUB)
        o_dec = attn_decode(qa_s, k_s.reshape(nb, d), v_s.reshape(nb, d), cache_k, cache_v, page_table, dbias,
                            lams, subln, lam_init=lam_init)
    xp, k_p, k16, v_p, v16, qa_p = _after_mixer0(o_p.reshape(bsz * s, d), xp, p, scale)

    o_a = attn_prompt(r3(qa_p), r3(k16), r3(v16), tiles, lams, subln,
                      tq=min(ATTN_TQ, s), tk=min(ATTN_TK, s), lam_init=lam_init)
    y_p = _after_mixer1(o_a.reshape(bsz * s, d), xp, p)
    y_s = _after_mixer1(o_dec, xs, p)
    hk = DA_HEADS, dv
    return (y_p.reshape(bsz, s, d), y_s.reshape(nb, 1, d),
            k_p.reshape(bsz, s, *hk), v_p.reshape(bsz, s, *hk), st_p,
            k_s.reshape(nb, 1, *hk), v_s.reshape(nb, 1, *hk), st_s)
```
